```python
import numpy as np
import jax
import jax.numpy as jnp
from jax import lax

D_MODEL = 1024
BATCH = 8
SEQ = 2048
DEPTH = 2

GRID_W = 64
CTX_LEN = 256
Q_BLOCK = 128
ROPE_THETA = 10000.0
NORM_EPS = 1e-6
N_MOD = 6

GQA_HEADS = 8
GQA_KV_HEADS = 2
GQA_GROUP = GQA_HEADS // GQA_KV_HEADS
GQA_HEAD_DIM = 64
GQA_WIDTH = GQA_HEADS * GQA_HEAD_DIM

LRU_WIDTH = 512
LRU_BLOCKS = 8
LRU_BLOCK_W = LRU_WIDTH // LRU_BLOCKS
CONV_WIDTH = 4
CONV_PAD = (1, 2)
LRU_C = 8.0

MLA_HEADS = 8
MLA_NOPE = 64
MLA_ROPE = 32
MLA_V = 64
MLA_Q_LORA = 384
MLA_KV_LORA = 256
MLA_WIDTH = MLA_HEADS * MLA_V

N_BRANCHES = 3
IN_WIDTHS = (GQA_WIDTH, GQA_KV_HEADS * GQA_HEAD_DIM, GQA_KV_HEADS * GQA_HEAD_DIM, LRU_WIDTH, LRU_WIDTH,
             MLA_Q_LORA, MLA_KV_LORA, MLA_ROPE, N_BRANCHES * D_MODEL)
IN_WIDTH = sum(IN_WIDTHS)

N_EXPERTS = 16
N_GROUPS = 4
EXPERTS_PER_GROUP = N_EXPERTS // N_GROUPS
TOPK_GROUPS = 1
GROUP_SCORE_TOPK = 2
TOP_K = 2
EXPERT_FF = 512
ROUTED_SCALE = 1.0

kernel_name = 'hybrid_gqa_rglru_mla_moe_prefix'


def rms_norm(x, g):
    xf = x.astype(jnp.float32)
    y = xf * lax.rsqrt(jnp.mean(xf * xf, axis=-1, keepdims=True) + NORM_EPS)
    return (y * g.astype(jnp.float32)).astype(x.dtype)


def modulate(h, shift, scale):
    return h * (1 + scale) + shift


def axial_rope_tables(rows, dim):
    r, col = jnp.meshgrid(jnp.arange(rows, dtype=jnp.float32), jnp.arange(GRID_W, dtype=jnp.float32), indexing='ij')
    quarter = dim // 4
    inv_freq = ROPE_THETA ** (-jnp.arange(quarter, dtype=jnp.float32) / quarter)
    ang_r = r.reshape(-1, 1) * inv_freq
    ang_c = col.reshape(-1, 1) * inv_freq
    return (jnp.cos(ang_r), jnp.sin(ang_r), jnp.cos(ang_c), jnp.sin(ang_c))


def _rotate(x, cos, sin):
    x1, x2 = jnp.split(x, 2, axis=-1)
    cos = cos[None, :, None, :].astype(x.dtype)
    sin = sin[None, :, None, :].astype(x.dtype)
    return jnp.concatenate([x1 * cos - x2 * sin, x2 * cos + x1 * sin], axis=-1)


def apply_axial_rope(x, tables):
    cr, sr, cc, sc = tables
    x_row, x_col = jnp.split(x, 2, axis=-1)
    return jnp.concatenate([_rotate(x_row, cr, sr), _rotate(x_col, cc, sc)], axis=-1)


def block_attention(q, k, v, scale):
    b, sq, hkv, g, dk = q.shape
    nb = sq // Q_BLOCK
    qb = jnp.moveaxis(q.reshape(b, nb, Q_BLOCK, hkv, g, dk), 1, 0)

    def one_block(qblk):
        s = jnp.einsum('bqhgd,bkhd->bhgqk', qblk, k, preferred_element_type=jnp.float32) * scale
        p = jax.nn.softmax(s, axis=-1).astype(v.dtype)
        return jnp.einsum('bhgqk,bkhe->bqhge', p, v)

    o = lax.map(one_block, qb)
    return jnp.moveaxis(o, 0, 1).reshape(b, sq, -1)


def centred_dwconv(x, w, b):
    y = lax.conv_general_dilated(x, w[:, None, :].astype(x.dtype), window_strides=(1,), padding=[CONV_PAD],
                                 dimension_numbers=('NWC', 'WIO', 'NWC'), feature_group_count=x.shape[-1])
    return y + b


def rglru_coeffs(u, w_a, b_a, w_i, b_i, lam):
    bsz, t, width = u.shape
    uf = u.astype(jnp.float32)
    ub = uf.reshape(bsz, t, LRU_BLOCKS, LRU_BLOCK_W)
    r = jax.nn.sigmoid(jnp.einsum('btnk,nkj->btnj', ub, w_a.astype(jnp.float32)).reshape(bsz, t, width)
                       + b_a.astype(jnp.float32))
    i = jax.nn.sigmoid(jnp.einsum('btnk,nkj->btnj', ub, w_i.astype(jnp.float32)).reshape(bsz, t, width)
                       + b_i.astype(jnp.float32))
    log_a = -LRU_C * r * jax.nn.softplus(-lam.astype(jnp.float32))
    a = jnp.exp(log_a)
    gated_x = jnp.sqrt(-jnp.expm1(2.0 * log_a)) * (i * uf)
    return a, gated_x


def _scan_combine(left, right):
    a_l, b_l = left
    a_r, b_r = right
    return a_l * a_r, a_r * b_l + b_r


def linear_recurrence(a, b, h0, reverse):
    edge = -1 if reverse else 0
    b = b.at[:, edge].add(a[:, edge] * h0)
    _, h = lax.associative_scan(_scan_combine, (a, b), reverse=reverse, axis=1)
    return h


def token_mixer(hc, hl, need_ctx, rope_gqa, rope_mla, w_in, gqa_q_norm, gqa_k_norm, conv_w, conv_b,
                lru_w_a, lru_b_a, lru_w_i, lru_b_i, lru_lam, mla_q_a_norm, mla_w_qb, mla_kv_a_norm, mla_w_kvb,
                w_branch_attn, w_branch_lru, w_branch_mla, w_out):
    bsz, n_ctx, _ = hc.shape
    n_lat = hl.shape[1]
    h = jnp.concatenate([hc, hl], axis=1)
    z = h @ w_in
    zq, zk, zv, zx, zy, zcq, zckv, zkr, zg = jnp.split(z, np.cumsum(IN_WIDTHS)[:-1].tolist(), axis=-1)
    q0 = 0 if need_ctx else n_ctx
    lq = n_ctx - q0
    nq = n_ctx + n_lat - q0

    q = rms_norm(zq[:, q0:].reshape(bsz, nq, GQA_HEADS, GQA_HEAD_DIM), gqa_q_norm)
    k = rms_norm(zk.reshape(bsz, -1, GQA_KV_HEADS, GQA_HEAD_DIM), gqa_k_norm)
    v = zv.reshape(bsz, -1, GQA_KV_HEADS, GQA_HEAD_DIM)
    k = jnp.concatenate([k[:, :n_ctx], apply_axial_rope(k[:, n_ctx:], rope_gqa)], axis=1)
    q_lat = apply_axial_rope(q[:, lq:], rope_gqa).reshape(bsz, n_lat, GQA_KV_HEADS, GQA_GROUP, GQA_HEAD_DIM)
    gqa_scale = GQA_HEAD_DIM ** -0.5
    o_attn = block_attention(q_lat, k, v, gqa_scale)
    if need_ctx:
        q_ctx = q[:, :lq].reshape(bsz, n_ctx, GQA_KV_HEADS, GQA_GROUP, GQA_HEAD_DIM)
        o_attn = jnp.concatenate([block_attention(q_ctx, k[:, :n_ctx], v[:, :n_ctx], gqa_scale), o_attn], axis=1)

    u_ctx = centred_dwconv(zx[:, :n_ctx], conv_w, conv_b)
    u_lat = centred_dwconv(zx[:, n_ctx:], conv_w, conv_b)
    lat_states = []
    ctx_states = []
    for d, reverse in enumerate((False, True)):
        params = (lru_w_a[d], lru_b_a[d], lru_w_i[d], lru_b_i[d], lru_lam[d])
        a_c, x_c = rglru_coeffs(u_ctx, *params)
        hs_c = linear_recurrence(a_c, x_c, jnp.zeros_like(x_c[:, 0]), reverse)
        h_end = hs_c[:, 0] if reverse else hs_c[:, -1]
        a_l, x_l = rglru_coeffs(u_lat, *params)
        lat_states.append(linear_recurrence(a_l, x_l, h_end, reverse))
        if need_ctx:
            ctx_states.append(hs_c)
    rec = lat_states[0] + lat_states[1]
    if need_ctx:
        rec = jnp.concatenate([ctx_states[0] + ctx_states[1], rec], axis=1)
    o_lru = rec.astype(h.dtype) * jax.nn.gelu(zy[:, q0:], approximate=True)

    cq = rms_norm(zcq[:, q0:], mla_q_a_norm)
    qm = (cq @ mla_w_qb).reshape(bsz, nq, MLA_HEADS, MLA_NOPE + MLA_ROPE)
    ckv = rms_norm(zckv, mla_kv_a_norm)
    kv = (ckv @ mla_w_kvb).reshape(bsz, -1, MLA_HEADS, MLA_NOPE + MLA_V)
    k_nope, vm = jnp.split(kv, [MLA_NOPE], axis=-1)
    k_rope = zkr[:, :, None, :]
    k_rope = jnp.concatenate([k_rope[:, :n_ctx], apply_axial_rope(k_rope[:, n_ctx:], rope_mla)], axis=1)
    km = jnp.concatenate([k_nope, jnp.broadcast_to(k_rope, k_nope.shape[:-1] + (MLA_ROPE,))], axis=-1)
    q_nope, q_rope = jnp.split(qm, [MLA_NOPE], axis=-1)
    q_lat_m = jnp.concatenate([q_nope[:, lq:], apply_axial_rope(q_rope[:, lq:], rope_mla)], axis=-1)
    mla_scale = (MLA_NOPE + MLA_ROPE) ** -0.5
    o_mla = block_attention(q_lat_m[:, :, :, None, :], km, vm, mla_scale)
    if need_ctx:
        o_mla_ctx = block_attention(qm[:, :lq][:, :, :, None, :], km[:, :n_ctx], vm[:, :n_ctx], mla_scale)
        o_mla = jnp.concatenate([o_mla_ctx, o_mla], axis=1)

    g_attn, g_lru, g_mla = jnp.split(jax.nn.sigmoid(zg[:, q0:]), N_BRANCHES, axis=-1)
    merged = (g_attn * (o_attn @ w_branch_attn) + g_lru * (o_lru @ w_branch_lru)
              + g_mla * (o_mla @ w_branch_mla))
    y = merged @ w_out
    return y[:, :lq], y[:, lq:]


def moe_ffn(h, router_w, router_bias, w_gate, w_up, w_down):
    shape = h.shape
    t = h.reshape(-1, shape[-1])
    scores = jax.nn.sigmoid(jnp.matmul(t, router_w, preferred_element_type=jnp.float32))
    sel = scores + router_bias.astype(jnp.float32)
    grp = sel.reshape(-1, N_GROUPS, EXPERTS_PER_GROUP)
    group_score = jnp.sum(lax.top_k(grp, GROUP_SCORE_TOPK)[0], axis=-1)
    _, g_idx = lax.top_k(group_score, TOPK_GROUPS)
    group_mask = jnp.sum(jax.nn.one_hot(g_idx, N_GROUPS, dtype=jnp.float32), axis=1)
    expert_mask = jnp.repeat(group_mask, EXPERTS_PER_GROUP, axis=-1)
    masked = jnp.where(expert_mask > 0, sel, -jnp.inf)
    _, e_idx = lax.top_k(masked, TOP_K)
    w = jnp.take_along_axis(scores, e_idx, axis=-1)
    w = ROUTED_SCALE * w / jnp.sum(w, axis=-1, keepdims=True)
    combine = jnp.sum(jax.nn.one_hot(e_idx, N_EXPERTS, dtype=jnp.float32) * w[..., None], axis=1).astype(h.dtype)
    gate = jnp.einsum('nd,edf->nef', t, w_gate)
    up = jnp.einsum('nd,edf->nef', t, w_up)
    act = jax.nn.silu(gate) * up * combine[..., None]
    y = jnp.einsum('nef,efd->nd', act, w_down)
    return y.reshape(shape)


def setup_inputs(seed: int = 0) -> dict:
    key = jax.random.key(seed)
    ks = iter(jax.random.split(key, 40))
    d = D_MODEL

    def nrm(shape, scale):
        return jax.random.normal(next(ks), shape, jnp.float32) * scale

    x = nrm((BATCH, SEQ, d), 1.0)
    c = nrm((BATCH, d), 1.0)
    ctx = nrm((BATCH, CTX_LEN, d), 1.0)
    c_ctx = nrm((d,), 1.0)
    w_mod = nrm((DEPTH, d, N_MOD * d), 0.5 * d ** -0.5)
    b_mod = nrm((DEPTH, N_MOD * d), 0.02)
    norm_mix = 1.0 + nrm((DEPTH, d), 0.05)
    norm_ffn = 1.0 + nrm((DEPTH, d), 0.05)
    w_in = nrm((DEPTH, d, IN_WIDTH), d ** -0.5)
    gqa_q_norm = 1.0 + nrm((DEPTH, GQA_HEAD_DIM), 0.05)
    gqa_k_norm = 1.0 + nrm((DEPTH, GQA_HEAD_DIM), 0.05)
    conv_w = nrm((DEPTH, CONV_WIDTH, LRU_WIDTH), CONV_WIDTH ** -0.5)
    conv_b = nrm((DEPTH, LRU_WIDTH), 0.02)
    lru_w_a = nrm((DEPTH, 2, LRU_BLOCKS, LRU_BLOCK_W, LRU_BLOCK_W), LRU_BLOCK_W ** -0.5)
    lru_b_a = nrm((DEPTH, 2, LRU_WIDTH), 0.02)
    lru_w_i = nrm((DEPTH, 2, LRU_BLOCKS, LRU_BLOCK_W, LRU_BLOCK_W), LRU_BLOCK_W ** -0.5)
    lru_b_i = nrm((DEPTH, 2, LRU_WIDTH), 0.02)
    a8 = jax.random.uniform(next(ks), (DEPTH, 2, LRU_WIDTH), jnp.float32, minval=0.9, maxval=0.999)
    s = a8 ** 0.125
    lru_lam = jnp.log(s) - jnp.log1p(-s)
    mla_q_a_norm = 1.0 + nrm((DEPTH, MLA_Q_LORA), 0.05)
    mla_w_qb = nrm((DEPTH, MLA_Q_LORA, MLA_HEADS * (MLA_NOPE + MLA_ROPE)), MLA_Q_LORA ** -0.5)
    mla_kv_a_norm = 1.0 + nrm((DEPTH, MLA_KV_LORA), 0.05)
    mla_w_kvb = nrm((DEPTH, MLA_KV_LORA, MLA_HEADS * (MLA_NOPE + MLA_V)), MLA_KV_LORA ** -0.5)
    w_branch_attn = nrm((DEPTH, GQA_WIDTH, d), GQA_WIDTH ** -0.5)
    w_branch_lru = nrm((DEPTH, LRU_WIDTH, d), LRU_WIDTH ** -0.5)
    w_branch_mla = nrm((DEPTH, MLA_WIDTH, d), MLA_WIDTH ** -0.5)
    w_out = nrm((DEPTH, d, d), d ** -0.5)
    router_w = nrm((d, N_EXPERTS), d ** -0.5)
    router_bias = nrm((N_EXPERTS,), 0.01)
    moe_w_gate = nrm((DEPTH, N_EXPERTS, d, EXPERT_FF), d ** -0.5)
    moe_w_up = nrm((DEPTH, N_EXPERTS, d, EXPERT_FF), d ** -0.5)
    moe_w_down = nrm((DEPTH, N_EXPERTS, EXPERT_FF, d), EXPERT_FF ** -0.5)
    final_norm = 1.0 + nrm((d,), 0.05)
    return {'x': x, 'c': c, 'ctx': ctx, 'c_ctx': c_ctx, 'w_mod': w_mod, 'b_mod': b_mod,
            'norm_mix': norm_mix, 'norm_ffn': norm_ffn, 'w_in': w_in,
            'gqa_q_norm': gqa_q_norm, 'gqa_k_norm': gqa_k_norm, 'conv_w': conv_w, 'conv_b': conv_b,
            'lru_w_a': lru_w_a, 'lru_b_a': lru_b_a, 'lru_w_i': lru_w_i, 'lru_b_i': lru_b_i, 'lru_lam': lru_lam,
            'mla_q_a_norm': mla_q_a_norm, 'mla_w_qb': mla_w_qb, 'mla_kv_a_norm': mla_kv_a_norm,
            'mla_w_kvb': mla_w_kvb, 'w_branch_attn': w_branch_attn, 'w_branch_lru': w_branch_lru,
            'w_branch_mla': w_branch_mla, 'w_out': w_out, 'router_w': router_w, 'router_bias': router_bias,
            'moe_w_gate': moe_w_gate, 'moe_w_up': moe_w_up, 'moe_w_down': moe_w_down, 'final_norm': final_norm}


def reference(x, c, ctx, c_ctx, w_mod, b_mod, norm_mix, norm_ffn, w_in, gqa_q_norm, gqa_k_norm, conv_w, conv_b,
              lru_w_a, lru_b_a, lru_w_i, lru_b_i, lru_lam, mla_q_a_norm, mla_w_qb, mla_kv_a_norm, mla_w_kvb,
              w_branch_attn, w_branch_lru, w_branch_mla, w_out, router_w, router_bias,
              moe_w_gate, moe_w_up, moe_w_down, final_norm):
    rows = x.shape[1] // GRID_W
    rope_gqa = axial_rope_tables(rows, GQA_HEAD_DIM)
    rope_mla = axial_rope_tables(rows, MLA_ROPE)
    act_c = jax.nn.silu(c)[:, None, :]
    act_cc = jax.nn.silu(c_ctx)[None, None, :]
    n_ctx = ctx.shape[1]
    xc, xl = ctx, x
    for layer in range(DEPTH):
        need_ctx = layer < DEPTH - 1
        mod_l = jnp.split(act_c @ w_mod[layer] + b_mod[layer], N_MOD, axis=-1)
        mod_c = jnp.split(act_cc @ w_mod[layer] + b_mod[layer], N_MOD, axis=-1)
        hc = modulate(rms_norm(xc, norm_mix[layer]), mod_c[0], mod_c[1])
        hl = modulate(rms_norm(xl, norm_mix[layer]), mod_l[0], mod_l[1])
        yc, yl = token_mixer(hc, hl, need_ctx, rope_gqa, rope_mla, w_in[layer], gqa_q_norm[layer],
                             gqa_k_norm[layer], conv_w[layer], conv_b[layer], lru_w_a[layer], lru_b_a[layer],
                             lru_w_i[layer], lru_b_i[layer], lru_lam[layer], mla_q_a_norm[layer],
                             mla_w_qb[layer], mla_kv_a_norm[layer], mla_w_kvb[layer], w_branch_attn[layer],
                             w_branch_lru[layer], w_branch_mla[layer], w_out[layer])
        xl = xl + mod_l[2] * yl
        hl = modulate(rms_norm(xl, norm_ffn[layer]), mod_l[3], mod_l[4])
        if need_ctx:
            xc = xc + mod_c[2] * yc
            hc = modulate(rms_norm(xc, norm_ffn[layer]), mod_c[3], mod_c[4])
            f = moe_ffn(jnp.concatenate([hc, hl], axis=1), router_w, router_bias,
                        moe_w_gate[layer], moe_w_up[layer], moe_w_down[layer])
            xc = xc + mod_c[5] * f[:, :n_ctx]
            xl = xl + mod_l[5] * f[:, n_ctx:]
        else:
            xl = xl + mod_l[5] * moe_ffn(hl, router_w, router_bias, moe_w_gate[layer], moe_w_up[layer],
                                         moe_w_down[layer])
    return rms_norm(xl, final_norm)
```

```python
import functools
import math

import numpy as np
import jax
import jax.numpy as jnp
from jax import lax
from jax.experimental import pallas as pl
from jax.experimental.pallas import tpu as pltpu

F32 = jnp.float32
BF16 = jnp.bfloat16

GRID_W = 64
ROPE_THETA = 10000.0
NORM_EPS = 1e-6
N_MOD = 6
GQA_HEADS = 8
GQA_KV_HEADS = 2
GQA_GROUP = GQA_HEADS // GQA_KV_HEADS
MLA_HEADS = 8
MLA_NOPE = 64
MLA_ROPE = 32
MLA_V = 64
MLA_QK = MLA_NOPE + MLA_ROPE
CONV_WIDTH = 4
LRU_C = 8.0
N_GROUPS = 4
ROUTED_SCALE = 1.0

V7X_LANES = 128
V7X_SUBLANES = 8
V7X_VMEM_LIMIT_BYTES = 56 * 1024 * 1024

ROW_TILE = 256
ATTN_Q_TILE = 128
MOE_ROW_TILE = 1024
MOD_ROWS = 16


def _cparams(sem, vmem=None):
    return pltpu.CompilerParams(dimension_semantics=sem, vmem_limit_bytes=vmem)


def _tile(n, pref):
    t = min(n, pref)
    while n % t or t % V7X_SUBLANES:
        t -= 1
    return t


def _sigmoid(x):
    return 1.0 / (1.0 + jnp.exp(-x))


def _rms(x):
    return x * lax.rsqrt(jnp.mean(x * x, axis=-1, keepdims=True) + NORM_EPS)


def _rope_lanes(x, cos, sin_up, sin_dn, half):
    outs = []
    for c in range(x.shape[-1] // V7X_LANES):
        sl = slice(c * V7X_LANES, (c + 1) * V7X_LANES)
        xc = x[:, sl]
        up = pltpu.roll(xc, V7X_LANES - half, 1)
        dn = pltpu.roll(xc, half, 1)
        outs.append(xc * cos[:, sl] + up * sin_up[:, sl] + dn * sin_dn[:, sl])
    return outs[0] if len(outs) == 1 else jnp.concatenate(outs, axis=-1)


def _mod_kernel(c_ref, w_ref, b_ref, o_ref):
    c = c_ref[...]
    a = (c * _sigmoid(c)).astype(BF16)
    o_ref[0] = jnp.dot(a, w_ref[0].astype(BF16), preferred_element_type=F32) + b_ref[0]


def _modulation(cc, w_mod, b_mod):
    depth, d, n = w_mod.shape
    tn = _tile(n, 1536) if n % V7X_LANES == 0 else n
    return pl.pallas_call(
        _mod_kernel,
        grid=(depth, n // tn),
        in_specs=[pl.BlockSpec((MOD_ROWS, d), lambda l, j: (0, 0)),
                  pl.BlockSpec((1, d, tn), lambda l, j: (l, 0, j)),
                  pl.BlockSpec((1, 1, tn), lambda l, j: (l, 0, j))],
        out_specs=pl.BlockSpec((1, MOD_ROWS, tn), lambda l, j: (l, 0, j)),
        out_shape=jax.ShapeDtypeStruct((depth, MOD_ROWS, n), F32),
        compiler_params=_cparams(("arbitrary", "arbitrary"), V7X_VMEM_LIMIT_BYTES),
        name="modulation",
    )(cc, w_mod, b_mod.reshape(depth, 1, n))


def _in_proj_kernel(*refs, has_prev, widths):
    if has_prev:
        x_ref, f_ref, pm_ref, g_ref, m_ref, w_ref, xo_ref = refs[:7]
        outs = refs[7:]
        x = x_ref[0] + pm_ref[0, 5:6, :] * f_ref[0]
        xo_ref[0] = x
    else:
        x_ref, g_ref, m_ref, w_ref = refs[:4]
        outs = refs[4:]
        x = x_ref[0]
    h = _rms(x) * g_ref[...]
    h = h * (1.0 + m_ref[0, 1:2, :]) + m_ref[0, 0:1, :]
    hb = h.astype(BF16)
    off = 0
    for k, (o_ref, wd) in enumerate(zip(outs, widths)):
        z = jnp.dot(hb, w_ref[:, off:off + wd], preferred_element_type=F32)
        if k == len(widths) - 1:
            z = _sigmoid(z)
        o_ref[0] = z.astype(o_ref.dtype)
        off += wd


def _in_proj(x, prev, g, mods, w_packed, widths, n_ctx):
    b, t, d = x.shape
    tm = _tile(math.gcd(n_ctx, t), ROW_TILE)
    nc = n_ctx // tm
    row = lambda bb, i: (bb, i, 0)
    mod_idx = lambda bb, i: (jnp.where(i < nc, 0, bb + 1), 0, 0)
    x_spec = pl.BlockSpec((1, tm, d), row)
    m_spec = pl.BlockSpec((1, N_MOD, d), mod_idx)
    in_specs, args = [x_spec], [x]
    out_specs, out_shape = [], []
    if prev is not None:
        in_specs += [x_spec, m_spec]
        args += [prev[0], prev[1]]
        out_specs.append(x_spec)
        out_shape.append(jax.ShapeDtypeStruct((b, t, d), F32))
    in_specs += [pl.BlockSpec((1, d), lambda bb, i: (0, 0)), m_spec,
                 pl.BlockSpec(w_packed.shape, lambda bb, i: (0, 0))]
    args += [g.reshape(1, d), mods, w_packed]
    dtypes = [F32] * (len(widths) - 1) + [BF16]
    for wd, dt in zip(widths, dtypes):
        out_specs.append(pl.BlockSpec((1, tm, wd), row))
        out_shape.append(jax.ShapeDtypeStruct((b, t, wd), dt))
    outs = pl.pallas_call(
        functools.partial(_in_proj_kernel, has_prev=prev is not None, widths=tuple(widths)),
        grid=(b, t // tm),
        in_specs=in_specs, out_specs=out_specs, out_shape=out_shape,
        compiler_params=_cparams(("arbitrary", "arbitrary"), V7X_VMEM_LIMIT_BYTES),
        name="in_proj",
    )(*args)
    if prev is not None:
        return outs[0], outs[1:]
    return x, outs


def _gqa_prep_kernel(z_ref, gq_ref, gk_ref, cos_ref, su_ref, sd_ref, q_ref, k_ref, v_ref, *, hd):
    z = z_ref[0]
    nq = GQA_HEADS * hd
    nk = GQA_KV_HEADS * hd
    cos, su, sd = cos_ref[...], su_ref[...], sd_ref[...]

    def head_norm(a, n_heads):
        return jnp.concatenate([_rms(a[:, h * hd:(h + 1) * hd]) for h in range(n_heads)], axis=-1)

    def tiled(tbl, width):
        reps = width // tbl.shape[-1]
        return tbl if reps == 1 else jnp.concatenate([tbl] * reps, axis=-1)

    q = head_norm(z[:, :nq], GQA_HEADS) * gq_ref[...]
    q = _rope_lanes(q, tiled(cos, nq), tiled(su, nq), tiled(sd, nq), hd // 4)
    q_ref[0] = (q * (hd ** -0.5)).astype(BF16)
    k = head_norm(z[:, nq:nq + nk], GQA_KV_HEADS) * gk_ref[...]
    k = _rope_lanes(k, tiled(cos, nk), tiled(su, nk), tiled(sd, nk), hd // 4)
    v = z[:, nq + nk:nq + 2 * nk]
    for h in range(GQA_KV_HEADS):
        k_ref[0, h] = k[:, h * hd:(h + 1) * hd].astype(BF16)
        v_ref[0, h] = v[:, h * hd:(h + 1) * hd].astype(BF16)


def _gqa_prep(zqkv, gq, gk, tables, n_ctx):
    b, t, w = zqkv.shape
    hd = gq.shape[-1]
    nq, nk = GQA_HEADS * hd, GQA_KV_HEADS * hd
    tm = _tile(math.gcd(n_ctx, t), ROW_TILE)
    tw = tables[0].shape[-1]
    tbl_spec = pl.BlockSpec((tm, tw), lambda i, bb: (i, 0))
    kv_spec = pl.BlockSpec((1, GQA_KV_HEADS, tm, hd), lambda i, bb: (bb, 0, i, 0))
    return pl.pallas_call(
        functools.partial(_gqa_prep_kernel, hd=hd),
        grid=(t // tm, b),
        in_specs=[pl.BlockSpec((1, tm, w), lambda i, bb: (bb, i, 0)),
                  pl.BlockSpec((1, nq), lambda i, bb: (0, 0)),
                  pl.BlockSpec((1, nk), lambda i, bb: (0, 0)),
                  tbl_spec, tbl_spec, tbl_spec],
        out_specs=[pl.BlockSpec((1, tm, nq), lambda i, bb: (bb, i, 0)), kv_spec, kv_spec],
        out_shape=[jax.ShapeDtypeStruct((b, t, nq), BF16),
                   jax.ShapeDtypeStruct((b, GQA_KV_HEADS, t, hd), BF16),
                   jax.ShapeDtypeStruct((b, GQA_KV_HEADS, t, hd), BF16)],
        compiler_params=_cparams(("arbitrary", "arbitrary")),
        name="gqa_prep",
    )(zqkv, jnp.tile(gq, GQA_HEADS).reshape(1, nq), jnp.tile(gk, GQA_KV_HEADS).reshape(1, nk), *tables)


def _mla_prep_kernel(z_ref, gqa_ref, gkv_ref, wq_ref, wkv_ref, qc_ref, qu_ref, qd_ref,
                     kc_ref, ku_ref, kd_ref, q_ref, k_ref, v_ref, *, q_lora, kv_lora):
    z = z_ref[0]
    cq = (_rms(z[:, :q_lora]) * gqa_ref[...]).astype(BF16)
    qm = jnp.dot(cq, wq_ref[...], preferred_element_type=F32)

    def tiled(tbl, width):
        reps = width // tbl.shape[-1]
        return tbl if reps == 1 else jnp.concatenate([tbl] * reps, axis=-1)

    wq = qm.shape[-1]
    qm = _rope_lanes(qm, tiled(qc_ref[...], wq), tiled(qu_ref[...], wq), tiled(qd_ref[...], wq),
                     MLA_ROPE // 4)
    q_ref[0] = (qm * (MLA_QK ** -0.5)).astype(BF16)
    ckv = (_rms(z[:, q_lora:q_lora + kv_lora]) * gkv_ref[...]).astype(BF16)
    kv = jnp.dot(ckv, wkv_ref[...], preferred_element_type=F32)
    kr = _rope_lanes(z[:, q_lora + kv_lora:], kc_ref[...], ku_ref[...], kd_ref[...], MLA_ROPE // 4)
    kr = kr[:, :MLA_ROPE]
    per = MLA_NOPE + MLA_V
    for h in range(MLA_HEADS):
        k_ref[0, h] = jnp.concatenate([kv[:, h * per:h * per + MLA_NOPE], kr], axis=-1).astype(BF16)
        v_ref[0, h] = kv[:, h * per + MLA_NOPE:(h + 1) * per].astype(BF16)


def _mla_prep(zmla, gqa, gkv, wq, wkv, q_tables, k_tables, n_ctx):
    b, t, w = zmla.shape
    q_lora, kv_lora = gqa.shape[-1], gkv.shape[-1]
    tm = _tile(math.gcd(n_ctx, t), ROW_TILE)
    qw = MLA_HEADS * MLA_QK
    const = lambda i, bb: (0, 0)
    qt_spec = pl.BlockSpec((tm, q_tables[0].shape[-1]), lambda i, bb: (i, 0))
    kt_spec = pl.BlockSpec((tm, k_tables[0].shape[-1]), lambda i, bb: (i, 0))
    return pl.pallas_call(
        functools.partial(_mla_prep_kernel, q_lora=q_lora, kv_lora=kv_lora),
        grid=(t // tm, b),
        in_specs=[pl.BlockSpec((1, tm, w), lambda i, bb: (bb, i, 0)),
                  pl.BlockSpec((1, q_lora), const), pl.BlockSpec((1, kv_lora), const),
                  pl.BlockSpec(wq.shape, const), pl.BlockSpec(wkv.shape, const),
                  qt_spec, qt_spec, qt_spec, kt_spec, kt_spec, kt_spec],
        out_specs=[pl.BlockSpec((1, tm, qw), lambda i, bb: (bb, i, 0)),
                   pl.BlockSpec((1, MLA_HEADS, tm, MLA_QK), lambda i, bb: (bb, 0, i, 0)),
                   pl.BlockSpec((1, MLA_HEADS, tm, MLA_V), lambda i, bb: (bb, 0, i, 0))],
        out_shape=[jax.ShapeDtypeStruct((b, t, qw), BF16),
                   jax.ShapeDtypeStruct((b, MLA_HEADS, t, MLA_QK), BF16),
                   jax.ShapeDtypeStruct((b, MLA_HEADS, t, MLA_V), BF16)],
        compiler_params=_cparams(("arbitrary", "arbitrary")),
        name="mla_prep",
    )(zmla, gqa.reshape(1, -1), gkv.reshape(1, -1), wq, wkv, *q_tables, *k_tables)


def _lru_kernel(z_ref, cw_ref, cb_ref, wg_ref, bg_ref, lam_ref, o_ref,
                xpad, a_s, b_s, rec, *, n_ctx, chunk):
    t, width = a_s.shape
    halo = V7X_SUBLANES
    n_chunks = t // chunk
    zero_rows = jnp.zeros((halo, width), F32)
    xpad[0:halo, :] = zero_rows
    xpad[halo + t:halo + t + halo, :] = zero_rows
    for c in range(n_chunks):
        xpad[halo + c * chunk:halo + (c + 1) * chunk, :] = z_ref[0, c * chunk:(c + 1) * chunk, 0:width]

    row = lax.broadcasted_iota(jnp.int32, (chunk, 1), 0)

    def conv_chunk(c):
        r0 = c * chunk
        seg_lo = 0 if r0 < n_ctx else n_ctx
        seg_hi = n_ctx if r0 < n_ctx else t
        u = jnp.zeros((chunk, width), F32) + cb_ref[...]
        for j in range(CONV_WIDTH):
            off = j - 1
            tap = xpad[halo + r0 + off:halo + r0 + off + chunk, :]
            pos = row + (r0 + off)
            ok = (pos >= seg_lo) & (pos < seg_hi)
            u = u + jnp.where(ok, tap, 0.0) * cw_ref[j:j + 1, :]
        return u

    def scan_blocks(lo_blk, hi_blk, h, reverse, first):
        nblk = hi_blk - lo_blk

        def body(i, h):
            blk = (hi_blk - 1 - i) if reverse else (lo_blk + i)
            r = pl.multiple_of(blk * V7X_SUBLANES, V7X_SUBLANES)
            ab = a_s[pl.ds(r, V7X_SUBLANES), :]
            bb = b_s[pl.ds(r, V7X_SUBLANES), :]
            rows = [None] * V7X_SUBLANES
            order = range(V7X_SUBLANES - 1, -1, -1) if reverse else range(V7X_SUBLANES)
            for s in order:
                h = ab[s:s + 1, :] * h + bb[s:s + 1, :]
                rows[s] = h
            hs = jnp.concatenate(rows, axis=0)
            if first:
                rec[pl.ds(r, V7X_SUBLANES), :] = hs
            else:
                rec[pl.ds(r, V7X_SUBLANES), :] = rec[pl.ds(r, V7X_SUBLANES), :] + hs
            return h

        return lax.fori_loop(0, nblk, body, h)

    ctx_blk = n_ctx // V7X_SUBLANES
    all_blk = t // V7X_SUBLANES
    for d in range(2):
        lam = lam_ref[d]
        sp = jnp.maximum(-lam, 0.0) + jnp.log1p(jnp.exp(-jnp.abs(lam)))
        for c in range(n_chunks):
            u = conv_chunk(c)
            g = jnp.dot(u.astype(BF16), wg_ref[d], preferred_element_type=F32) + bg_ref[d]
            r_gate = _sigmoid(g[:, :width])
            i_gate = _sigmoid(g[:, width:])
            log_a = -LRU_C * r_gate * sp
            a_s[c * chunk:(c + 1) * chunk, :] = jnp.exp(log_a)
            th = jnp.tanh(log_a)
            one_minus_a2 = -2.0 * th / (1.0 - th)
            b_s[c * chunk:(c + 1) * chunk, :] = jnp.sqrt(one_minus_a2) * (i_gate * u)
        h0 = jnp.zeros((1, width), F32)
        if d == 0:
            scan_blocks(0, all_blk, h0, False, True)
        else:
            h_ctx = scan_blocks(0, ctx_blk, h0, True, False)
            scan_blocks(ctx_blk, all_blk, h_ctx, True, False)

    k0 = math.sqrt(2.0 / math.pi)
    for c in range(n_chunks):
        y = z_ref[0, c * chunk:(c + 1) * chunk, width:2 * width]
        gelu = 0.5 * y * (1.0 + jnp.tanh(k0 * (y + 0.044715 * (y * y * y))))
        o_ref[0, c * chunk:(c + 1) * chunk, :] = (rec[c * chunk:(c + 1) * chunk, :] * gelu).astype(BF16)


def _lru(zlru, conv_w, conv_b, wg, bg, lam, n_ctx):
    b, t, w2 = zlru.shape
    width = w2 // 2
    chunk = _tile(math.gcd(n_ctx, t), ROW_TILE)
    const3 = lambda bb: (0, 0, 0)
    return pl.pallas_call(
        functools.partial(_lru_kernel, n_ctx=n_ctx, chunk=chunk),
        grid=(b,),
        in_specs=[pl.BlockSpec((1, t, w2), lambda bb: (bb, 0, 0)),
                  pl.BlockSpec((CONV_WIDTH, width), lambda bb: (0, 0)),
                  pl.BlockSpec((1, width), lambda bb: (0, 0)),
                  pl.BlockSpec(wg.shape, const3), pl.BlockSpec(bg.shape, const3),
                  pl.BlockSpec(lam.shape, const3)],
        out_specs=pl.BlockSpec((1, t, width), lambda bb: (bb, 0, 0)),
        out_shape=jax.ShapeDtypeStruct((b, t, width), BF16),
        scratch_shapes=[pltpu.VMEM((t + 2 * V7X_SUBLANES, width), F32),
                        pltpu.VMEM((t, width), F32), pltpu.VMEM((t, width), F32),
                        pltpu.VMEM((t, width), F32)],
        compiler_params=_cparams(("arbitrary",), V7X_VMEM_LIMIT_BYTES),
        name="rg_lru",
    )(zlru, conv_w, conv_b.reshape(1, width), wg, bg, lam)


def _softmax_pv(q, k, v):
    s = lax.dot_general(q, k, (((1,), (1,)), ((), ())), preferred_element_type=F32)
    m = jnp.max(s, axis=-1, keepdims=True)
    p = jnp.exp(s - m)
    l = jnp.sum(p, axis=-1, keepdims=True)
    o = jnp.dot(p.astype(BF16), v, preferred_element_type=F32)
    return o / l


def _attn_kernel(q_ref, k_ref, v_ref, o_ref, *, group, shared_kv, dk, dv, n_ctx, n_ctx_tiles, q_tile_off):
    tq = q_ref.shape[1]
    t = k_ref.shape[2]

    def compute(nk):
        q = q_ref[0]
        if shared_kv:
            qs = jnp.concatenate([q[:, h * dk:(h + 1) * dk] for h in range(group)], axis=0)
            o = _softmax_pv(qs, k_ref[0, 0, 0:nk, :], v_ref[0, 0, 0:nk, :])
            outs = [o[h * tq:(h + 1) * tq, :] for h in range(group)]
        else:
            outs = [_softmax_pv(q[:, h * dk:(h + 1) * dk], k_ref[0, h, 0:nk, :], v_ref[0, h, 0:nk, :])
                    for h in range(group)]
        o_ref[0] = jnp.concatenate(outs, axis=-1).astype(o_ref.dtype)

    if n_ctx_tiles > 0:
        i = pl.program_id(2) + q_tile_off
        pl.when(i < n_ctx_tiles)(lambda: compute(n_ctx))
        pl.when(i >= n_ctx_tiles)(lambda: compute(t))
    else:
        compute(t)


def _attention(q, k, v, *, group, shared_kv, n_ctx, with_ctx, name):
    b, t, qw = q.shape
    dk, dv = k.shape[-1], v.shape[-1]
    heads = qw // dk
    n_groups = heads // group
    tq = _tile(math.gcd(n_ctx, t), ATTN_Q_TILE)
    q_off = 0 if with_ctx else n_ctx // tq
    tq_total = t if with_ctx else t - n_ctx
    kvh = (lambda g: g) if shared_kv else (lambda g: g)
    kv_heads = 1 if shared_kv else group
    kern = functools.partial(_attn_kernel, group=group, shared_kv=shared_kv, dk=dk, dv=dv,
                             n_ctx=n_ctx, n_ctx_tiles=(n_ctx // tq if with_ctx else 0), q_tile_off=0)
    return pl.pallas_call(
        kern,
        grid=(b, n_groups, tq_total // tq),
        in_specs=[pl.BlockSpec((1, tq, group * dk), lambda bb, g, i: (bb, i + q_off, g)),
                  pl.BlockSpec((1, kv_heads, t, dk), lambda bb, g, i: (bb, kvh(g), 0, 0)),
                  pl.BlockSpec((1, kv_heads, t, dv), lambda bb, g, i: (bb, kvh(g), 0, 0))],
        out_specs=pl.BlockSpec((1, tq, group * dv), lambda bb, g, i: (bb, i, g)),
        out_shape=jax.ShapeDtypeStruct((b, tq_total, heads * dv), BF16),
        compiler_params=_cparams(("arbitrary", "arbitrary", "arbitrary"), V7X_VMEM_LIMIT_BYTES),
        name=name,
    )(q, k, v)


def _merge_kernel(x_ref, oa_ref, ol_ref, om_ref, gt_ref, m_ref, g_ref, wa_ref, wl_ref, wm_ref,
                  wo_ref, rh_ref, rl_ref, rb_ref, xo_ref, h_ref, ids_ref, wts_ref, *, d, n_exp):
    gates = gt_ref[0]
    merged = (gates[:, 0:d].astype(F32) * jnp.dot(oa_ref[0], wa_ref[...], preferred_element_type=F32)
              + gates[:, d:2 * d].astype(F32) * jnp.dot(ol_ref[0], wl_ref[...], preferred_element_type=F32)
              + gates[:, 2 * d:3 * d].astype(F32) * jnp.dot(om_ref[0], wm_ref[...], preferred_element_type=F32))
    y = jnp.dot(merged.astype(BF16), wo_ref[...], preferred_element_type=F32)
    x = x_ref[0] + m_ref[0, 2:3, :] * y
    xo_ref[0] = x
    h = _rms(x) * g_ref[...]
    h = h * (1.0 + m_ref[0, 4:5, :]) + m_ref[0, 3:4, :]
    h_hi = h.astype(BF16)
    h_ref[0] = h_hi
    h_lo = (h - h_hi.astype(F32)).astype(BF16)

    dn = (((1,), (1,)), ((), ()))
    logits = (lax.dot_general(rh_ref[...], h_hi, dn, preferred_element_type=F32)
              + lax.dot_general(rh_ref[...], h_lo, dn, preferred_element_type=F32)
              + lax.dot_general(rl_ref[...], h_hi, dn, preferred_element_type=F32))
    scores = _sigmoid(logits)
    sel = scores + rb_ref[...]
    per = n_exp // N_GROUPS
    gs = []
    for g in range(N_GROUPS):
        r = [sel[g * per + j:g * per + j + 1, :] for j in range(per)]
        best = None
        for a in range(per):
            for bq in range(a + 1, per):
                pair = r[a] + r[bq]
                best = pair if best is None else jnp.maximum(best, pair)
        gs.append(best)
    gmax = functools.reduce(jnp.maximum, gs)
    gbest = jnp.full(gmax.shape, N_GROUPS - 1, jnp.int32)
    for g in range(N_GROUPS - 2, -1, -1):
        gbest = jnp.where(gs[g] == gmax, g, gbest)
    eid = lax.broadcasted_iota(jnp.int32, sel.shape, 0)
    gid = jnp.zeros(sel.shape, jnp.int32)
    for g in range(1, N_GROUPS):
        gid = gid + (eid >= g * per).astype(jnp.int32)
    masked = jnp.where(gid == gbest, sel, -jnp.inf)
    m1 = jnp.max(masked, axis=0, keepdims=True)
    i1 = jnp.min(jnp.where(masked == m1, eid, n_exp), axis=0, keepdims=True)
    masked2 = jnp.where(eid == i1, -jnp.inf, masked)
    m2 = jnp.max(masked2, axis=0, keepdims=True)
    i2 = jnp.min(jnp.where(masked2 == m2, eid, n_exp), axis=0, keepdims=True)
    s1 = jnp.sum(jnp.where(eid == i1, scores, 0.0), axis=0, keepdims=True)
    s2 = jnp.sum(jnp.where(eid == i2, scores, 0.0), axis=0, keepdims=True)
    tot = s1 + s2
    ids_ref[0] = jnp.concatenate([i1, i2], axis=0)
    wts_ref[0] = jnp.concatenate([ROUTED_SCALE * s1 / tot, ROUTED_SCALE * s2 / tot], axis=0)


def _merge(x, o_attn, o_lru, o_mla, gates, mods, g_ffn, wa, wl, wm, wo, r_hi, r_lo, r_bias,
           n_ctx, with_ctx):
    b, t, d = x.shape
    n_exp = r_hi.shape[0]
    tm = _tile(math.gcd(n_ctx, t), ROW_TILE)
    off = 0 if with_ctx else n_ctx // tm
    nc = n_ctx // tm if with_ctx else 0
    tq = t if with_ctx else t - n_ctx
    nt = tq // tm
    full = lambda bb, i: (bb, i + off, 0)
    qrow = lambda bb, i: (bb, i, 0)
    const = lambda bb, i: (0, 0)
    mod_idx = lambda bb, i: (jnp.where(i < nc, 0, bb + 1), 0, 0)
    bw = o_attn.shape[-1]
    route_spec = pl.BlockSpec((1, 2, tm), lambda bb, i: (bb * nt + i, 0, 0))
    return pl.pallas_call(
        functools.partial(_merge_kernel, d=d, n_exp=n_exp),
        grid=(b, nt),
        in_specs=[pl.BlockSpec((1, tm, d), full),
                  pl.BlockSpec((1, tm, bw), qrow),
                  pl.BlockSpec((1, tm, o_lru.shape[-1]), full),
                  pl.BlockSpec((1, tm, bw), qrow),
                  pl.BlockSpec((1, tm, 3 * d), full),
                  pl.BlockSpec((1, N_MOD, d), mod_idx),
                  pl.BlockSpec((1, d), const),
                  pl.BlockSpec(wa.shape, const), pl.BlockSpec(wl.shape, const),
                  pl.BlockSpec(wm.shape, const), pl.BlockSpec(wo.shape, const),
                  pl.BlockSpec(r_hi.shape, const), pl.BlockSpec(r_lo.shape, const),
                  pl.BlockSpec(r_bias.shape, const)],
        out_specs=[pl.BlockSpec((1, tm, d), qrow), pl.BlockSpec((1, tm, d), qrow),
                   route_spec, route_spec],
        out_shape=[jax.ShapeDtypeStruct((b, tq, d), F32), jax.ShapeDtypeStruct((b, tq, d), BF16),
                   jax.ShapeDtypeStruct((b * nt, 2, tm), jnp.int32),
                   jax.ShapeDtypeStruct((b * nt, 2, tm), F32)],
        compiler_params=_cparams(("arbitrary", "arbitrary"), V7X_VMEM_LIMIT_BYTES),
        name="merge_router",
    )(x, o_attn, o_lru, o_mla, gates, mods, g_ffn.reshape(1, d), wa, wl, wm, wo, r_hi, r_lo, r_bias)


def _moe_kernel(h_ref, ids_ref, wts_ref, wgu_ref, wd_ref, o_ref, acc, *, ff):
    e = pl.program_id(1)

    @pl.when(e == 0)
    def _():
        acc[...] = jnp.zeros_like(acc)

    ids = ids_ref[...]
    wts = wts_ref[...]
    c = jnp.sum(jnp.where(ids == e, wts, 0.0), axis=-1, keepdims=True)
    gu = jnp.dot(h_ref[...], wgu_ref[0], preferred_element_type=F32)
    g, u = gu[:, :ff], gu[:, ff:]
    act = (g * _sigmoid(g)) * u * c
    acc[...] += jnp.dot(act.astype(BF16), wd_ref[0], preferred_element_type=F32)

    @pl.when(e == pl.num_programs(1) - 1)
    def _():
        o_ref[...] = acc[...]


def _moe(h, ids, wts, wgu, wd):
    n, d = h.shape
    n_exp, _, ff2 = wgu.shape
    tm = _tile(n, MOE_ROW_TILE)
    return pl.pallas_call(
        functools.partial(_moe_kernel, ff=ff2 // 2),
        grid=(n // tm, n_exp),
        in_specs=[pl.BlockSpec((tm, d), lambda i, e: (i, 0)),
                  pl.BlockSpec((tm, 2), lambda i, e: (i, 0)),
                  pl.BlockSpec((tm, 2), lambda i, e: (i, 0)),
                  pl.BlockSpec((1, d, ff2), lambda i, e: (e, 0, 0)),
                  pl.BlockSpec((1, ff2 // 2, d), lambda i, e: (e, 0, 0))],
        out_specs=pl.BlockSpec((tm, d), lambda i, e: (i, 0)),
        out_shape=jax.ShapeDtypeStruct((n, d), F32),
        scratch_shapes=[pltpu.VMEM((tm, d), F32)],
        compiler_params=_cparams(("arbitrary", "arbitrary"), V7X_VMEM_LIMIT_BYTES),
        name="moe_ffn",
    )(h, ids, wts, wgu, wd)


def _final_kernel(x_ref, f_ref, m_ref, g_ref, o_ref):
    x = x_ref[0] + m_ref[0, 5:6, :] * f_ref[0]
    o_ref[0] = _rms(x) * g_ref[...]


def _final(x, f, mods, g):
    b, s, d = x.shape
    tm = _tile(s, 2 * ROW_TILE)
    row = lambda bb, i: (bb, i, 0)
    return pl.pallas_call(
        _final_kernel,
        grid=(b, s // tm),
        in_specs=[pl.BlockSpec((1, tm, d), row), pl.BlockSpec((1, tm, d), row),
                  pl.BlockSpec((1, N_MOD, d), lambda bb, i: (bb + 1, 0, 0)),
                  pl.BlockSpec((1, d), lambda bb, i: (0, 0))],
        out_specs=pl.BlockSpec((1, tm, d), row),
        out_shape=jax.ShapeDtypeStruct((b, s, d), F32),
        compiler_params=_cparams(("arbitrary", "arbitrary")),
        name="final_norm",
    )(x, f, mods, g.reshape(1, d))


def _rope_tables(n_ctx, n_lat, dim, period, lane_off, width):
    quarter = dim // 4
    pos = jnp.arange(n_lat, dtype=F32)
    r, col = jnp.floor(pos / GRID_W), pos - GRID_W * jnp.floor(pos / GRID_W)
    inv_freq = ROPE_THETA ** (-jnp.arange(quarter, dtype=F32) / quarter)
    lane = np.arange(width)
    j = (lane % period) - lane_off
    active = (j >= 0) & (j < dim)
    jj = np.where(active, j, 0)
    use_col = jj >= dim // 2
    upper = (jj % (dim // 2)) >= quarter
    f = jj % quarter
    ang = jnp.where(use_col[None, :], col[:, None], r[:, None]) * inv_freq[f][None, :]
    act = jnp.asarray(active)[None, :]
    cos = jnp.where(act, jnp.cos(ang), 1.0)
    sin = jnp.where(act, jnp.sin(ang), 0.0)
    sin_up = jnp.where(jnp.asarray(~upper)[None, :], -sin, 0.0)
    sin_dn = jnp.where(jnp.asarray(upper)[None, :], sin, 0.0)
    ident = lambda v, fill: jnp.concatenate([jnp.full((n_ctx, width), fill, F32), v], axis=0)
    return ident(cos, 1.0), ident(sin_up, 0.0), ident(sin_dn, 0.0)


def _block_diag(w):
    nb, k, j = w.shape
    eye = jnp.eye(nb, dtype=w.dtype)
    return jnp.einsum('nkj,nm->nkmj', w, eye).reshape(nb * k, nb * j)


def kernel(x, c, ctx, c_ctx, w_mod, b_mod, norm_mix, norm_ffn, w_in, gqa_q_norm, gqa_k_norm, conv_w, conv_b,
           lru_w_a, lru_b_a, lru_w_i, lru_b_i, lru_lam, mla_q_a_norm, mla_w_qb, mla_kv_a_norm, mla_w_kvb,
           w_branch_attn, w_branch_lru, w_branch_mla, w_out, router_w, router_bias,
           moe_w_gate, moe_w_up, moe_w_down, final_norm):
    bsz, n_lat, d = x.shape
    n_ctx = ctx.shape[1]
    t = n_ctx + n_lat
    depth = w_mod.shape[0]
    hd = gqa_q_norm.shape[-1]
    lru_w = conv_w.shape[-1]
    q_lora, kv_lora = mla_q_a_norm.shape[-1], mla_kv_a_norm.shape[-1]
    n_exp = router_w.shape[-1]
    assert bsz + 1 <= MOD_ROWS and d % V7X_LANES == 0

    cc = jnp.zeros((MOD_ROWS, d), F32).at[0].set(c_ctx).at[1:1 + bsz].set(c)
    mods_all = _modulation(cc, w_mod, b_mod).reshape(depth, MOD_ROWS, N_MOD, d)

    gqa_tabs = _rope_tables(n_ctx, n_lat, hd, hd, 0, V7X_LANES)
    mq_period = MLA_QK * V7X_LANES // math.gcd(MLA_QK, V7X_LANES)
    mla_q_tabs = _rope_tables(n_ctx, n_lat, MLA_ROPE, MLA_QK, MLA_NOPE, mq_period)
    mla_k_tabs = _rope_tables(n_ctx, n_lat, MLA_ROPE, V7X_LANES, 0, V7X_LANES)

    wq_, wk_ = GQA_HEADS * hd, GQA_KV_HEADS * hd
    splits = np.cumsum([wq_, wk_, wk_, lru_w, lru_w, q_lora, kv_lora, MLA_ROPE])
    kr_pad = V7X_LANES - MLA_ROPE
    widths = (wq_ + 2 * wk_, 2 * lru_w, q_lora + kv_lora + V7X_LANES, 3 * d)

    r_t = router_w.T
    r_hi = r_t.astype(BF16)
    r_lo = (r_t - r_hi.astype(F32)).astype(BF16)
    r_bias = router_bias.reshape(n_exp, 1).astype(F32)

    xs = jnp.concatenate([ctx, x], axis=1)
    prev = None
    for layer in range(depth):
        last = layer == depth - 1
        mods = mods_all[layer]
        wl = w_in[layer]
        w_packed = jnp.concatenate(
            [wl[:, :splits[7]], jnp.zeros((d, kr_pad), F32), wl[:, splits[7]:]], axis=1).astype(BF16)
        xs, (zqkv, zlru, zmla, gates) = _in_proj(xs, prev, norm_mix[layer], mods, w_packed, widths, n_ctx)

        q_g, k_g, v_g = _gqa_prep(zqkv, gqa_q_norm[layer], gqa_k_norm[layer], gqa_tabs, n_ctx)
        o_attn = _attention(q_g, k_g, v_g, group=GQA_GROUP, shared_kv=True, n_ctx=n_ctx,
                            with_ctx=not last, name="gqa_attention")

        wg = jnp.stack([jnp.concatenate([_block_diag(lru_w_a[layer, dd]), _block_diag(lru_w_i[layer, dd])],
                                        axis=1) for dd in range(2)]).astype(BF16)
        bg = jnp.concatenate([lru_b_a[layer], lru_b_i[layer]], axis=-1).reshape(2, 1, 2 * lru_w)
        o_lru = _lru(zlru, conv_w[layer], conv_b[layer], wg, bg, lru_lam[layer].reshape(2, 1, lru_w), n_ctx)

        q_m, k_m, v_m = _mla_prep(zmla, mla_q_a_norm[layer], mla_kv_a_norm[layer],
                                  mla_w_qb[layer].astype(BF16), mla_w_kvb[layer].astype(BF16),
                                  mla_q_tabs, mla_k_tabs, n_ctx)
        o_mla = _attention(q_m, k_m, v_m, group=MLA_HEADS // 2, shared_kv=False, n_ctx=n_ctx,
                           with_ctx=not last, name="mla_attention")

        x_mid, h2, ids, wts = _merge(
            xs, o_attn, o_lru, o_mla, gates, mods, norm_ffn[layer],
            w_branch_attn[layer].astype(BF16), w_branch_lru[layer].astype(BF16),
            w_branch_mla[layer].astype(BF16), w_out[layer].astype(BF16), r_hi, r_lo, r_bias,
            n_ctx, with_ctx=not last)

        n_tok = x_mid.shape[0] * x_mid.shape[1]
        ids_t = jnp.transpose(ids, (0, 2, 1)).reshape(n_tok, 2)
        wts_t = jnp.transpose(wts, (0, 2, 1)).reshape(n_tok, 2)
        wgu = jnp.concatenate([moe_w_gate[layer], moe_w_up[layer]], axis=-1).astype(BF16)
        f = _moe(h2.reshape(n_tok, d), ids_t, wts_t, wgu, moe_w_down[layer].astype(BF16))
        f = f.reshape(x_mid.shape)
        xs, prev = x_mid, (f, mods)

    return _final(xs, prev[0], prev[1], final_norm)
```

```python
import functools
import math

import numpy as np
import jax
import jax.numpy as jnp
from jax import lax
from jax.experimental import pallas as pl
from jax.experimental.pallas import tpu as pltpu

F32 = jnp.float32
BF16 = jnp.bfloat16

GRID_W = 64
ROPE_THETA = 10000.0
NORM_EPS = 1e-6
N_MOD = 6
GQA_HEADS = 8
GQA_KV_HEADS = 2
GQA_GROUP = GQA_HEADS // GQA_KV_HEADS
MLA_HEADS = 8
MLA_NOPE = 64
MLA_ROPE = 32
MLA_V = 64
MLA_QK = MLA_NOPE + MLA_ROPE
CONV_WIDTH = 4
LRU_C = 8.0
N_GROUPS = 4
ROUTED_SCALE = 1.0
LOG2E = math.log2(math.e)

V7X_LANES = 128
V7X_SUBLANES = 8
V7X_VMEM_LIMIT_BYTES = 56 * 1024 * 1024

ROW_TILE = 256
ATTN_Q_TILE = 512
ATTN_SUB = 128
V_ROWS = 80
MOE_ROW_TILE = 1024
MOD_ROWS = 16


def _cparams(sem, vmem=None):
    return pltpu.CompilerParams(dimension_semantics=sem, vmem_limit_bytes=vmem)


def _tile(n, pref):
    t = min(n, pref)
    while n % t or t % V7X_SUBLANES:
        t -= 1
    return t


def _sigmoid(x):
    return 1.0 / (1.0 + jnp.exp(-x))


def _rms(x):
    return x * lax.rsqrt(jnp.mean(x * x, axis=-1, keepdims=True) + NORM_EPS)


def _rope_lanes(x, cos, sin_up, sin_dn, half):
    outs = []
    for c in range(x.shape[-1] // V7X_LANES):
        sl = slice(c * V7X_LANES, (c + 1) * V7X_LANES)
        xc = x[:, sl]
        up = pltpu.roll(xc, V7X_LANES - half, 1)
        dn = pltpu.roll(xc, half, 1)
        outs.append(xc * cos[:, sl] + up * sin_up[:, sl] + dn * sin_dn[:, sl])
    return outs[0] if len(outs) == 1 else jnp.concatenate(outs, axis=-1)


def _mod_kernel(c_ref, w_ref, b_ref, o_ref):
    c = c_ref[...]
    a = (c * _sigmoid(c)).astype(BF16)
    o_ref[0] = jnp.dot(a, w_ref[0].astype(BF16), preferred_element_type=F32) + b_ref[0]


def _modulation(cc, w_mod, b_mod):
    depth, d, n = w_mod.shape
    tn = _tile(n, 1536) if n % V7X_LANES == 0 else n
    return pl.pallas_call(
        _mod_kernel,
        grid=(depth, n // tn),
        in_specs=[pl.BlockSpec((MOD_ROWS, d), lambda l, j: (0, 0)),
                  pl.BlockSpec((1, d, tn), lambda l, j: (l, 0, j)),
                  pl.BlockSpec((1, 1, tn), lambda l, j: (l, 0, j))],
        out_specs=pl.BlockSpec((1, MOD_ROWS, tn), lambda l, j: (l, 0, j)),
        out_shape=jax.ShapeDtypeStruct((depth, MOD_ROWS, n), F32),
        compiler_params=_cparams(("arbitrary", "arbitrary"), V7X_VMEM_LIMIT_BYTES),
        name="modulation",
    )(cc, w_mod, b_mod.reshape(depth, 1, n))


def _in_proj_kernel(*refs, has_prev, widths):
    if has_prev:
        x_ref, f_ref, pm_ref, g_ref, m_ref, w_ref, xo_ref = refs[:7]
        outs = refs[7:]
        x = x_ref[0] + pm_ref[0, 5:6, :] * f_ref[0]
        xo_ref[0] = x
    else:
        x_ref, g_ref, m_ref, w_ref = refs[:4]
        outs = refs[4:]
        x = x_ref[0]
    h = _rms(x) * g_ref[...]
    h = h * (1.0 + m_ref[0, 1:2, :]) + m_ref[0, 0:1, :]
    hb = h.astype(BF16)
    off = 0
    for k, (o_ref, wd) in enumerate(zip(outs, widths)):
        z = jnp.dot(hb, w_ref[:, off:off + wd], preferred_element_type=F32)
        if k == len(widths) - 1:
            z = _sigmoid(z)
        o_ref[0] = z.astype(o_ref.dtype)
        off += wd


def _in_proj(x, prev, g, mods, w_packed, widths, n_ctx):
    b, t, d = x.shape
    tm = _tile(math.gcd(n_ctx, t), ROW_TILE)
    nc = n_ctx // tm
    row = lambda bb, i: (bb, i, 0)
    mod_idx = lambda bb, i: (jnp.where(i < nc, 0, bb + 1), 0, 0)
    x_spec = pl.BlockSpec((1, tm, d), row)
    m_spec = pl.BlockSpec((1, N_MOD, d), mod_idx)
    in_specs, args = [x_spec], [x]
    out_specs, out_shape = [], []
    if prev is not None:
        in_specs += [x_spec, m_spec]
        args += [prev[0], prev[1]]
        out_specs.append(x_spec)
        out_shape.append(jax.ShapeDtypeStruct((b, t, d), F32))
    in_specs += [pl.BlockSpec((1, d), lambda bb, i: (0, 0)), m_spec,
                 pl.BlockSpec(w_packed.shape, lambda bb, i: (0, 0))]
    args += [g.reshape(1, d), mods, w_packed]
    dtypes = [F32] * (len(widths) - 1) + [BF16]
    for wd, dt in zip(widths, dtypes):
        out_specs.append(pl.BlockSpec((1, tm, wd), row))
        out_shape.append(jax.ShapeDtypeStruct((b, t, wd), dt))
    outs = pl.pallas_call(
        functools.partial(_in_proj_kernel, has_prev=prev is not None, widths=tuple(widths)),
        grid=(b, t // tm),
        in_specs=in_specs, out_specs=out_specs, out_shape=out_shape,
        compiler_params=_cparams(("arbitrary", "arbitrary"), V7X_VMEM_LIMIT_BYTES),
        name="in_proj",
    )(*args)
    if prev is not None:
        return outs[0], outs[1:]
    return x, outs


def _ones_row_tail(rows, cols):
    r = lax.broadcasted_iota(jnp.int32, (rows, cols), 0)
    return jnp.where(r == 0, 1.0, 0.0).astype(BF16)


def _gqa_prep_kernel(z_ref, gq_ref, gk_ref, cos_ref, su_ref, sd_ref, q_ref, k_ref, v_ref, *, hd):
    z = z_ref[0]
    nq = GQA_HEADS * hd
    nk = GQA_KV_HEADS * hd
    cos, su, sd = cos_ref[...], su_ref[...], sd_ref[...]

    def head_norm(a, n_heads):
        return jnp.concatenate([_rms(a[:, h * hd:(h + 1) * hd]) for h in range(n_heads)], axis=-1)

    def tiled(tbl, width):
        reps = width // tbl.shape[-1]
        return tbl if reps == 1 else jnp.concatenate([tbl] * reps, axis=-1)

    q = head_norm(z[:, :nq], GQA_HEADS) * gq_ref[...]
    q = _rope_lanes(q, tiled(cos, nq), tiled(su, nq), tiled(sd, nq), hd // 4)
    q_ref[0] = (q * (LOG2E * hd ** -0.5)).astype(BF16)
    k = head_norm(z[:, nq:nq + nk], GQA_KV_HEADS) * gk_ref[...]
    k = _rope_lanes(k, tiled(cos, nk), tiled(su, nk), tiled(sd, nk), hd // 4)
    v_t = z[:, nq + nk:nq + 2 * nk].T
    tail = _ones_row_tail(V_ROWS - hd, z.shape[0])
    for h in range(GQA_KV_HEADS):
        k_ref[0, h] = k[:, h * hd:(h + 1) * hd].astype(BF16)
        v_ref[0, h, 0:hd, :] = v_t[h * hd:(h + 1) * hd, :].astype(BF16)
        v_ref[0, h, hd:V_ROWS, :] = tail


def _gqa_prep(zqkv, gq, gk, tables, n_ctx):
    b, t, w = zqkv.shape
    hd = gq.shape[-1]
    nq, nk = GQA_HEADS * hd, GQA_KV_HEADS * hd
    tm = _tile(math.gcd(n_ctx, t), ROW_TILE)
    tw = tables[0].shape[-1]
    tbl_spec = pl.BlockSpec((tm, tw), lambda i, bb: (i, 0))
    kv_spec = pl.BlockSpec((1, GQA_KV_HEADS, tm, hd), lambda i, bb: (bb, 0, i, 0))
    vt_spec = pl.BlockSpec((1, GQA_KV_HEADS, V_ROWS, tm), lambda i, bb: (bb, 0, 0, i))
    return pl.pallas_call(
        functools.partial(_gqa_prep_kernel, hd=hd),
        grid=(t // tm, b),
        in_specs=[pl.BlockSpec((1, tm, w), lambda i, bb: (bb, i, 0)),
                  pl.BlockSpec((1, nq), lambda i, bb: (0, 0)),
                  pl.BlockSpec((1, nk), lambda i, bb: (0, 0)),
                  tbl_spec, tbl_spec, tbl_spec],
        out_specs=[pl.BlockSpec((1, tm, nq), lambda i, bb: (bb, i, 0)), kv_spec, vt_spec],
        out_shape=[jax.ShapeDtypeStruct((b, t, nq), BF16),
                   jax.ShapeDtypeStruct((b, GQA_KV_HEADS, t, hd), BF16),
                   jax.ShapeDtypeStruct((b, GQA_KV_HEADS, V_ROWS, t), BF16)],
        compiler_params=_cparams(("arbitrary", "arbitrary")),
        name="gqa_prep",
    )(zqkv, jnp.tile(gq, GQA_HEADS).reshape(1, nq), jnp.tile(gk, GQA_KV_HEADS).reshape(1, nk), *tables)


def _mla_prep_kernel(z_ref, gqa_ref, gkv_ref, wq_ref, wkv_ref, qc_ref, qu_ref, qd_ref,
                     kc_ref, ku_ref, kd_ref, q_ref, k_ref, v_ref, *, q_lora, kv_lora):
    z = z_ref[0]
    cq = (_rms(z[:, :q_lora]) * gqa_ref[...]).astype(BF16)
    qm = jnp.dot(cq, wq_ref[...], preferred_element_type=F32)

    def tiled(tbl, width):
        reps = width // tbl.shape[-1]
        return tbl if reps == 1 else jnp.concatenate([tbl] * reps, axis=-1)

    wq = qm.shape[-1]
    qm = _rope_lanes(qm, tiled(qc_ref[...], wq), tiled(qu_ref[...], wq), tiled(qd_ref[...], wq),
                     MLA_ROPE // 4)
    q_ref[0] = (qm * (LOG2E * MLA_QK ** -0.5)).astype(BF16)
    ckv = (_rms(z[:, q_lora:q_lora + kv_lora]) * gkv_ref[...]).astype(BF16)
    kv = jnp.dot(ckv, wkv_ref[...], preferred_element_type=F32)
    kr = _rope_lanes(z[:, q_lora + kv_lora:], kc_ref[...], ku_ref[...], kd_ref[...], MLA_ROPE // 4)
    kr = kr[:, :MLA_ROPE]
    per = MLA_NOPE + MLA_V
    tail = _ones_row_tail(V_ROWS - MLA_V, z.shape[0])
    for h in range(MLA_HEADS):
        k_ref[0, h] = jnp.concatenate([kv[:, h * per:h * per + MLA_NOPE], kr], axis=-1).astype(BF16)
        head_t = kv[:, h * per:(h + 1) * per].T
        v_ref[0, h, 0:MLA_V, :] = head_t[MLA_NOPE:per, :].astype(BF16)
        v_ref[0, h, MLA_V:V_ROWS, :] = tail


def _mla_prep(zmla, gqa, gkv, wq, wkv, q_tables, k_tables, n_ctx):
    b, t, w = zmla.shape
    q_lora, kv_lora = gqa.shape[-1], gkv.shape[-1]
    tm = _tile(math.gcd(n_ctx, t), ROW_TILE)
    qw = MLA_HEADS * MLA_QK
    const = lambda i, bb: (0, 0)
    qt_spec = pl.BlockSpec((tm, q_tables[0].shape[-1]), lambda i, bb: (i, 0))
    kt_spec = pl.BlockSpec((tm, k_tables[0].shape[-1]), lambda i, bb: (i, 0))
    return pl.pallas_call(
        functools.partial(_mla_prep_kernel, q_lora=q_lora, kv_lora=kv_lora),
        grid=(t // tm, b),
        in_specs=[pl.BlockSpec((1, tm, w), lambda i, bb: (bb, i, 0)),
                  pl.BlockSpec((1, q_lora), const), pl.BlockSpec((1, kv_lora), const),
                  pl.BlockSpec(wq.shape, const), pl.BlockSpec(wkv.shape, const),
                  qt_spec, qt_spec, qt_spec, kt_spec, kt_spec, kt_spec],
        out_specs=[pl.BlockSpec((1, tm, qw), lambda i, bb: (bb, i, 0)),
                   pl.BlockSpec((1, MLA_HEADS, tm, MLA_QK), lambda i, bb: (bb, 0, i, 0)),
                   pl.BlockSpec((1, MLA_HEADS, V_ROWS, tm), lambda i, bb: (bb, 0, 0, i))],
        out_shape=[jax.ShapeDtypeStruct((b, t, qw), BF16),
                   jax.ShapeDtypeStruct((b, MLA_HEADS, t, MLA_QK), BF16),
                   jax.ShapeDtypeStruct((b, MLA_HEADS, V_ROWS, t), BF16)],
        compiler_params=_cparams(("arbitrary", "arbitrary")),
        name="mla_prep",
    )(zmla, gqa.reshape(1, -1), gkv.reshape(1, -1), wq, wkv, *q_tables, *k_tables)


def _lru_kernel(z_ref, cw_ref, cb_ref, wg_ref, bg_ref, lam_ref, o_ref,
                xpad, a_s, b_s, rec, *, n_ctx, chunk):
    t, width = a_s.shape
    halo = V7X_SUBLANES
    n_chunks = t // chunk
    zero_rows = jnp.zeros((halo, width), F32)
    xpad[0:halo, :] = zero_rows
    xpad[halo + t:halo + t + halo, :] = zero_rows
    for c in range(n_chunks):
        xpad[halo + c * chunk:halo + (c + 1) * chunk, :] = z_ref[0, c * chunk:(c + 1) * chunk, 0:width]

    row = lax.broadcasted_iota(jnp.int32, (chunk, 1), 0)

    def conv_chunk(c):
        r0 = c * chunk
        seg_lo = 0 if r0 < n_ctx else n_ctx
        seg_hi = n_ctx if r0 < n_ctx else t
        u = jnp.zeros((chunk, width), F32) + cb_ref[...]
        for j in range(CONV_WIDTH):
            off = j - 1
            tap = xpad[halo + r0 + off:halo + r0 + off + chunk, :]
            pos = row + (r0 + off)
            ok = (pos >= seg_lo) & (pos < seg_hi)
            u = u + jnp.where(ok, tap, 0.0) * cw_ref[j:j + 1, :]
        return u

    def scan_blocks(lo_blk, hi_blk, h, reverse, first):
        nblk = hi_blk - lo_blk

        def body(i, h):
            blk = (hi_blk - 1 - i) if reverse else (lo_blk + i)
            r = pl.multiple_of(blk * V7X_SUBLANES, V7X_SUBLANES)
            ab = a_s[pl.ds(r, V7X_SUBLANES), :]
            bb = b_s[pl.ds(r, V7X_SUBLANES), :]
            rows = [None] * V7X_SUBLANES
            order = range(V7X_SUBLANES - 1, -1, -1) if reverse else range(V7X_SUBLANES)
            for s in order:
                h = ab[s:s + 1, :] * h + bb[s:s + 1, :]
                rows[s] = h
            hs = jnp.concatenate(rows, axis=0)
            if first:
                rec[pl.ds(r, V7X_SUBLANES), :] = hs
            else:
                rec[pl.ds(r, V7X_SUBLANES), :] = rec[pl.ds(r, V7X_SUBLANES), :] + hs
            return h

        return lax.fori_loop(0, nblk, body, h)

    ctx_blk = n_ctx // V7X_SUBLANES
    all_blk = t // V7X_SUBLANES
    for d in range(2):
        lam = lam_ref[d]
        sp = jnp.maximum(-lam, 0.0) + jnp.log1p(jnp.exp(-jnp.abs(lam)))
        for c in range(n_chunks):
            u = conv_chunk(c)
            g = jnp.dot(u.astype(BF16), wg_ref[d], preferred_element_type=F32) + bg_ref[d]
            r_gate = _sigmoid(g[:, :width])
            i_gate = _sigmoid(g[:, width:])
            log_a = -LRU_C * r_gate * sp
            a_s[c * chunk:(c + 1) * chunk, :] = jnp.exp(log_a)
            th = jnp.tanh(log_a)
            one_minus_a2 = -2.0 * th / (1.0 - th)
            b_s[c * chunk:(c + 1) * chunk, :] = jnp.sqrt(one_minus_a2) * (i_gate * u)
        h0 = jnp.zeros((1, width), F32)
        if d == 0:
            scan_blocks(0, all_blk, h0, False, True)
        else:
            h_ctx = scan_blocks(0, ctx_blk, h0, True, False)
            scan_blocks(ctx_blk, all_blk, h_ctx, True, False)

    k0 = math.sqrt(2.0 / math.pi)
    for c in range(n_chunks):
        y = z_ref[0, c * chunk:(c + 1) * chunk, width:2 * width]
        gelu = 0.5 * y * (1.0 + jnp.tanh(k0 * (y + 0.044715 * (y * y * y))))
        o_ref[0, c * chunk:(c + 1) * chunk, :] = (rec[c * chunk:(c + 1) * chunk, :] * gelu).astype(BF16)


def _lru(zlru, conv_w, conv_b, wg, bg, lam, n_ctx):
    b, t, w2 = zlru.shape
    width = w2 // 2
    chunk = _tile(math.gcd(n_ctx, t), ROW_TILE)
    const3 = lambda bb: (0, 0, 0)
    return pl.pallas_call(
        functools.partial(_lru_kernel, n_ctx=n_ctx, chunk=chunk),
        grid=(b,),
        in_specs=[pl.BlockSpec((1, t, w2), lambda bb: (bb, 0, 0)),
                  pl.BlockSpec((CONV_WIDTH, width), lambda bb: (0, 0)),
                  pl.BlockSpec((1, width), lambda bb: (0, 0)),
                  pl.BlockSpec(wg.shape, const3), pl.BlockSpec(bg.shape, const3),
                  pl.BlockSpec(lam.shape, const3)],
        out_specs=pl.BlockSpec((1, t, width), lambda bb: (bb, 0, 0)),
        out_shape=jax.ShapeDtypeStruct((b, t, width), BF16),
        scratch_shapes=[pltpu.VMEM((t + 2 * V7X_SUBLANES, width), F32),
                        pltpu.VMEM((t, width), F32), pltpu.VMEM((t, width), F32),
                        pltpu.VMEM((t, width), F32)],
        compiler_params=_cparams(("arbitrary",), V7X_VMEM_LIMIT_BYTES),
        name="rg_lru",
    )(zlru, conv_w, conv_b.reshape(1, width), wg, bg, lam)


def _attn_kernel(q_ref, k_ref, vt_ref, o_ref, s0, s1, p0, p1, *, gqa, dk, dv, n_ctx, n_lat, with_ctx, tq):
    t = n_ctx + n_lat
    sbuf, pbuf = (s0, s1), (p0, p1)
    out_off = 0 if with_ctx else n_ctx
    nt_dims = (((1,), (1,)), ((), ()))

    def n_sub(rows):
        return GQA_KV_HEADS * (rows // ATTN_SUB) if gqa else MLA_HEADS

    def load_q(r0, j, rows):
        if gqa:
            kvh, piece = j % GQA_KV_HEADS, j // GQA_KV_HEADS
            blk = q_ref[0, pl.ds(r0 + piece * ATTN_SUB, ATTN_SUB), kvh * GQA_GROUP * dk:(kvh + 1) * GQA_GROUP * dk]
            return jnp.concatenate([blk[:, h * dk:(h + 1) * dk] for h in range(GQA_GROUP)], axis=0), kvh
        return q_ref[0, pl.ds(r0, rows), j * dk:(j + 1) * dk], j

    def qk(r0, j, rows, n_keys, par):
        q, kh = load_q(r0, j, rows)
        sbuf[par][0:n_keys, 0:q.shape[0]] = lax.dot_general(
            k_ref[0, kh, 0:n_keys, :], q, nt_dims, preferred_element_type=F32)

    def finish(r0, j, rows, n_keys, par):
        cols = GQA_GROUP * ATTN_SUB if gqa else rows
        kh = j % GQA_KV_HEADS if gqa else j
        s = sbuf[par][0:n_keys, 0:cols]
        m = jnp.max(s, axis=0, keepdims=True)
        pbuf[par][0:n_keys, 0:cols] = jnp.exp2(s - m).astype(BF16)
        ot = jnp.dot(vt_ref[0, kh, :, 0:n_keys], pbuf[par][0:n_keys, 0:cols], preferred_element_type=F32)
        o = (ot[0:dv, :] / ot[dv:dv + 1, :]).T.astype(o_ref.dtype)
        if gqa:
            piece = j // GQA_KV_HEADS
            out = jnp.concatenate([o[h * ATTN_SUB:(h + 1) * ATTN_SUB, :] for h in range(GQA_GROUP)], axis=-1)
            o_ref[0, pl.ds(r0 - out_off + piece * ATTN_SUB, ATTN_SUB),
                  kh * GQA_GROUP * dv:(kh + 1) * GQA_GROUP * dv] = out
        else:
            o_ref[0, pl.ds(r0 - out_off, rows), j * dv:(j + 1) * dv] = o

    if with_ctx:
        rows_c = min(tq, n_ctx)
        for r in range(0, n_ctx, rows_c):
            for j in range(n_sub(rows_c)):
                qk(r, j, rows_c, n_ctx, 0)
                finish(r, j, rows_c, n_ctx, 0)

    n_tiles = n_lat // tq
    ns = n_sub(tq)
    assert ns % 2 == 0
    qk(n_ctx, 0, tq, t, 0)

    def body(i, carry):
        r0 = pl.multiple_of(n_ctx + i * tq, ATTN_SUB)
        r_next = pl.multiple_of(n_ctx + jnp.minimum(i + 1, n_tiles - 1) * tq, ATTN_SUB)
        for j in range(ns):
            if j + 1 < ns:
                qk(r0, j + 1, tq, t, (j + 1) % 2)
            else:
                qk(r_next, 0, tq, t, 0)
            finish(r0, j, tq, t, j % 2)
        return carry

    lax.fori_loop(0, n_tiles, body, 0)


def _attention(q, k, vt, *, gqa, n_ctx, with_ctx, name):
    b, t, qw = q.shape
    hk, dk = k.shape[1], k.shape[-1]
    dv = MLA_V if not gqa else dk
    n_lat = t - n_ctx
    tq = _tile(n_lat, ATTN_Q_TILE)
    cols = GQA_GROUP * ATTN_SUB if gqa else tq
    t_out = t if with_ctx else n_lat
    w_out = (qw // dk) * dv
    kern = functools.partial(_attn_kernel, gqa=gqa, dk=dk, dv=dv, n_ctx=n_ctx, n_lat=n_lat,
                             with_ctx=with_ctx, tq=tq)
    return pl.pallas_call(
        kern,
        grid=(b,),
        in_specs=[pl.BlockSpec((1, t, qw), lambda bb: (bb, 0, 0)),
                  pl.BlockSpec((1, hk, t, dk), lambda bb: (bb, 0, 0, 0)),
                  pl.BlockSpec((1, hk, V_ROWS, t), lambda bb: (bb, 0, 0, 0))],
        out_specs=pl.BlockSpec((1, t_out, w_out), lambda bb: (bb, 0, 0)),
        out_shape=jax.ShapeDtypeStruct((b, t_out, w_out), BF16),
        scratch_shapes=[pltpu.VMEM((t, cols), F32), pltpu.VMEM((t, cols), F32),
                        pltpu.VMEM((t, cols), BF16), pltpu.VMEM((t, cols), BF16)],
        compiler_params=_cparams(("arbitrary",), V7X_VMEM_LIMIT_BYTES),
        name=name,
    )(q, k, vt)


def _merge_kernel(x_ref, oa_ref, ol_ref, om_ref, gt_ref, m_ref, g_ref, wa_ref, wl_ref, wm_ref,
                  wo_ref, rh_ref, rl_ref, rb_ref, xo_ref, h_ref, ids_ref, wts_ref, *, d, n_exp):
    gates = gt_ref[0]
    merged = (gates[:, 0:d].astype(F32) * jnp.dot(oa_ref[0], wa_ref[...], preferred_element_type=F32)
              + gates[:, d:2 * d].astype(F32) * jnp.dot(ol_ref[0], wl_ref[...], preferred_element_type=F32)
              + gates[:, 2 * d:3 * d].astype(F32) * jnp.dot(om_ref[0], wm_ref[...], preferred_element_type=F32))
    y = jnp.dot(merged.astype(BF16), wo_ref[...], preferred_element_type=F32)
    x = x_ref[0] + m_ref[0, 2:3, :] * y
    xo_ref[0] = x
    h = _rms(x) * g_ref[...]
    h = h * (1.0 + m_ref[0, 4:5, :]) + m_ref[0, 3:4, :]
    h_hi = h.astype(BF16)
    h_ref[0] = h_hi
    h_lo = (h - h_hi.astype(F32)).astype(BF16)

    dn = (((1,), (1,)), ((), ()))
    logits = (lax.dot_general(rh_ref[...], h_hi, dn, preferred_element_type=F32)
              + lax.dot_general(rh_ref[...], h_lo, dn, preferred_element_type=F32)
              + lax.dot_general(rl_ref[...], h_hi, dn, preferred_element_type=F32))
    scores = _sigmoid(logits)
    sel = scores + rb_ref[...]
    per = n_exp // N_GROUPS
    gs = []
    for g in range(N_GROUPS):
        r = [sel[g * per + j:g * per + j + 1, :] for j in range(per)]
        best = None
        for a in range(per):
            for bq in range(a + 1, per):
                pair = r[a] + r[bq]
                best = pair if best is None else jnp.maximum(best, pair)
        gs.append(best)
    gmax = functools.reduce(jnp.maximum, gs)
    gbest = jnp.full(gmax.shape, N_GROUPS - 1, jnp.int32)
    for g in range(N_GROUPS - 2, -1, -1):
        gbest = jnp.where(gs[g] == gmax, g, gbest)
    eid = lax.broadcasted_iota(jnp.int32, sel.shape, 0)
    gid = jnp.zeros(sel.shape, jnp.int32)
    for g in range(1, N_GROUPS):
        gid = gid + (eid >= g * per).astype(jnp.int32)
    masked = jnp.where(gid == gbest, sel, -jnp.inf)
    m1 = jnp.max(masked, axis=0, keepdims=True)
    i1 = jnp.min(jnp.where(masked == m1, eid, n_exp), axis=0, keepdims=True)
    masked2 = jnp.where(eid == i1, -jnp.inf, masked)
    m2 = jnp.max(masked2, axis=0, keepdims=True)
    i2 = jnp.min(jnp.where(masked2 == m2, eid, n_exp), axis=0, keepdims=True)
    s1 = jnp.sum(jnp.where(eid == i1, scores, 0.0), axis=0, keepdims=True)
    s2 = jnp.sum(jnp.where(eid == i2, scores, 0.0), axis=0, keepdims=True)
    tot = s1 + s2
    ids_ref[0] = jnp.concatenate([i1, i2], axis=0)
    wts_ref[0] = jnp.concatenate([ROUTED_SCALE * s1 / tot, ROUTED_SCALE * s2 / tot], axis=0)


def _merge(x, o_attn, o_lru, o_mla, gates, mods, g_ffn, wa, wl, wm, wo, r_hi, r_lo, r_bias,
           n_ctx, with_ctx):
    b, t, d = x.shape
    n_exp = r_hi.shape[0]
    tm = _tile(math.gcd(n_ctx, t), ROW_TILE)
    off = 0 if with_ctx else n_ctx // tm
    nc = n_ctx // tm if with_ctx else 0
    tq = t if with_ctx else t - n_ctx
    nt = tq // tm
    full = lambda bb, i: (bb, i + off, 0)
    qrow = lambda bb, i: (bb, i, 0)
    const = lambda bb, i: (0, 0)
    mod_idx = lambda bb, i: (jnp.where(i < nc, 0, bb + 1), 0, 0)
    bw = o_attn.shape[-1]
    route_spec = pl.BlockSpec((1, 2, tm), lambda bb, i: (bb * nt + i, 0, 0))
    return pl.pallas_call(
        functools.partial(_merge_kernel, d=d, n_exp=n_exp),
        grid=(b, nt),
        in_specs=[pl.BlockSpec((1, tm, d), full),
                  pl.BlockSpec((1, tm, bw), qrow),
                  pl.BlockSpec((1, tm, o_lru.shape[-1]), full),
                  pl.BlockSpec((1, tm, bw), qrow),
                  pl.BlockSpec((1, tm, 3 * d), full),
                  pl.BlockSpec((1, N_MOD, d), mod_idx),
                  pl.BlockSpec((1, d), const),
                  pl.BlockSpec(wa.shape, const), pl.BlockSpec(wl.shape, const),
                  pl.BlockSpec(wm.shape, const), pl.BlockSpec(wo.shape, const),
                  pl.BlockSpec(r_hi.shape, const), pl.BlockSpec(r_lo.shape, const),
                  pl.BlockSpec(r_bias.shape, const)],
        out_specs=[pl.BlockSpec((1, tm, d), qrow), pl.BlockSpec((1, tm, d), qrow),
                   route_spec, route_spec],
        out_shape=[jax.ShapeDtypeStruct((b, tq, d), F32), jax.ShapeDtypeStruct((b, tq, d), BF16),
                   jax.ShapeDtypeStruct((b * nt, 2, tm), jnp.int32),
                   jax.ShapeDtypeStruct((b * nt, 2, tm), F32)],
        compiler_params=_cparams(("arbitrary", "arbitrary"), V7X_VMEM_LIMIT_BYTES),
        name="merge_router",
    )(x, o_attn, o_lru, o_mla, gates, mods, g_ffn.reshape(1, d), wa, wl, wm, wo, r_hi, r_lo, r_bias)


def _moe_kernel(h_ref, ids_ref, wts_ref, wgu_ref, wd_ref, o_ref, acc, *, ff):
    e = pl.program_id(1)

    @pl.when(e == 0)
    def _():
        acc[...] = jnp.zeros_like(acc)

    ids = ids_ref[...]
    wts = wts_ref[...]
    c = jnp.sum(jnp.where(ids == e, wts, 0.0), axis=-1, keepdims=True)
    gu = jnp.dot(h_ref[...], wgu_ref[0], preferred_element_type=F32)
    g, u = gu[:, :ff], gu[:, ff:]
    act = (g * _sigmoid(g)) * u * c
    acc[...] += jnp.dot(act.astype(BF16), wd_ref[0], preferred_element_type=F32)

    @pl.when(e == pl.num_programs(1) - 1)
    def _():
        o_ref[...] = acc[...]


def _moe(h, ids, wts, wgu, wd):
    n, d = h.shape
    n_exp, _, ff2 = wgu.shape
    tm = _tile(n, MOE_ROW_TILE)
    return pl.pallas_call(
        functools.partial(_moe_kernel, ff=ff2 // 2),
        grid=(n // tm, n_exp),
        in_specs=[pl.BlockSpec((tm, d), lambda i, e: (i, 0)),
                  pl.BlockSpec((tm, 2), lambda i, e: (i, 0)),
                  pl.BlockSpec((tm, 2), lambda i, e: (i, 0)),
                  pl.BlockSpec((1, d, ff2), lambda i, e: (e, 0, 0)),
                  pl.BlockSpec((1, ff2 // 2, d), lambda i, e: (e, 0, 0))],
        out_specs=pl.BlockSpec((tm, d), lambda i, e: (i, 0)),
        out_shape=jax.ShapeDtypeStruct((n, d), F32),
        scratch_shapes=[pltpu.VMEM((tm, d), F32)],
        compiler_params=_cparams(("arbitrary", "arbitrary"), V7X_VMEM_LIMIT_BYTES),
        name="moe_ffn",
    )(h, ids, wts, wgu, wd)


def _final_kernel(x_ref, f_ref, m_ref, g_ref, o_ref):
    x = x_ref[0] + m_ref[0, 5:6, :] * f_ref[0]
    o_ref[0] = _rms(x) * g_ref[...]


def _final(x, f, mods, g):
    b, s, d = x.shape
    tm = _tile(s, 2 * ROW_TILE)
    row = lambda bb, i: (bb, i, 0)
    return pl.pallas_call(
        _final_kernel,
        grid=(b, s // tm),
        in_specs=[pl.BlockSpec((1, tm, d), row), pl.BlockSpec((1, tm, d), row),
                  pl.BlockSpec((1, N_MOD, d), lambda bb, i: (bb + 1, 0, 0)),
                  pl.BlockSpec((1, d), lambda bb, i: (0, 0))],
        out_specs=pl.BlockSpec((1, tm, d), row),
        out_shape=jax.ShapeDtypeStruct((b, s, d), F32),
        compiler_params=_cparams(("arbitrary", "arbitrary")),
        name="final_norm",
    )(x, f, mods, g.reshape(1, d))


def _rope_tables(n_ctx, n_lat, dim, period, lane_off, width):
    quarter = dim // 4
    pos = jnp.arange(n_lat, dtype=F32)
    r, col = jnp.floor(pos / GRID_W), pos - GRID_W * jnp.floor(pos / GRID_W)
    inv_freq = ROPE_THETA ** (-jnp.arange(quarter, dtype=F32) / quarter)
    lane = np.arange(width)
    j = (lane % period) - lane_off
    active = (j >= 0) & (j < dim)
    jj = np.where(active, j, 0)
    use_col = jj >= dim // 2
    upper = (jj % (dim // 2)) >= quarter
    f = jj % quarter
    ang = jnp.where(use_col[None, :], col[:, None], r[:, None]) * inv_freq[f][None, :]
    act = jnp.asarray(active)[None, :]
    cos = jnp.where(act, jnp.cos(ang), 1.0)
    sin = jnp.where(act, jnp.sin(ang), 0.0)
    sin_up = jnp.where(jnp.asarray(~upper)[None, :], -sin, 0.0)
    sin_dn = jnp.where(jnp.asarray(upper)[None, :], sin, 0.0)
    ident = lambda v, fill: jnp.concatenate([jnp.full((n_ctx, width), fill, F32), v], axis=0)
    return ident(cos, 1.0), ident(sin_up, 0.0), ident(sin_dn, 0.0)


def _block_diag(w):
    nb, k, j = w.shape
    eye = jnp.eye(nb, dtype=w.dtype)
    return jnp.einsum('nkj,nm->nkmj', w, eye).reshape(nb * k, nb * j)


def kernel(x, c, ctx, c_ctx, w_mod, b_mod, norm_mix, norm_ffn, w_in, gqa_q_norm, gqa_k_norm, conv_w, conv_b,
           lru_w_a, lru_b_a, lru_w_i, lru_b_i, lru_lam, mla_q_a_norm, mla_w_qb, mla_kv_a_norm, mla_w_kvb,
           w_branch_attn, w_branch_lru, w_branch_mla, w_out, router_w, router_bias,
           moe_w_gate, moe_w_up, moe_w_down, final_norm):
    bsz, n_lat, d = x.shape
    n_ctx = ctx.shape[1]
    t = n_ctx + n_lat
    depth = w_mod.shape[0]
    hd = gqa_q_norm.shape[-1]
    lru_w = conv_w.shape[-1]
    q_lora, kv_lora = mla_q_a_norm.shape[-1], mla_kv_a_norm.shape[-1]
    n_exp = router_w.shape[-1]
    assert bsz + 1 <= MOD_ROWS and d % V7X_LANES == 0

    cc = jnp.zeros((MOD_ROWS, d), F32).at[0].set(c_ctx).at[1:1 + bsz].set(c)
    mods_all = _modulation(cc, w_mod, b_mod).reshape(depth, MOD_ROWS, N_MOD, d)

    gqa_tabs = _rope_tables(n_ctx, n_lat, hd, hd, 0, V7X_LANES)
    mq_period = MLA_QK * V7X_LANES // math.gcd(MLA_QK, V7X_LANES)
    mla_q_tabs = _rope_tables(n_ctx, n_lat, MLA_ROPE, MLA_QK, MLA_NOPE, mq_period)
    mla_k_tabs = _rope_tables(n_ctx, n_lat, MLA_ROPE, V7X_LANES, 0, V7X_LANES)

    wq_, wk_ = GQA_HEADS * hd, GQA_KV_HEADS * hd
    splits = np.cumsum([wq_, wk_, wk_, lru_w, lru_w, q_lora, kv_lora, MLA_ROPE])
    kr_pad = V7X_LANES - MLA_ROPE
    widths = (wq_ + 2 * wk_, 2 * lru_w, q_lora + kv_lora + V7X_LANES, 3 * d)

    r_t = router_w.T
    r_hi = r_t.astype(BF16)
    r_lo = (r_t - r_hi.astype(F32)).astype(BF16)
    r_bias = router_bias.reshape(n_exp, 1).astype(F32)

    xs = jnp.concatenate([ctx, x], axis=1)
    prev = None
    for layer in range(depth):
        last = layer == depth - 1
        mods = mods_all[layer]
        wl = w_in[layer]
        w_packed = jnp.concatenate(
            [wl[:, :splits[7]], jnp.zeros((d, kr_pad), F32), wl[:, splits[7]:]], axis=1).astype(BF16)
        xs, (zqkv, zlru, zmla, gates) = _in_proj(xs, prev, norm_mix[layer], mods, w_packed, widths, n_ctx)

        q_g, k_g, v_g = _gqa_prep(zqkv, gqa_q_norm[layer], gqa_k_norm[layer], gqa_tabs, n_ctx)
        o_attn = _attention(q_g, k_g, v_g, gqa=True, n_ctx=n_ctx, with_ctx=not last, name="gqa_attention")

        wg = jnp.stack([jnp.concatenate([_block_diag(lru_w_a[layer, dd]), _block_diag(lru_w_i[layer, dd])],
                                        axis=1) for dd in range(2)]).astype(BF16)
        bg = jnp.concatenate([lru_b_a[layer], lru_b_i[layer]], axis=-1).reshape(2, 1, 2 * lru_w)
        o_lru = _lru(zlru, conv_w[layer], conv_b[layer], wg, bg, lru_lam[layer].reshape(2, 1, lru_w), n_ctx)

        q_m, k_m, v_m = _mla_prep(zmla, mla_q_a_norm[layer], mla_kv_a_norm[layer],
                                  mla_w_qb[layer].astype(BF16), mla_w_kvb[layer].astype(BF16),
                                  mla_q_tabs, mla_k_tabs, n_ctx)
        o_mla = _attention(q_m, k_m, v_m, gqa=False, n_ctx=n_ctx, with_ctx=not last, name="mla_attention")

        x_mid, h2, ids, wts = _merge(
            xs, o_attn, o_lru, o_mla, gates, mods, norm_ffn[layer],
            w_branch_attn[layer].astype(BF16), w_branch_lru[layer].astype(BF16),
            w_branch_mla[layer].astype(BF16), w_out[layer].astype(BF16), r_hi, r_lo, r_bias,
            n_ctx, with_ctx=not last)

        n_tok = x_mid.shape[0] * x_mid.shape[1]
        ids_t = jnp.transpose(ids, (0, 2, 1)).reshape(n_tok, 2)
        wts_t = jnp.transpose(wts, (0, 2, 1)).reshape(n_tok, 2)
        wgu = jnp.concatenate([moe_w_gate[layer], moe_w_up[layer]], axis=-1).astype(BF16)
        f = _moe(h2.reshape(n_tok, d), ids_t, wts_t, wgu, moe_w_down[layer].astype(BF16))
        f = f.reshape(x_mid.shape)
        xs, prev = x_mid, (f, mods)

    return _final(xs, prev[0], prev[1], final_norm)
```

```python
import functools
import math

import numpy as np
import jax
import jax.numpy as jnp
from jax import lax
from jax.experimental import pallas as pl
from jax.experimental.pallas import tpu as pltpu

F32 = jnp.float32
BF16 = jnp.bfloat16

GRID_W = 64
ROPE_THETA = 10000.0
NORM_EPS = 1e-6
N_MOD = 6
GQA_HEADS = 8
GQA_KV_HEADS = 2
GQA_GROUP = GQA_HEADS // GQA_KV_HEADS
MLA_HEADS = 8
MLA_NOPE = 64
MLA_ROPE = 32
MLA_V = 64
MLA_QK = MLA_NOPE + MLA_ROPE
CONV_WIDTH = 4
LRU_C = 8.0
N_GROUPS = 4
ROUTED_SCALE = 1.0
LOG2E = math.log2(math.e)

V7X_LANES = 128
V7X_SUBLANES = 8
V7X_VMEM_LIMIT_BYTES = 56 * 1024 * 1024

ROW_TILE = 256
ATTN_Q_TILE = 512
ATTN_SUB = 128
V_ROWS = 80
MOE_ROW_TILE = 1024
MOD_ROWS = 16


def _cparams(sem, vmem=None):
    return pltpu.CompilerParams(dimension_semantics=sem, vmem_limit_bytes=vmem)


def _tile(n, pref):
    t = min(n, pref)
    while n % t or t % V7X_SUBLANES:
        t -= 1
    return t


def _sigmoid(x):
    return 0.5 * (1.0 + jnp.tanh(0.5 * x))


def _rms(x):
    return x * lax.rsqrt(jnp.mean(x * x, axis=-1, keepdims=True) + NORM_EPS)


def _tiled(tbl, width):
    reps = width // tbl.shape[-1]
    return tbl if reps == 1 else jnp.concatenate([tbl] * reps, axis=-1)


def _rope_lanes(x, cos, sin_up, sin_dn, half):
    outs = []
    for c in range(x.shape[-1] // V7X_LANES):
        sl = slice(c * V7X_LANES, (c + 1) * V7X_LANES)
        xc = x[:, sl]
        up = pltpu.roll(xc, V7X_LANES - half, 1)
        dn = pltpu.roll(xc, half, 1)
        outs.append(xc * cos[:, sl] + up * sin_up[:, sl] + dn * sin_dn[:, sl])
    return outs[0] if len(outs) == 1 else jnp.concatenate(outs, axis=-1)


def _ones_row_tail(rows, cols):
    r = lax.broadcasted_iota(jnp.int32, (rows, cols), 0)
    return jnp.where(r == 0, 1.0, 0.0).astype(BF16)


def _stream_specs(streams, tm, nc, tile_of, batch_of):
    d = streams[0].shape[-1]
    if len(streams) == 1:
        return [pl.BlockSpec((1, tm, d), lambda *g: (batch_of(*g), tile_of(*g), 0))]
    return [pl.BlockSpec((1, tm, d), lambda *g: (batch_of(*g), jnp.minimum(tile_of(*g), nc - 1), 0)),
            pl.BlockSpec((1, tm, d), lambda *g: (batch_of(*g), jnp.maximum(tile_of(*g) - nc, 0), 0))]


def _read_stream(refs, tile, nc):
    if len(refs) == 1:
        return refs[0][0]
    return jnp.where(tile < nc, refs[0][0], refs[1][0])


def _mod_kernel(c_ref, w_ref, b_ref, o_ref):
    c = c_ref[...]
    a = (c * _sigmoid(c)).astype(BF16)
    o_ref[0] = jnp.dot(a, w_ref[0].astype(BF16), preferred_element_type=F32) + b_ref[0]


def _modulation(cc, w_mod, b_mod):
    depth, d, n = w_mod.shape
    tn = _tile(n, 1536) if n % V7X_LANES == 0 else n
    return pl.pallas_call(
        _mod_kernel,
        grid=(depth, n // tn),
        in_specs=[pl.BlockSpec((MOD_ROWS, d), lambda l, j: (0, 0)),
                  pl.BlockSpec((1, d, tn), lambda l, j: (l, 0, j)),
                  pl.BlockSpec((1, 1, tn), lambda l, j: (l, 0, j))],
        out_specs=pl.BlockSpec((1, MOD_ROWS, tn), lambda l, j: (l, 0, j)),
        out_shape=jax.ShapeDtypeStruct((depth, MOD_ROWS, n), F32),
        compiler_params=_cparams(("arbitrary", "arbitrary"), V7X_VMEM_LIMIT_BYTES),
        name="modulation",
    )(cc, w_mod, b_mod.reshape(depth, 1, n))


def _in_proj_kernel(*refs, n_x, has_prev, nc, hd, lru_w, q_lora, kv_lora, d):
    tile = pl.program_id(0)
    x_refs, refs = refs[:n_x], refs[n_x:]
    if has_prev:
        f_ref, pm_ref, refs = refs[0], refs[1], refs[2:]
    (g_ref, m_ref, w_ref, gq_ref, gk_ref, gc_ref, gu_ref, gd_ref, nqa_ref, nkv_ref, wq_ref, wkv_ref,
     qc_ref, qu_ref, qd_ref, kc_ref, ku_ref, kd_ref) = refs[:18]
    outs = refs[18:]
    x = _read_stream(x_refs, tile, nc)
    if has_prev:
        x = x + pm_ref[0, 0, 5:6, :] * f_ref[0]
        outs[0][0] = x
        outs = outs[1:]
    qg_ref, kg_ref, vg_ref, lru_ref, qm_ref, km_ref, vm_ref, gate_ref = outs
    tm = x.shape[0]
    h = _rms(x) * g_ref[0]
    h = h * (1.0 + m_ref[0, 0, 1:2, :]) + m_ref[0, 0, 0:1, :]
    hb = h.astype(BF16)

    nq, nk = GQA_HEADS * hd, GQA_KV_HEADS * hd
    w_gqa = nq + 2 * nk
    w_mla = q_lora + kv_lora + V7X_LANES
    o_lru, o_mla, o_gate = w_gqa, w_gqa + 2 * lru_w, w_gqa + 2 * lru_w + w_mla

    z = jnp.dot(hb, w_ref[0, :, 0:w_gqa], preferred_element_type=F32)

    def head_norm(a, n_heads):
        return jnp.concatenate([_rms(a[:, i * hd:(i + 1) * hd]) for i in range(n_heads)], axis=-1)

    cos, su, sd = gc_ref[...], gu_ref[...], gd_ref[...]
    q = head_norm(z[:, :nq], GQA_HEADS) * gq_ref[0]
    q = _rope_lanes(q, _tiled(cos, nq), _tiled(su, nq), _tiled(sd, nq), hd // 4)
    qg_ref[0] = (q * (LOG2E * hd ** -0.5)).astype(BF16)
    k = head_norm(z[:, nq:nq + nk], GQA_KV_HEADS) * gk_ref[0]
    k = _rope_lanes(k, _tiled(cos, nk), _tiled(su, nk), _tiled(sd, nk), hd // 4)
    v_t = z[:, nq + nk:nq + 2 * nk].T
    tail = _ones_row_tail(V_ROWS - hd, tm)
    for i in range(GQA_KV_HEADS):
        kg_ref[0, i] = k[:, i * hd:(i + 1) * hd].astype(BF16)
        vg_ref[0, i, 0:hd, :] = v_t[i * hd:(i + 1) * hd, :].astype(BF16)
        vg_ref[0, i, hd:V_ROWS, :] = tail

    lru_ref[0] = jnp.dot(hb, w_ref[0, :, o_lru:o_lru + 2 * lru_w], preferred_element_type=F32)

    z = jnp.dot(hb, w_ref[0, :, o_mla:o_mla + w_mla], preferred_element_type=F32)
    cq = (_rms(z[:, :q_lora]) * nqa_ref[0]).astype(BF16)
    qm = jnp.dot(cq, wq_ref[0], preferred_element_type=F32)
    wq = qm.shape[-1]
    qm = _rope_lanes(qm, _tiled(qc_ref[...], wq), _tiled(qu_ref[...], wq), _tiled(qd_ref[...], wq),
                     MLA_ROPE // 4)
    qm_ref[0] = (qm * (LOG2E * MLA_QK ** -0.5)).astype(BF16)
    ckv = (_rms(z[:, q_lora:q_lora + kv_lora]) * nkv_ref[0]).astype(BF16)
    kv = jnp.dot(ckv, wkv_ref[0], preferred_element_type=F32)
    kr = _rope_lanes(z[:, q_lora + kv_lora:], kc_ref[...], ku_ref[...], kd_ref[...], MLA_ROPE // 4)
    kr = kr[:, :MLA_ROPE]
    per = MLA_NOPE + MLA_V
    tail = _ones_row_tail(V_ROWS - MLA_V, tm)
    for i in range(MLA_HEADS):
        km_ref[0, i] = jnp.concatenate([kv[:, i * per:i * per + MLA_NOPE], kr], axis=-1).astype(BF16)
        head_t = kv[:, i * per:(i + 1) * per].T
        vm_ref[0, i, 0:MLA_V, :] = head_t[MLA_NOPE:per, :].astype(BF16)
        vm_ref[0, i, MLA_V:V_ROWS, :] = tail

    gate_ref[0] = _sigmoid(jnp.dot(hb, w_ref[0, :, o_gate:o_gate + 3 * d],
                                   preferred_element_type=F32)).astype(BF16)


def _in_proj(streams, prev, layer, p, n_ctx):
    b, d = streams[0].shape[0], streams[0].shape[-1]
    t = sum(s.shape[1] for s in streams) if len(streams) == 2 else streams[0].shape[1]
    tm = _tile(math.gcd(n_ctx, t), ROW_TILE)
    nc = n_ctx // tm
    hd, lru_w, q_lora, kv_lora = p["hd"], p["lru_w"], p["q_lora"], p["kv_lora"]
    nq, nk = GQA_HEADS * hd, GQA_KV_HEADS * hd
    qw = MLA_HEADS * MLA_QK
    tile_of, batch_of = (lambda i, bb: i), (lambda i, bb: bb)
    row = lambda i, bb: (bb, i, 0)
    mod_row = lambda i, bb: jnp.where(i < nc, 0, bb + 1)
    lay3 = lambda i, bb: (layer, 0, 0)
    tbl = lambda a: pl.BlockSpec((tm, a.shape[-1]), lambda i, bb: (i, 0))
    x_spec = pl.BlockSpec((1, tm, d), row)

    in_specs = _stream_specs(streams, tm, nc, tile_of, batch_of)
    args = list(streams)
    out_specs, out_shape = [], []
    if prev is not None:
        in_specs += [x_spec, pl.BlockSpec((1, 1, N_MOD, d), lambda i, bb: (prev[1], mod_row(i, bb), 0, 0))]
        args += [prev[0], p["mods"]]
        out_specs.append(x_spec)
        out_shape.append(jax.ShapeDtypeStruct((b, t, d), F32))
    in_specs += [pl.BlockSpec((1, 1, d), lay3),
                 pl.BlockSpec((1, 1, N_MOD, d), lambda i, bb: (layer, mod_row(i, bb), 0, 0)),
                 pl.BlockSpec((1,) + p["w_in"].shape[1:], lay3, pipeline_mode=pl.Buffered(1)),
                 pl.BlockSpec((1, 1, nq), lay3), pl.BlockSpec((1, 1, nk), lay3)]
    args += [p["norm_mix"], p["mods"], p["w_in"], p["gq"], p["gk"]]
    in_specs += [tbl(a) for a in p["gqa_tabs"]]
    args += list(p["gqa_tabs"])
    in_specs += [pl.BlockSpec((1, 1, q_lora), lay3), pl.BlockSpec((1, 1, kv_lora), lay3),
                 pl.BlockSpec((1,) + p["w_qb"].shape[1:], lay3),
                 pl.BlockSpec((1,) + p["w_kvb"].shape[1:], lay3)]
    args += [p["mla_q_norm"], p["mla_kv_norm"], p["w_qb"], p["w_kvb"]]
    in_specs += [tbl(a) for a in p["mla_q_tabs"]] + [tbl(a) for a in p["mla_k_tabs"]]
    args += list(p["mla_q_tabs"]) + list(p["mla_k_tabs"])

    def kv_specs(heads, dk):
        return [pl.BlockSpec((1, heads, tm, dk), lambda i, bb: (bb, 0, i, 0)),
                pl.BlockSpec((1, heads, V_ROWS, tm), lambda i, bb: (bb, 0, 0, i))]

    out_specs += ([pl.BlockSpec((1, tm, nq), row)] + kv_specs(GQA_KV_HEADS, hd)
                  + [pl.BlockSpec((1, tm, 2 * lru_w), row), pl.BlockSpec((1, tm, qw), row)]
                  + kv_specs(MLA_HEADS, MLA_QK) + [pl.BlockSpec((1, tm, 3 * d), row)])
    out_shape += [jax.ShapeDtypeStruct((b, t, nq), BF16),
                  jax.ShapeDtypeStruct((b, GQA_KV_HEADS, t, hd), BF16),
                  jax.ShapeDtypeStruct((b, GQA_KV_HEADS, V_ROWS, t), BF16),
                  jax.ShapeDtypeStruct((b, t, 2 * lru_w), F32),
                  jax.ShapeDtypeStruct((b, t, qw), BF16),
                  jax.ShapeDtypeStruct((b, MLA_HEADS, t, MLA_QK), BF16),
                  jax.ShapeDtypeStruct((b, MLA_HEADS, V_ROWS, t), BF16),
                  jax.ShapeDtypeStruct((b, t, 3 * d), BF16)]
    outs = pl.pallas_call(
        functools.partial(_in_proj_kernel, n_x=len(streams), has_prev=prev is not None, nc=nc, hd=hd,
                          lru_w=lru_w, q_lora=q_lora, kv_lora=kv_lora, d=d),
        grid=(t // tm, b),
        in_specs=in_specs, out_specs=out_specs, out_shape=out_shape,
        compiler_params=_cparams(("arbitrary", "arbitrary"), V7X_VMEM_LIMIT_BYTES),
        name="in_proj",
    )(*args)
    if prev is not None:
        return (outs[0],), outs[1:]
    return streams, outs


def _lru_kernel(z_ref, cw_ref, cb_ref, wg_ref, bg_ref, lam_ref, o_ref, a_f, b_f, a_b, b_b, *, n_ctx, chunk):
    t, width = a_f.shape
    n_chunks = t // chunk
    row = lax.broadcasted_iota(jnp.int32, (chunk, 1), 0)

    def conv_chunk(c):
        r0 = c * chunk
        seg_lo, seg_hi = (0, n_ctx) if r0 < n_ctx else (n_ctx, t)
        u = jnp.zeros((chunk, width), F32) + cb_ref[0]
        for j in range(CONV_WIDTH):
            off = j - 1
            lo = min(max(r0 + off, 0), t - chunk)
            tap = z_ref[0, lo:lo + chunk, 0:width]
            shift = (lo - (r0 + off)) % chunk
            if shift:
                tap = pltpu.roll(tap, shift, 0)
            if r0 + off < seg_lo or r0 + off + chunk > seg_hi:
                pos = row + (r0 + off)
                tap = jnp.where((pos >= seg_lo) & (pos < seg_hi), tap, 0.0)
            u = u + tap * cw_ref[0, j:j + 1, :]
        return u

    scr = ((a_f, b_f), (a_b, b_b))
    half_c_sp = []
    for dd in range(2):
        lam = lam_ref[0, dd:dd + 1, :]
        sp = jnp.maximum(-lam, 0.0) + jnp.log1p(jnp.exp(-jnp.abs(lam)))
        half_c_sp.append((-0.5 * LRU_C) * sp)
    for c in range(n_chunks):
        u = conv_chunk(c)
        g = jnp.dot(u.astype(BF16), wg_ref[0], preferred_element_type=F32) + bg_ref[0]
        half_u = 0.5 * u
        for dd in range(2):
            base = 2 * dd * width
            t_r = jnp.tanh(g[:, base:base + width])
            t_i = jnp.tanh(g[:, base + width:base + 2 * width])
            log_a = (1.0 + t_r) * half_c_sp[dd]
            th = jnp.tanh(log_a)
            v = (-2.0 * th) / (1.0 - th)
            root = jnp.where(v > 0.0, v * lax.rsqrt(v), 0.0)
            scr[dd][0][c * chunk:(c + 1) * chunk, :] = jnp.exp(log_a)
            scr[dd][1][c * chunk:(c + 1) * chunk, :] = root * ((1.0 + t_i) * half_u)

    ctx_blk = n_ctx // V7X_SUBLANES
    all_blk = t // V7X_SUBLANES
    nsub = V7X_SUBLANES

    def body(i, carry):
        hf, hb = carry
        rf = pl.multiple_of(i * nsub, nsub)
        jb = jnp.where(i < ctx_blk, ctx_blk - 1 - i, all_blk - 1 - (i - ctx_blk))
        rb = pl.multiple_of(jb * nsub, nsub)
        af, bf = a_f[pl.ds(rf, nsub), :], b_f[pl.ds(rf, nsub), :]
        ab, bb = a_b[pl.ds(rb, nsub), :], b_b[pl.ds(rb, nsub), :]
        rows_f, rows_b = [None] * nsub, [None] * nsub
        for s in range(nsub):
            hf = af[s:s + 1, :] * hf + bf[s:s + 1, :]
            rows_f[s] = hf
            sb = nsub - 1 - s
            hb = ab[sb:sb + 1, :] * hb + bb[sb:sb + 1, :]
            rows_b[sb] = hb
        a_f[pl.ds(rf, nsub), :] = jnp.concatenate(rows_f, axis=0)
        a_b[pl.ds(rb, nsub), :] = jnp.concatenate(rows_b, axis=0)
        return hf, hb

    h0 = jnp.zeros((1, width), F32)
    lax.fori_loop(0, all_blk, body, (h0, h0))

    k0 = math.sqrt(2.0 / math.pi)
    for c in range(n_chunks):
        sl = slice(c * chunk, (c + 1) * chunk)
        y = z_ref[0, sl, width:2 * width]
        gelu = 0.5 * y * (1.0 + jnp.tanh(k0 * (y + 0.044715 * (y * y * y))))
        o_ref[0, sl, :] = ((a_f[sl, :] + a_b[sl, :]) * gelu).astype(BF16)


def _lru(zlru, layer, p, n_ctx):
    b, t, w2 = zlru.shape
    width = w2 // 2
    chunk = _tile(math.gcd(n_ctx, t), ROW_TILE)
    lay3 = lambda bb: (layer, 0, 0)
    return pl.pallas_call(
        functools.partial(_lru_kernel, n_ctx=n_ctx, chunk=chunk),
        grid=(b,),
        in_specs=[pl.BlockSpec((1, t, w2), lambda bb: (bb, 0, 0)),
                  pl.BlockSpec((1, CONV_WIDTH, width), lay3),
                  pl.BlockSpec((1, 1, width), lay3),
                  pl.BlockSpec((1,) + p["lru_wg"].shape[1:], lay3),
                  pl.BlockSpec((1, 1, 4 * width), lay3),
                  pl.BlockSpec((1, 2, width), lay3)],
        out_specs=pl.BlockSpec((1, t, width), lambda bb: (bb, 0, 0)),
        out_shape=jax.ShapeDtypeStruct((b, t, width), BF16),
        scratch_shapes=[pltpu.VMEM((t, width), F32)] * 4,
        compiler_params=_cparams(("arbitrary",), V7X_VMEM_LIMIT_BYTES),
        name="rg_lru",
    )(zlru, p["conv_w"], p["conv_b"], p["lru_wg"], p["lru_bg"], p["lru_lam"])


def _attn_kernel(q_ref, k_ref, vt_ref, o_ref, s0, s1, p0, p1, *, gqa, dk, dv, n_ctx, n_lat, with_ctx, tq):
    t = n_ctx + n_lat
    sbuf, pbuf = (s0, s1), (p0, p1)
    out_off = 0 if with_ctx else n_ctx
    nt_dims = (((1,), (1,)), ((), ()))

    def n_sub(rows):
        return GQA_KV_HEADS * (rows // ATTN_SUB) if gqa else MLA_HEADS

    def load_q(r0, j, rows):
        if gqa:
            kvh, piece = j % GQA_KV_HEADS, j // GQA_KV_HEADS
            blk = q_ref[0, pl.ds(r0 + piece * ATTN_SUB, ATTN_SUB), kvh * GQA_GROUP * dk:(kvh + 1) * GQA_GROUP * dk]
            return jnp.concatenate([blk[:, h * dk:(h + 1) * dk] for h in range(GQA_GROUP)], axis=0), kvh
        return q_ref[0, pl.ds(r0, rows), j * dk:(j + 1) * dk], j

    def qk(r0, j, rows, n_keys, par):
        q, kh = load_q(r0, j, rows)
        sbuf[par][0:n_keys, 0:q.shape[0]] = lax.dot_general(
            k_ref[0, kh, 0:n_keys, :], q, nt_dims, preferred_element_type=F32)

    def finish(r0, j, rows, n_keys, par):
        cols = GQA_GROUP * ATTN_SUB if gqa else rows
        kh = j % GQA_KV_HEADS if gqa else j
        s = sbuf[par][0:n_keys, 0:cols]
        m = jnp.max(s, axis=0, keepdims=True)
        pbuf[par][0:n_keys, 0:cols] = jnp.exp2(s - m).astype(BF16)
        ot = jnp.dot(vt_ref[0, kh, :, 0:n_keys], pbuf[par][0:n_keys, 0:cols], preferred_element_type=F32)
        o = (ot[0:dv, :] / ot[dv:dv + 1, :]).T.astype(o_ref.dtype)
        if gqa:
            piece = j // GQA_KV_HEADS
            out = jnp.concatenate([o[h * ATTN_SUB:(h + 1) * ATTN_SUB, :] for h in range(GQA_GROUP)], axis=-1)
            o_ref[0, pl.ds(r0 - out_off + piece * ATTN_SUB, ATTN_SUB),
                  kh * GQA_GROUP * dv:(kh + 1) * GQA_GROUP * dv] = out
        else:
            o_ref[0, pl.ds(r0 - out_off, rows), j * dv:(j + 1) * dv] = o

    if with_ctx:
        rows_c = min(tq, n_ctx)
        for r in range(0, n_ctx, rows_c):
            for j in range(n_sub(rows_c)):
                qk(r, j, rows_c, n_ctx, 0)
                finish(r, j, rows_c, n_ctx, 0)

    n_tiles = n_lat // tq
    ns = n_sub(tq)
    assert ns % 2 == 0
    qk(n_ctx, 0, tq, t, 0)

    def body(i, carry):
        r0 = pl.multiple_of(n_ctx + i * tq, ATTN_SUB)
        r_next = pl.multiple_of(n_ctx + jnp.minimum(i + 1, n_tiles - 1) * tq, ATTN_SUB)
        for j in range(ns):
            if j + 1 < ns:
                qk(r0, j + 1, tq, t, (j + 1) % 2)
            else:
                qk(r_next, 0, tq, t, 0)
            finish(r0, j, tq, t, j % 2)
        return carry

    lax.fori_loop(0, n_tiles, body, 0)


def _attention(q, k, vt, *, gqa, n_ctx, with_ctx, name):
    b, t, qw = q.shape
    hk, dk = k.shape[1], k.shape[-1]
    dv = MLA_V if not gqa else dk
    n_lat = t - n_ctx
    tq = _tile(n_lat, ATTN_Q_TILE)
    cols = GQA_GROUP * ATTN_SUB if gqa else tq
    t_out = t if with_ctx else n_lat
    w_out = (qw // dk) * dv
    kern = functools.partial(_attn_kernel, gqa=gqa, dk=dk, dv=dv, n_ctx=n_ctx, n_lat=n_lat,
                             with_ctx=with_ctx, tq=tq)
    return pl.pallas_call(
        kern,
        grid=(b,),
        in_specs=[pl.BlockSpec((1, t, qw), lambda bb: (bb, 0, 0)),
                  pl.BlockSpec((1, hk, t, dk), lambda bb: (bb, 0, 0, 0)),
                  pl.BlockSpec((1, hk, V_ROWS, t), lambda bb: (bb, 0, 0, 0))],
        out_specs=pl.BlockSpec((1, t_out, w_out), lambda bb: (bb, 0, 0)),
        out_shape=jax.ShapeDtypeStruct((b, t_out, w_out), BF16),
        scratch_shapes=[pltpu.VMEM((t, cols), F32), pltpu.VMEM((t, cols), F32),
                        pltpu.VMEM((t, cols), BF16), pltpu.VMEM((t, cols), BF16)],
        compiler_params=_cparams(("arbitrary",), V7X_VMEM_LIMIT_BYTES),
        name=name,
    )(q, k, vt)


def _merge_kernel(*refs, n_x, nc, d, n_exp):
    x_refs, refs = refs[:n_x], refs[n_x:]
    (oa_ref, ol_ref, om_ref, gt_ref, m_ref, g_ref, wa_ref, wl_ref, wm_ref, wo_ref, rh_ref, rl_ref, rb_ref,
     xo_ref, h_ref, ids_ref, wts_ref) = refs
    gates = gt_ref[0]
    merged = (gates[:, 0:d].astype(F32) * jnp.dot(oa_ref[0], wa_ref[0], preferred_element_type=F32)
              + gates[:, d:2 * d].astype(F32) * jnp.dot(ol_ref[0], wl_ref[0], preferred_element_type=F32)
              + gates[:, 2 * d:3 * d].astype(F32) * jnp.dot(om_ref[0], wm_ref[0], preferred_element_type=F32))
    y = jnp.dot(merged.astype(BF16), wo_ref[0], preferred_element_type=F32)
    x = _read_stream(x_refs, pl.program_id(1), nc) + m_ref[0, 0, 2:3, :] * y
    xo_ref[0] = x
    h = _rms(x) * g_ref[0]
    h = h * (1.0 + m_ref[0, 0, 4:5, :]) + m_ref[0, 0, 3:4, :]
    h_hi = h.astype(BF16)
    h_ref[0] = h_hi
    h_lo = (h - h_hi.astype(F32)).astype(BF16)

    dn = (((1,), (1,)), ((), ()))
    logits = (lax.dot_general(rh_ref[...], h_hi, dn, preferred_element_type=F32)
              + lax.dot_general(rh_ref[...], h_lo, dn, preferred_element_type=F32)
              + lax.dot_general(rl_ref[...], h_hi, dn, preferred_element_type=F32))
    scores = _sigmoid(logits)
    sel = scores + rb_ref[...]
    per = n_exp // N_GROUPS
    gs = []
    for g in range(N_GROUPS):
        r = [sel[g * per + j:g * per + j + 1, :] for j in range(per)]
        best = None
        for a in range(per):
            for bq in range(a + 1, per):
                pair = r[a] + r[bq]
                best = pair if best is None else jnp.maximum(best, pair)
        gs.append(best)
    gmax = functools.reduce(jnp.maximum, gs)
    gbest = jnp.full(gmax.shape, N_GROUPS - 1, jnp.int32)
    for g in range(N_GROUPS - 2, -1, -1):
        gbest = jnp.where(gs[g] == gmax, g, gbest)
    eid = lax.broadcasted_iota(jnp.int32, sel.shape, 0)
    gid = jnp.zeros(sel.shape, jnp.int32)
    for g in range(1, N_GROUPS):
        gid = gid + (eid >= g * per).astype(jnp.int32)
    masked = jnp.where(gid == gbest, sel, -jnp.inf)
    m1 = jnp.max(masked, axis=0, keepdims=True)
    i1 = jnp.min(jnp.where(masked == m1, eid, n_exp), axis=0, keepdims=True)
    masked2 = jnp.where(eid == i1, -jnp.inf, masked)
    m2 = jnp.max(masked2, axis=0, keepdims=True)
    i2 = jnp.min(jnp.where(masked2 == m2, eid, n_exp), axis=0, keepdims=True)
    s1 = jnp.sum(jnp.where(eid == i1, scores, 0.0), axis=0, keepdims=True)
    s2 = jnp.sum(jnp.where(eid == i2, scores, 0.0), axis=0, keepdims=True)
    tot = s1 + s2
    ids_ref[0] = jnp.concatenate([i1, i2], axis=0)
    wts_ref[0] = jnp.concatenate([ROUTED_SCALE * s1 / tot, ROUTED_SCALE * s2 / tot], axis=0)


def _merge(streams, o_attn, o_lru, o_mla, gates, layer, p, n_ctx, with_ctx):
    b, d = streams[0].shape[0], streams[0].shape[-1]
    t = o_lru.shape[1]
    n_exp = p["r_hi"].shape[0]
    tm = _tile(math.gcd(n_ctx, t), ROW_TILE)
    off = 0 if with_ctx else n_ctx // tm
    nc = n_ctx // tm
    tq = t if with_ctx else t - n_ctx
    nt = tq // tm
    full = lambda bb, i: (bb, i + off, 0)
    qrow = lambda bb, i: (bb, i, 0)
    const = lambda bb, i: (0, 0)
    lay3 = lambda bb, i: (layer, 0, 0)
    bw = o_attn.shape[-1]
    route_spec = pl.BlockSpec((1, 2, tm), lambda bb, i: (bb * nt + i, 0, 0))
    wspec = lambda a: pl.BlockSpec((1,) + a.shape[1:], lay3)
    in_specs = _stream_specs(streams, tm, nc, lambda bb, i: i + off, lambda bb, i: bb)
    in_specs += [pl.BlockSpec((1, tm, bw), qrow),
                 pl.BlockSpec((1, tm, o_lru.shape[-1]), full),
                 pl.BlockSpec((1, tm, bw), qrow),
                 pl.BlockSpec((1, tm, 3 * d), full),
                 pl.BlockSpec((1, 1, N_MOD, d), lambda bb, i: (layer, jnp.where(i + off < nc, 0, bb + 1), 0, 0)),
                 pl.BlockSpec((1, 1, d), lay3),
                 wspec(p["w_ba"]), wspec(p["w_bl"]), wspec(p["w_bm"]), wspec(p["w_out"]),
                 pl.BlockSpec(p["r_hi"].shape, const), pl.BlockSpec(p["r_lo"].shape, const),
                 pl.BlockSpec(p["r_bias"].shape, const)]
    return pl.pallas_call(
        functools.partial(_merge_kernel, n_x=len(streams), nc=nc - off, d=d, n_exp=n_exp),
        grid=(b, nt),
        in_specs=in_specs,
        out_specs=[pl.BlockSpec((1, tm, d), qrow), pl.BlockSpec((1, tm, d), qrow),
                   route_spec, route_spec],
        out_shape=[jax.ShapeDtypeStruct((b, tq, d), F32), jax.ShapeDtypeStruct((b, tq, d), BF16),
                   jax.ShapeDtypeStruct((b * nt, 2, tm), jnp.int32),
                   jax.ShapeDtypeStruct((b * nt, 2, tm), F32)],
        compiler_params=_cparams(("arbitrary", "arbitrary"), V7X_VMEM_LIMIT_BYTES),
        name="merge_router",
    )(*streams, o_attn, o_lru, o_mla, gates, p["mods"], p["norm_ffn"], p["w_ba"], p["w_bl"], p["w_bm"],
      p["w_out"], p["r_hi"], p["r_lo"], p["r_bias"])


def _moe_kernel(h_ref, ids_ref, wts_ref, wg_ref, wu_ref, wd_ref, o_ref, acc):
    e = pl.program_id(1)

    @pl.when(e == 0)
    def _():
        acc[...] = jnp.zeros_like(acc)

    ids = ids_ref[...]
    wts = wts_ref[...]
    c = jnp.sum(jnp.where(ids == e, wts, 0.0), axis=-1, keepdims=True)
    g = jnp.dot(h_ref[...], wg_ref[0, 0], preferred_element_type=F32)
    u = jnp.dot(h_ref[...], wu_ref[0, 0], preferred_element_type=F32)
    act = (g * _sigmoid(g)) * u * c
    acc[...] += jnp.dot(act.astype(BF16), wd_ref[0, 0], preferred_element_type=F32)

    @pl.when(e == pl.num_programs(1) - 1)
    def _():
        o_ref[...] = acc[...]


def _moe(h, ids, wts, layer, p):
    n, d = h.shape
    _, n_exp, _, ff = p["moe_wg"].shape
    tm = _tile(n, MOE_ROW_TILE)
    return pl.pallas_call(
        _moe_kernel,
        grid=(n // tm, n_exp),
        in_specs=[pl.BlockSpec((tm, d), lambda i, e: (i, 0)),
                  pl.BlockSpec((tm, 2), lambda i, e: (i, 0)),
                  pl.BlockSpec((tm, 2), lambda i, e: (i, 0)),
                  pl.BlockSpec((1, 1, d, ff), lambda i, e: (layer, e, 0, 0)),
                  pl.BlockSpec((1, 1, d, ff), lambda i, e: (layer, e, 0, 0)),
                  pl.BlockSpec((1, 1, ff, d), lambda i, e: (layer, e, 0, 0))],
        out_specs=pl.BlockSpec((tm, d), lambda i, e: (i, 0)),
        out_shape=jax.ShapeDtypeStruct((n, d), F32),
        scratch_shapes=[pltpu.VMEM((tm, d), F32)],
        compiler_params=_cparams(("arbitrary", "arbitrary"), V7X_VMEM_LIMIT_BYTES),
        name="moe_ffn",
    )(h, ids, wts, p["moe_wg"], p["moe_wu"], p["moe_wd"])


def _final_kernel(x_ref, f_ref, m_ref, g_ref, o_ref):
    x = x_ref[0] + m_ref[0, 0, 5:6, :] * f_ref[0]
    o_ref[0] = _rms(x) * g_ref[...]


def _final(x, f, mods, layer, g):
    b, s, d = x.shape
    tm = _tile(s, 2 * ROW_TILE)
    row = lambda bb, i: (bb, i, 0)
    return pl.pallas_call(
        _final_kernel,
        grid=(b, s // tm),
        in_specs=[pl.BlockSpec((1, tm, d), row), pl.BlockSpec((1, tm, d), row),
                  pl.BlockSpec((1, 1, N_MOD, d), lambda bb, i: (layer, bb + 1, 0, 0)),
                  pl.BlockSpec((1, d), lambda bb, i: (0, 0))],
        out_specs=pl.BlockSpec((1, tm, d), row),
        out_shape=jax.ShapeDtypeStruct((b, s, d), F32),
        compiler_params=_cparams(("arbitrary", "arbitrary")),
        name="final_norm",
    )(x, f, mods, g.reshape(1, d))


def _rope_tables(n_ctx, n_lat, dim, period, lane_off, width):
    quarter = dim // 4
    pos = jnp.arange(n_lat, dtype=F32)
    r, col = jnp.floor(pos / GRID_W), pos - GRID_W * jnp.floor(pos / GRID_W)
    inv_freq = ROPE_THETA ** (-jnp.arange(quarter, dtype=F32) / quarter)
    lane = np.arange(width)
    j = (lane % period) - lane_off
    active = (j >= 0) & (j < dim)
    jj = np.where(active, j, 0)
    use_col = jj >= dim // 2
    upper = (jj % (dim // 2)) >= quarter
    f = jj % quarter
    ang = jnp.where(use_col[None, :], col[:, None], r[:, None]) * inv_freq[f][None, :]
    act = jnp.asarray(active)[None, :]
    cos = jnp.where(act, jnp.cos(ang), 1.0)
    sin = jnp.where(act, jnp.sin(ang), 0.0)
    sin_up = jnp.where(jnp.asarray(~upper)[None, :], -sin, 0.0)
    sin_dn = jnp.where(jnp.asarray(upper)[None, :], sin, 0.0)
    ident = lambda v, fill: jnp.concatenate([jnp.full((n_ctx, width), fill, F32), v], axis=0)
    return ident(cos, 1.0), ident(sin_up, 0.0), ident(sin_dn, 0.0)


def _lru_gate_weights(w_a, w_i):
    depth, _, nb, k, j = w_a.shape
    w = jnp.stack([w_a, w_i], axis=2)
    eye = jnp.eye(nb, dtype=w.dtype)
    dense = jnp.einsum('dsgnkj,nm->dnksgmj', w, eye)
    return (0.5 * dense).reshape(depth, nb * k, 4 * nb * j).astype(BF16)


def kernel(x, c, ctx, c_ctx, w_mod, b_mod, norm_mix, norm_ffn, w_in, gqa_q_norm, gqa_k_norm, conv_w, conv_b,
           lru_w_a, lru_b_a, lru_w_i, lru_b_i, lru_lam, mla_q_a_norm, mla_w_qb, mla_kv_a_norm, mla_w_kvb,
           w_branch_attn, w_branch_lru, w_branch_mla, w_out, router_w, router_bias,
           moe_w_gate, moe_w_up, moe_w_down, final_norm):
    bsz, n_lat, d = x.shape
    n_ctx = ctx.shape[1]
    depth = w_mod.shape[0]
    hd = gqa_q_norm.shape[-1]
    lru_w = conv_w.shape[-1]
    q_lora, kv_lora = mla_q_a_norm.shape[-1], mla_kv_a_norm.shape[-1]
    n_exp = router_w.shape[-1]
    assert bsz + 1 <= MOD_ROWS and d % V7X_LANES == 0

    cc = jnp.zeros((MOD_ROWS, d), F32).at[0].set(c_ctx).at[1:1 + bsz].set(c)
    mods = _modulation(cc, w_mod, b_mod).reshape(depth, MOD_ROWS, N_MOD, d)

    mq_period = MLA_QK * V7X_LANES // math.gcd(MLA_QK, V7X_LANES)
    kr_end = GQA_HEADS * hd + 2 * GQA_KV_HEADS * hd + 2 * lru_w + q_lora + kv_lora + MLA_ROPE
    w_in_b = w_in.astype(BF16)
    w_in_b = jnp.concatenate([w_in_b[..., :kr_end], jnp.zeros((depth, d, V7X_LANES - MLA_ROPE), BF16),
                              w_in_b[..., kr_end:]], axis=-1)
    r_t = router_w.T
    r_hi = r_t.astype(BF16)
    p = dict(
        hd=hd, lru_w=lru_w, q_lora=q_lora, kv_lora=kv_lora, mods=mods,
        norm_mix=norm_mix.reshape(depth, 1, d), norm_ffn=norm_ffn.reshape(depth, 1, d), w_in=w_in_b,
        gq=jnp.tile(gqa_q_norm, (1, GQA_HEADS)).reshape(depth, 1, -1),
        gk=jnp.tile(gqa_k_norm, (1, GQA_KV_HEADS)).reshape(depth, 1, -1),
        gqa_tabs=_rope_tables(n_ctx, n_lat, hd, hd, 0, V7X_LANES),
        mla_q_tabs=_rope_tables(n_ctx, n_lat, MLA_ROPE, MLA_QK, MLA_NOPE, mq_period),
        mla_k_tabs=_rope_tables(n_ctx, n_lat, MLA_ROPE, V7X_LANES, 0, V7X_LANES),
        mla_q_norm=mla_q_a_norm.reshape(depth, 1, q_lora), mla_kv_norm=mla_kv_a_norm.reshape(depth, 1, kv_lora),
        w_qb=mla_w_qb.astype(BF16), w_kvb=mla_w_kvb.astype(BF16),
        conv_w=conv_w, conv_b=conv_b.reshape(depth, 1, lru_w),
        lru_wg=_lru_gate_weights(lru_w_a, lru_w_i),
        lru_bg=0.5 * jnp.stack([lru_b_a, lru_b_i], axis=2).reshape(depth, 1, 4 * lru_w),
        lru_lam=lru_lam,
        w_ba=w_branch_attn.astype(BF16), w_bl=w_branch_lru.astype(BF16), w_bm=w_branch_mla.astype(BF16),
        w_out=w_out.astype(BF16), r_hi=r_hi, r_lo=(r_t - r_hi.astype(F32)).astype(BF16),
        r_bias=router_bias.reshape(n_exp, 1).astype(F32),
        moe_wg=moe_w_gate.astype(BF16), moe_wu=moe_w_up.astype(BF16), moe_wd=moe_w_down.astype(BF16),
    )

    streams = (ctx, x)
    prev = None
    for layer in range(depth):
        last = layer == depth - 1
        streams, (q_g, k_g, v_g, zlru, q_m, k_m, v_m, gates) = _in_proj(streams, prev, layer, p, n_ctx)
        o_attn = _attention(q_g, k_g, v_g, gqa=True, n_ctx=n_ctx, with_ctx=not last, name="gqa_attention")
        o_lru = _lru(zlru, layer, p, n_ctx)
        o_mla = _attention(q_m, k_m, v_m, gqa=False, n_ctx=n_ctx, with_ctx=not last, name="mla_attention")
        x_mid, h2, ids, wts = _merge(streams, o_attn, o_lru, o_mla, gates, layer, p, n_ctx, with_ctx=not last)

        n_tok = x_mid.shape[0] * x_mid.shape[1]
        ids_t = jnp.transpose(ids, (0, 2, 1)).reshape(n_tok, 2)
        wts_t = jnp.transpose(wts, (0, 2, 1)).reshape(n_tok, 2)
        f = _moe(h2.reshape(n_tok, d), ids_t, wts_t, layer, p).reshape(x_mid.shape)
        streams, prev = (x_mid,), (f, layer)

    return _final(streams[0], prev[0], mods, prev[1], final_norm)
```

```python
import functools
import math

import numpy as np
import jax
import jax.numpy as jnp
from jax import lax
from jax.experimental import pallas as pl
from jax.experimental.pallas import tpu as pltpu

F32 = jnp.float32
BF16 = jnp.bfloat16

GRID_W = 64
ROPE_THETA = 10000.0
NORM_EPS = 1e-6
N_MOD = 6
GQA_HEADS = 8
GQA_KV_HEADS = 2
GQA_GROUP = GQA_HEADS // GQA_KV_HEADS
MLA_HEADS = 8
MLA_NOPE = 64
MLA_ROPE = 32
MLA_V = 64
MLA_QK = MLA_NOPE + MLA_ROPE
CONV_WIDTH = 4
LRU_C = 8.0
N_GROUPS = 4
ROUTED_SCALE = 1.0
LOG2E = math.log2(math.e)

V7X_LANES = 128
V7X_SUBLANES = 8
V7X_VMEM_LIMIT_BYTES = 56 * 1024 * 1024

ROW_TILE = 256
ATTN_Q_TILE = 512
ATTN_SUB = 128
V_ROWS = 80
MOE_ROW_TILE = 1024
MOD_ROWS = 16


def _cparams(sem, vmem=None):
    return pltpu.CompilerParams(dimension_semantics=sem, vmem_limit_bytes=vmem)


def _tile(n, pref):
    t = min(n, pref)
    while n % t or t % V7X_SUBLANES:
        t -= 1
    return t


def _sigmoid(x):
    return 0.5 * (1.0 + jnp.tanh(0.5 * x))


def _rms(x):
    return x * lax.rsqrt(jnp.mean(x * x, axis=-1, keepdims=True) + NORM_EPS)


def _tiled(tbl, width):
    reps = width // tbl.shape[-1]
    return tbl if reps == 1 else jnp.concatenate([tbl] * reps, axis=-1)


def _rope_lanes(x, cos, sin_up, sin_dn, half):
    outs = []
    for c in range(x.shape[-1] // V7X_LANES):
        sl = slice(c * V7X_LANES, (c + 1) * V7X_LANES)
        xc = x[:, sl]
        up = pltpu.roll(xc, V7X_LANES - half, 1)
        dn = pltpu.roll(xc, half, 1)
        outs.append(xc * cos[:, sl] + up * sin_up[:, sl] + dn * sin_dn[:, sl])
    return outs[0] if len(outs) == 1 else jnp.concatenate(outs, axis=-1)


def _ones_row_tail(rows, cols):
    r = lax.broadcasted_iota(jnp.int32, (rows, cols), 0)
    return jnp.where(r == 0, 1.0, 0.0).astype(BF16)


def _stream_specs(streams, tm, nc, tile_of, batch_of):
    d = streams[0].shape[-1]
    if len(streams) == 1:
        return [pl.BlockSpec((1, tm, d), lambda *g: (batch_of(*g), tile_of(*g), 0))]
    return [pl.BlockSpec((1, tm, d), lambda *g: (batch_of(*g), jnp.minimum(tile_of(*g), nc - 1), 0)),
            pl.BlockSpec((1, tm, d), lambda *g: (batch_of(*g), jnp.maximum(tile_of(*g) - nc, 0), 0))]


def _read_stream(refs, tile, nc):
    if len(refs) == 1:
        return refs[0][0]
    return jnp.where(tile < nc, refs[0][0], refs[1][0])


def _mod_kernel(c_ref, w_ref, b_ref, o_ref):
    c = c_ref[...]
    a = (c * _sigmoid(c)).astype(BF16)
    o_ref[0] = jnp.dot(a, w_ref[0].astype(BF16), preferred_element_type=F32) + b_ref[0]


def _modulation(cc, w_mod, b_mod):
    depth, d, n = w_mod.shape
    tn = _tile(n, 1536) if n % V7X_LANES == 0 else n
    return pl.pallas_call(
        _mod_kernel,
        grid=(depth, n // tn),
        in_specs=[pl.BlockSpec((MOD_ROWS, d), lambda l, j: (0, 0)),
                  pl.BlockSpec((1, d, tn), lambda l, j: (l, 0, j)),
                  pl.BlockSpec((1, 1, tn), lambda l, j: (l, 0, j))],
        out_specs=pl.BlockSpec((1, MOD_ROWS, tn), lambda l, j: (l, 0, j)),
        out_shape=jax.ShapeDtypeStruct((depth, MOD_ROWS, n), F32),
        compiler_params=_cparams(("arbitrary", "arbitrary"), V7X_VMEM_LIMIT_BYTES),
        name="modulation",
    )(cc, w_mod, b_mod.reshape(depth, 1, n))


def _in_proj_kernel(*refs, n_x, has_prev, nc, hd, lru_w, q_lora, kv_lora, d):
    tile = pl.program_id(0)
    x_refs, refs = refs[:n_x], refs[n_x:]
    if has_prev:
        f_ref, pm_ref, refs = refs[0], refs[1], refs[2:]
    (g_ref, m_ref, w_ref, gq_ref, gk_ref, gc_ref, gu_ref, gd_ref, nqa_ref, nkv_ref, wq_ref, wkv_ref,
     qc_ref, qu_ref, qd_ref, kc_ref, ku_ref, kd_ref) = refs[:18]
    outs = refs[18:]
    x = _read_stream(x_refs, tile, nc)
    if has_prev:
        x = x + pm_ref[0, 0, 5:6, :] * f_ref[0]
        outs[0][0] = x
        outs = outs[1:]
    qg_ref, kg_ref, vg_ref, lru_ref, qm_ref, km_ref, vm_ref, gate_ref = outs
    tm = x.shape[0]
    h = _rms(x) * g_ref[0]
    h = h * (1.0 + m_ref[0, 0, 1:2, :]) + m_ref[0, 0, 0:1, :]
    hb = h.astype(BF16)

    nq, nk = GQA_HEADS * hd, GQA_KV_HEADS * hd
    w_gqa = nq + 2 * nk
    w_mla = q_lora + kv_lora + V7X_LANES
    o_lru, o_mla, o_gate = w_gqa, w_gqa + 2 * lru_w, w_gqa + 2 * lru_w + w_mla

    z = jnp.dot(hb, w_ref[0, :, 0:w_gqa], preferred_element_type=F32)

    def head_norm(a, n_heads):
        return jnp.concatenate([_rms(a[:, i * hd:(i + 1) * hd]) for i in range(n_heads)], axis=-1)

    cos, su, sd = gc_ref[...], gu_ref[...], gd_ref[...]
    q = head_norm(z[:, :nq], GQA_HEADS) * gq_ref[0]
    q = _rope_lanes(q, _tiled(cos, nq), _tiled(su, nq), _tiled(sd, nq), hd // 4)
    qg_ref[0] = (q * (LOG2E * hd ** -0.5)).astype(BF16)
    k = head_norm(z[:, nq:nq + nk], GQA_KV_HEADS) * gk_ref[0]
    k = _rope_lanes(k, _tiled(cos, nk), _tiled(su, nk), _tiled(sd, nk), hd // 4)
    v_t = z[:, nq + nk:nq + 2 * nk].T
    tail = _ones_row_tail(V_ROWS - hd, tm)
    for i in range(GQA_KV_HEADS):
        kg_ref[0, i] = k[:, i * hd:(i + 1) * hd].astype(BF16)
        vg_ref[0, i, 0:hd, :] = v_t[i * hd:(i + 1) * hd, :].astype(BF16)
        vg_ref[0, i, hd:V_ROWS, :] = tail

    lru_ref[0] = jnp.dot(hb, w_ref[0, :, o_lru:o_lru + 2 * lru_w], preferred_element_type=F32)

    z = jnp.dot(hb, w_ref[0, :, o_mla:o_mla + w_mla], preferred_element_type=F32)
    cq = (_rms(z[:, :q_lora]) * nqa_ref[0]).astype(BF16)
    qm = jnp.dot(cq, wq_ref[0], preferred_element_type=F32)
    wq = qm.shape[-1]
    qm = _rope_lanes(qm, _tiled(qc_ref[...], wq), _tiled(qu_ref[...], wq), _tiled(qd_ref[...], wq),
                     MLA_ROPE // 4)
    qm_ref[0] = (qm * (LOG2E * MLA_QK ** -0.5)).astype(BF16)
    ckv = (_rms(z[:, q_lora:q_lora + kv_lora]) * nkv_ref[0]).astype(BF16)
    kv = jnp.dot(ckv, wkv_ref[0], preferred_element_type=F32)
    kr = _rope_lanes(z[:, q_lora + kv_lora:], kc_ref[...], ku_ref[...], kd_ref[...], MLA_ROPE // 4)
    kr = kr[:, :MLA_ROPE]
    per = MLA_NOPE + MLA_V
    tail = _ones_row_tail(V_ROWS - MLA_V, tm)
    for i in range(MLA_HEADS):
        km_ref[0, i] = jnp.concatenate([kv[:, i * per:i * per + MLA_NOPE], kr], axis=-1).astype(BF16)
        head_t = kv[:, i * per:(i + 1) * per].T
        vm_ref[0, i, 0:MLA_V, :] = head_t[MLA_NOPE:per, :].astype(BF16)
        vm_ref[0, i, MLA_V:V_ROWS, :] = tail

    gate_ref[0] = _sigmoid(jnp.dot(hb, w_ref[0, :, o_gate:o_gate + 3 * d],
                                   preferred_element_type=F32)).astype(BF16)


def _in_proj(streams, prev, layer, p, n_ctx):
    b, d = streams[0].shape[0], streams[0].shape[-1]
    t = sum(s.shape[1] for s in streams) if len(streams) == 2 else streams[0].shape[1]
    tm = _tile(math.gcd(n_ctx, t), ROW_TILE)
    nc = n_ctx // tm
    hd, lru_w, q_lora, kv_lora = p["hd"], p["lru_w"], p["q_lora"], p["kv_lora"]
    nq, nk = GQA_HEADS * hd, GQA_KV_HEADS * hd
    qw = MLA_HEADS * MLA_QK
    tile_of, batch_of = (lambda i, bb: i), (lambda i, bb: bb)
    row = lambda i, bb: (bb, i, 0)
    mod_row = lambda i, bb: jnp.where(i < nc, 0, bb + 1)
    lay3 = lambda i, bb: (layer, 0, 0)
    tbl = lambda a: pl.BlockSpec((tm, a.shape[-1]), lambda i, bb: (i, 0))
    x_spec = pl.BlockSpec((1, tm, d), row)

    in_specs = _stream_specs(streams, tm, nc, tile_of, batch_of)
    args = list(streams)
    out_specs, out_shape = [], []
    if prev is not None:
        in_specs += [x_spec, pl.BlockSpec((1, 1, N_MOD, d), lambda i, bb: (prev[1], mod_row(i, bb), 0, 0))]
        args += [prev[0], p["mods"]]
        out_specs.append(x_spec)
        out_shape.append(jax.ShapeDtypeStruct((b, t, d), F32))
    in_specs += [pl.BlockSpec((1, 1, d), lay3),
                 pl.BlockSpec((1, 1, N_MOD, d), lambda i, bb: (layer, mod_row(i, bb), 0, 0)),
                 pl.BlockSpec((1,) + p["w_in"].shape[1:], lay3, pipeline_mode=pl.Buffered(1)),
                 pl.BlockSpec((1, 1, nq), lay3), pl.BlockSpec((1, 1, nk), lay3)]
    args += [p["norm_mix"], p["mods"], p["w_in"], p["gq"], p["gk"]]
    in_specs += [tbl(a) for a in p["gqa_tabs"]]
    args += list(p["gqa_tabs"])
    in_specs += [pl.BlockSpec((1, 1, q_lora), lay3), pl.BlockSpec((1, 1, kv_lora), lay3),
                 pl.BlockSpec((1,) + p["w_qb"].shape[1:], lay3),
                 pl.BlockSpec((1,) + p["w_kvb"].shape[1:], lay3)]
    args += [p["mla_q_norm"], p["mla_kv_norm"], p["w_qb"], p["w_kvb"]]
    in_specs += [tbl(a) for a in p["mla_q_tabs"]] + [tbl(a) for a in p["mla_k_tabs"]]
    args += list(p["mla_q_tabs"]) + list(p["mla_k_tabs"])

    def kv_specs(heads, dk):
        return [pl.BlockSpec((1, heads, tm, dk), lambda i, bb: (bb, 0, i, 0)),
                pl.BlockSpec((1, heads, V_ROWS, tm), lambda i, bb: (bb, 0, 0, i))]

    out_specs += ([pl.BlockSpec((1, tm, nq), row)] + kv_specs(GQA_KV_HEADS, hd)
                  + [pl.BlockSpec((1, tm, 2 * lru_w), row), pl.BlockSpec((1, tm, qw), row)]
                  + kv_specs(MLA_HEADS, MLA_QK) + [pl.BlockSpec((1, tm, 3 * d), row)])
    out_shape += [jax.ShapeDtypeStruct((b, t, nq), BF16),
                  jax.ShapeDtypeStruct((b, GQA_KV_HEADS, t, hd), BF16),
                  jax.ShapeDtypeStruct((b, GQA_KV_HEADS, V_ROWS, t), BF16),
                  jax.ShapeDtypeStruct((b, t, 2 * lru_w), F32),
                  jax.ShapeDtypeStruct((b, t, qw), BF16),
                  jax.ShapeDtypeStruct((b, MLA_HEADS, t, MLA_QK), BF16),
                  jax.ShapeDtypeStruct((b, MLA_HEADS, V_ROWS, t), BF16),
                  jax.ShapeDtypeStruct((b, t, 3 * d), BF16)]
    outs = pl.pallas_call(
        functools.partial(_in_proj_kernel, n_x=len(streams), has_prev=prev is not None, nc=nc, hd=hd,
                          lru_w=lru_w, q_lora=q_lora, kv_lora=kv_lora, d=d),
        grid=(t // tm, b),
        in_specs=in_specs, out_specs=out_specs, out_shape=out_shape,
        compiler_params=_cparams(("arbitrary", "arbitrary"), V7X_VMEM_LIMIT_BYTES),
        name="in_proj",
    )(*args)
    if prev is not None:
        return (outs[0],), outs[1:]
    return streams, outs


def _lru_kernel(z_ref, cw_ref, cb_ref, wg_ref, bg_ref, lam_ref, o_ref, a_f, b_f, a_b, b_b, *, n_ctx, chunk):
    t, width = a_f.shape
    n_chunks = t // chunk
    row = lax.broadcasted_iota(jnp.int32, (chunk, 1), 0)

    def conv_chunk(c):
        r0 = c * chunk
        seg_lo, seg_hi = (0, n_ctx) if r0 < n_ctx else (n_ctx, t)
        u = jnp.zeros((chunk, width), F32) + cb_ref[0]
        for j in range(CONV_WIDTH):
            off = j - 1
            lo = min(max(r0 + off, 0), t - chunk)
            tap = z_ref[0, lo:lo + chunk, 0:width]
            shift = (lo - (r0 + off)) % chunk
            if shift:
                tap = pltpu.roll(tap, shift, 0)
            if r0 + off < seg_lo or r0 + off + chunk > seg_hi:
                pos = row + (r0 + off)
                tap = jnp.where((pos >= seg_lo) & (pos < seg_hi), tap, 0.0)
            u = u + tap * cw_ref[0, j:j + 1, :]
        return u

    scr = ((a_f, b_f), (a_b, b_b))
    half_c_sp = []
    for dd in range(2):
        lam = lam_ref[0, dd:dd + 1, :]
        sp = jnp.maximum(-lam, 0.0) + jnp.log1p(jnp.exp(-jnp.abs(lam)))
        half_c_sp.append((-0.5 * LRU_C) * sp)
    for c in range(n_chunks):
        u = conv_chunk(c)
        g = jnp.dot(u.astype(BF16), wg_ref[0], preferred_element_type=F32) + bg_ref[0]
        half_u = 0.5 * u
        for dd in range(2):
            base = 2 * dd * width
            t_r = jnp.tanh(g[:, base:base + width])
            t_i = jnp.tanh(g[:, base + width:base + 2 * width])
            log_a = (1.0 + t_r) * half_c_sp[dd]
            th = jnp.tanh(log_a)
            v = (-2.0 * th) / (1.0 - th)
            root = jnp.where(v > 0.0, v * lax.rsqrt(v), 0.0)
            scr[dd][0][c * chunk:(c + 1) * chunk, :] = jnp.exp(log_a)
            scr[dd][1][c * chunk:(c + 1) * chunk, :] = root * ((1.0 + t_i) * half_u)

    ctx_blk = n_ctx // V7X_SUBLANES
    all_blk = t // V7X_SUBLANES
    nsub = V7X_SUBLANES

    def body(i, carry):
        hf, hb = carry
        rf = pl.multiple_of(i * nsub, nsub)
        jb = jnp.where(i < ctx_blk, ctx_blk - 1 - i, all_blk - 1 - (i - ctx_blk))
        rb = pl.multiple_of(jb * nsub, nsub)
        af, bf = a_f[pl.ds(rf, nsub), :], b_f[pl.ds(rf, nsub), :]
        ab, bb = a_b[pl.ds(rb, nsub), :], b_b[pl.ds(rb, nsub), :]
        rows_f, rows_b = [None] * nsub, [None] * nsub
        for s in range(nsub):
            hf = af[s:s + 1, :] * hf + bf[s:s + 1, :]
            rows_f[s] = hf
            sb = nsub - 1 - s
            hb = ab[sb:sb + 1, :] * hb + bb[sb:sb + 1, :]
            rows_b[sb] = hb
        a_f[pl.ds(rf, nsub), :] = jnp.concatenate(rows_f, axis=0)
        a_b[pl.ds(rb, nsub), :] = jnp.concatenate(rows_b, axis=0)
        return hf, hb

    h0 = jnp.zeros((1, width), F32)
    lax.fori_loop(0, all_blk, body, (h0, h0))

    k0 = math.sqrt(2.0 / math.pi)
    for c in range(n_chunks):
        sl = slice(c * chunk, (c + 1) * chunk)
        y = z_ref[0, sl, width:2 * width]
        gelu = 0.5 * y * (1.0 + jnp.tanh(k0 * (y + 0.044715 * (y * y * y))))
        o_ref[0, sl, :] = ((a_f[sl, :] + a_b[sl, :]) * gelu).astype(BF16)


def _lru(zlru, layer, p, n_ctx):
    b, t, w2 = zlru.shape
    width = w2 // 2
    chunk = _tile(math.gcd(n_ctx, t), ROW_TILE)
    lay3 = lambda bb: (layer, 0, 0)
    return pl.pallas_call(
        functools.partial(_lru_kernel, n_ctx=n_ctx, chunk=chunk),
        grid=(b,),
        in_specs=[pl.BlockSpec((1, t, w2), lambda bb: (bb, 0, 0)),
                  pl.BlockSpec((1, CONV_WIDTH, width), lay3),
                  pl.BlockSpec((1, 1, width), lay3),
                  pl.BlockSpec((1,) + p["lru_wg"].shape[1:], lay3),
                  pl.BlockSpec((1, 1, 4 * width), lay3),
                  pl.BlockSpec((1, 2, width), lay3)],
        out_specs=pl.BlockSpec((1, t, width), lambda bb: (bb, 0, 0)),
        out_shape=jax.ShapeDtypeStruct((b, t, width), BF16),
        scratch_shapes=[pltpu.VMEM((t, width), F32)] * 4,
        compiler_params=_cparams(("arbitrary",), V7X_VMEM_LIMIT_BYTES),
        name="rg_lru",
    )(zlru, p["conv_w"], p["conv_b"], p["lru_wg"], p["lru_bg"], p["lru_lam"])


def _attn_kernel(q_ref, k_ref, vt_ref, o_ref, s0, s1, p0, p1, *, gqa, dk, dv, n_ctx, n_lat, with_ctx, tq):
    t = n_ctx + n_lat
    sbuf, pbuf = (s0, s1), (p0, p1)
    out_off = 0 if with_ctx else n_ctx
    nt_dims = (((1,), (1,)), ((), ()))

    def n_sub(rows):
        return GQA_KV_HEADS * (rows // ATTN_SUB) if gqa else MLA_HEADS

    def load_q(r0, j, rows):
        if gqa:
            kvh, piece = j % GQA_KV_HEADS, j // GQA_KV_HEADS
            blk = q_ref[0, pl.ds(r0 + piece * ATTN_SUB, ATTN_SUB), kvh * GQA_GROUP * dk:(kvh + 1) * GQA_GROUP * dk]
            return jnp.concatenate([blk[:, h * dk:(h + 1) * dk] for h in range(GQA_GROUP)], axis=0), kvh
        return q_ref[0, pl.ds(r0, rows), j * dk:(j + 1) * dk], j

    def qk(r0, j, rows, n_keys, par):
        q, kh = load_q(r0, j, rows)
        sbuf[par][0:n_keys, 0:q.shape[0]] = lax.dot_general(
            k_ref[0, kh, 0:n_keys, :], q, nt_dims, preferred_element_type=F32)

    def finish(r0, j, rows, n_keys, par):
        cols = GQA_GROUP * ATTN_SUB if gqa else rows
        kh = j % GQA_KV_HEADS if gqa else j
        s = sbuf[par][0:n_keys, 0:cols]
        m = jnp.max(s, axis=0, keepdims=True)
        pbuf[par][0:n_keys, 0:cols] = jnp.exp2(s - m).astype(BF16)
        ot = jnp.dot(vt_ref[0, kh, :, 0:n_keys], pbuf[par][0:n_keys, 0:cols], preferred_element_type=F32)
        o = (ot[0:dv, :] / ot[dv:dv + 1, :]).T.astype(o_ref.dtype)
        if gqa:
            piece = j // GQA_KV_HEADS
            out = jnp.concatenate([o[h * ATTN_SUB:(h + 1) * ATTN_SUB, :] for h in range(GQA_GROUP)], axis=-1)
            o_ref[0, pl.ds(r0 - out_off + piece * ATTN_SUB, ATTN_SUB),
                  kh * GQA_GROUP * dv:(kh + 1) * GQA_GROUP * dv] = out
        else:
            o_ref[0, pl.ds(r0 - out_off, rows), j * dv:(j + 1) * dv] = o

    if with_ctx:
        rows_c = min(tq, n_ctx)
        for r in range(0, n_ctx, rows_c):
            for j in range(n_sub(rows_c)):
                qk(r, j, rows_c, n_ctx, 0)
                finish(r, j, rows_c, n_ctx, 0)

    n_tiles = n_lat // tq
    ns = n_sub(tq)
    assert ns % 2 == 0
    qk(n_ctx, 0, tq, t, 0)

    def body(i, carry):
        r0 = pl.multiple_of(n_ctx + i * tq, ATTN_SUB)
        r_next = pl.multiple_of(n_ctx + jnp.minimum(i + 1, n_tiles - 1) * tq, ATTN_SUB)
        for j in range(ns):
            if j + 1 < ns:
                qk(r0, j + 1, tq, t, (j + 1) % 2)
            else:
                qk(r_next, 0, tq, t, 0)
            finish(r0, j, tq, t, j % 2)
        return carry

    lax.fori_loop(0, n_tiles, body, 0)


def _attention(q, k, vt, *, gqa, n_ctx, with_ctx, name):
    b, t, qw = q.shape
    hk, dk = k.shape[1], k.shape[-1]
    dv = MLA_V if not gqa else dk
    n_lat = t - n_ctx
    tq = _tile(n_lat, ATTN_Q_TILE)
    cols = GQA_GROUP * ATTN_SUB if gqa else tq
    t_out = t if with_ctx else n_lat
    w_out = (qw // dk) * dv
    kern = functools.partial(_attn_kernel, gqa=gqa, dk=dk, dv=dv, n_ctx=n_ctx, n_lat=n_lat,
                             with_ctx=with_ctx, tq=tq)
    return pl.pallas_call(
        kern,
        grid=(b,),
        in_specs=[pl.BlockSpec((1, t, qw), lambda bb: (bb, 0, 0)),
                  pl.BlockSpec((1, hk, t, dk), lambda bb: (bb, 0, 0, 0)),
                  pl.BlockSpec((1, hk, V_ROWS, t), lambda bb: (bb, 0, 0, 0))],
        out_specs=pl.BlockSpec((1, t_out, w_out), lambda bb: (bb, 0, 0)),
        out_shape=jax.ShapeDtypeStruct((b, t_out, w_out), BF16),
        scratch_shapes=[pltpu.VMEM((t, cols), F32), pltpu.VMEM((t, cols), F32),
                        pltpu.VMEM((t, cols), BF16), pltpu.VMEM((t, cols), BF16)],
        compiler_params=_cparams(("arbitrary",), V7X_VMEM_LIMIT_BYTES),
        name=name,
    )(q, k, vt)


def _merge_kernel(*refs, n_x, nc, d, n_exp):
    x_refs, refs = refs[:n_x], refs[n_x:]
    (oa_ref, ol_ref, om_ref, gt_ref, m_ref, g_ref, wa_ref, wl_ref, wm_ref, wo_ref, rh_ref, rl_ref, rb_ref,
     xo_ref, h_ref, ids_ref, wts_ref) = refs
    gates = gt_ref[0]
    merged = (gates[:, 0:d].astype(F32) * jnp.dot(oa_ref[0], wa_ref[0], preferred_element_type=F32)
              + gates[:, d:2 * d].astype(F32) * jnp.dot(ol_ref[0], wl_ref[0], preferred_element_type=F32)
              + gates[:, 2 * d:3 * d].astype(F32) * jnp.dot(om_ref[0], wm_ref[0], preferred_element_type=F32))
    y = jnp.dot(merged.astype(BF16), wo_ref[0], preferred_element_type=F32)
    x = _read_stream(x_refs, pl.program_id(1), nc) + m_ref[0, 0, 2:3, :] * y
    xo_ref[0] = x
    h = _rms(x) * g_ref[0]
    h = h * (1.0 + m_ref[0, 0, 4:5, :]) + m_ref[0, 0, 3:4, :]
    h_hi = h.astype(BF16)
    h_ref[0] = h_hi
    h_lo = (h - h_hi.astype(F32)).astype(BF16)

    dn = (((1,), (1,)), ((), ()))
    logits = (lax.dot_general(rh_ref[...], h_hi, dn, preferred_element_type=F32)
              + lax.dot_general(rh_ref[...], h_lo, dn, preferred_element_type=F32)
              + lax.dot_general(rl_ref[...], h_hi, dn, preferred_element_type=F32))
    scores = _sigmoid(logits)
    sel = scores + rb_ref[...]
    per = n_exp // N_GROUPS
    gs = []
    for g in range(N_GROUPS):
        r = [sel[g * per + j:g * per + j + 1, :] for j in range(per)]
        best = None
        for a in range(per):
            for bq in range(a + 1, per):
                pair = r[a] + r[bq]
                best = pair if best is None else jnp.maximum(best, pair)
        gs.append(best)
    gmax = functools.reduce(jnp.maximum, gs)
    gbest = jnp.full(gmax.shape, N_GROUPS - 1, jnp.int32)
    for g in range(N_GROUPS - 2, -1, -1):
        gbest = jnp.where(gs[g] == gmax, g, gbest)
    eid = lax.broadcasted_iota(jnp.int32, sel.shape, 0)
    gid = jnp.zeros(sel.shape, jnp.int32)
    for g in range(1, N_GROUPS):
        gid = gid + (eid >= g * per).astype(jnp.int32)
    masked = jnp.where(gid == gbest, sel, -jnp.inf)
    m1 = jnp.max(masked, axis=0, keepdims=True)
    i1 = jnp.min(jnp.where(masked == m1, eid, n_exp), axis=0, keepdims=True)
    masked2 = jnp.where(eid == i1, -jnp.inf, masked)
    m2 = jnp.max(masked2, axis=0, keepdims=True)
    i2 = jnp.min(jnp.where(masked2 == m2, eid, n_exp), axis=0, keepdims=True)
    s1 = jnp.sum(jnp.where(eid == i1, scores, 0.0), axis=0, keepdims=True)
    s2 = jnp.sum(jnp.where(eid == i2, scores, 0.0), axis=0, keepdims=True)
    tot = s1 + s2
    ids_ref[0] = jnp.concatenate([i1, i2], axis=0)
    wts_ref[0] = jnp.concatenate([ROUTED_SCALE * s1 / tot, ROUTED_SCALE * s2 / tot], axis=0)


def _merge(streams, o_attn, o_lru, o_mla, gates, layer, p, n_ctx, with_ctx):
    b, d = streams[0].shape[0], streams[0].shape[-1]
    t = o_lru.shape[1]
    n_exp = p["r_hi"].shape[0]
    tm = _tile(math.gcd(n_ctx, t), ROW_TILE)
    off = 0 if with_ctx else n_ctx // tm
    nc = n_ctx // tm
    tq = t if with_ctx else t - n_ctx
    nt = tq // tm
    full = lambda bb, i: (bb, i + off, 0)
    qrow = lambda bb, i: (bb, i, 0)
    const = lambda bb, i: (0, 0)
    lay3 = lambda bb, i: (layer, 0, 0)
    bw = o_attn.shape[-1]
    route_spec = pl.BlockSpec((1, 2, tm), lambda bb, i: (bb * nt + i, 0, 0))
    wspec = lambda a: pl.BlockSpec((1,) + a.shape[1:], lay3)
    in_specs = _stream_specs(streams, tm, nc, lambda bb, i: i + off, lambda bb, i: bb)
    in_specs += [pl.BlockSpec((1, tm, bw), qrow),
                 pl.BlockSpec((1, tm, o_lru.shape[-1]), full),
                 pl.BlockSpec((1, tm, bw), qrow),
                 pl.BlockSpec((1, tm, 3 * d), full),
                 pl.BlockSpec((1, 1, N_MOD, d), lambda bb, i: (layer, jnp.where(i + off < nc, 0, bb + 1), 0, 0)),
                 pl.BlockSpec((1, 1, d), lay3),
                 wspec(p["w_ba"]), wspec(p["w_bl"]), wspec(p["w_bm"]), wspec(p["w_out"]),
                 pl.BlockSpec(p["r_hi"].shape, const), pl.BlockSpec(p["r_lo"].shape, const),
                 pl.BlockSpec(p["r_bias"].shape, const)]
    return pl.pallas_call(
        functools.partial(_merge_kernel, n_x=len(streams), nc=nc - off, d=d, n_exp=n_exp),
        grid=(b, nt),
        in_specs=in_specs,
        out_specs=[pl.BlockSpec((1, tm, d), qrow), pl.BlockSpec((1, tm, d), qrow),
                   route_spec, route_spec],
        out_shape=[jax.ShapeDtypeStruct((b, tq, d), F32), jax.ShapeDtypeStruct((b, tq, d), BF16),
                   jax.ShapeDtypeStruct((b * nt, 2, tm), jnp.int32),
                   jax.ShapeDtypeStruct((b * nt, 2, tm), F32)],
        compiler_params=_cparams(("arbitrary", "arbitrary"), V7X_VMEM_LIMIT_BYTES),
        name="merge_router",
    )(*streams, o_attn, o_lru, o_mla, gates, p["mods"], p["norm_ffn"], p["w_ba"], p["w_bl"], p["w_bm"],
      p["w_out"], p["r_hi"], p["r_lo"], p["r_bias"])


def _moe_kernel(h_ref, idr_ref, idc_ref, wtc_ref, low_ref, wg_ref, wu_ref, wd_ref, o_ref,
                xs, ys, cs, posc, meta, *, per, n_pairs):
    s = pl.program_id(1)
    tm = h_ref.shape[0]
    slots = xs.shape[0]
    pairs_per_group = per // 2

    @pl.when(s == 0)
    def _sort_tokens():
        e_row = idr_ref[0, 0:1, :]
        e_col = idc_ref[:, 0:1]
        g_row = jnp.zeros(e_row.shape, jnp.int32)
        g_col = jnp.zeros(e_col.shape, jnp.int32)
        for g in range(1, N_GROUPS):
            g_row = g_row + (e_row >= g * per).astype(jnp.int32)
            g_col = g_col + (e_col >= g * per).astype(jnp.int32)
        sub8 = lax.broadcasted_iota(jnp.int32, (V7X_SUBLANES, tm), 0)
        lane = lax.broadcasted_iota(jnp.int32, (tm, V7X_LANES), 1)
        onehot_row = jnp.where(sub8 == g_row, 1.0, 0.0)
        onehot_col = jnp.where(lane == g_col, 1.0, 0.0)
        rank_col = jnp.dot(low_ref[...], onehot_col.astype(BF16), preferred_element_type=F32)
        rank_row = lax.dot_general(onehot_row.astype(BF16), low_ref[...], (((1,), (1,)), ((), ())),
                                   preferred_element_type=F32)
        start = 0
        start_row = jnp.zeros((V7X_SUBLANES, 1), F32)
        start_col = jnp.zeros((1, V7X_LANES), F32)
        for g in range(N_GROUPS):
            cnt = jnp.sum((g_row == g).astype(jnp.int32))
            n_sub = lax.shift_right_logical(cnt + (V7X_LANES - 1), 7)
            meta[g] = start
            meta[N_GROUPS + g] = n_sub
            start_f = jnp.asarray(start, jnp.int32).astype(F32)
            start_row = jnp.where(lax.broadcasted_iota(jnp.int32, start_row.shape, 0) == g, start_f, start_row)
            start_col = jnp.where(lax.broadcasted_iota(jnp.int32, start_col.shape, 1) == g, start_f, start_col)
            start = start + n_sub * V7X_LANES
        pos_row = jnp.sum(onehot_row * (rank_row + start_row), axis=0, keepdims=True).astype(jnp.int32)
        pos_col = jnp.sum(onehot_col * (rank_col + start_col), axis=1, keepdims=True).astype(jnp.int32)
        posc[...] = pos_col
        slot = lax.broadcasted_iota(jnp.int32, (slots, tm), 0)
        perm = jnp.where(slot == pos_row, 1.0, 0.0).astype(BF16)
        xs[...] = jnp.dot(perm, h_ref[...], preferred_element_type=F32).astype(BF16)
        ids_c, wts_c = idc_ref[...], wtc_ref[...]
        comb = jnp.zeros((tm, V7X_LANES), F32)
        for k in range(2):
            local = jnp.bitwise_and(ids_c[:, k:k + 1], per - 1)
            comb = comb + jnp.where(lane == local, wts_c[:, k:k + 1], 0.0)
        c_hi = comb.astype(BF16)
        c_lo = (comb - c_hi.astype(F32)).astype(BF16)
        cs[...] = (jnp.dot(perm, c_hi, preferred_element_type=F32)
                   + jnp.dot(perm, c_lo, preferred_element_type=F32))
        ys[...] = jnp.zeros_like(ys)

    grp = s // pairs_per_group
    first_local = (s % pairs_per_group) * 2
    seg_start = meta[grp]
    lane_sub = lax.broadcasted_iota(jnp.int32, (V7X_LANES, V7X_LANES), 1)

    def sub_tile(k, carry):
        r0 = pl.multiple_of(seg_start + k * V7X_LANES, V7X_LANES)
        x = xs[pl.ds(r0, V7X_LANES), :]
        c_all = cs[pl.ds(r0, V7X_LANES), :]
        acc = None
        for jj in range(2):
            c = jnp.sum(jnp.where(lane_sub == first_local + jj, c_all, 0.0), axis=-1, keepdims=True)
            g = jnp.dot(x, wg_ref[0, jj], preferred_element_type=F32)
            u = jnp.dot(x, wu_ref[0, jj], preferred_element_type=F32)
            act = ((g * _sigmoid(g)) * u * c).astype(BF16)
            part = jnp.dot(act, wd_ref[0, jj], preferred_element_type=F32)
            acc = part if acc is None else acc + part
        ys[pl.ds(r0, V7X_LANES), :] = ys[pl.ds(r0, V7X_LANES), :] + acc
        return carry

    lax.fori_loop(0, meta[N_GROUPS + grp], sub_tile, 0)

    @pl.when(s == n_pairs - 1)
    def _unsort():
        slot_l = lax.broadcasted_iota(jnp.int32, (tm, slots), 1)
        perm_t = jnp.where(slot_l == posc[...], 1.0, 0.0).astype(BF16)
        y = ys[...]
        y_hi = y.astype(BF16)
        y_lo = (y - y_hi.astype(F32)).astype(BF16)
        o_ref[...] = (jnp.dot(perm_t, y_hi, preferred_element_type=F32)
                      + jnp.dot(perm_t, y_lo, preferred_element_type=F32))


def _moe(h, ids, wts, layer, p):
    n, d = h.shape
    _, n_exp, _, ff = p["moe_wg"].shape
    per = n_exp // N_GROUPS
    assert per % 2 == 0 and per & (per - 1) == 0
    rt = ids.shape[-1]
    tm = _tile(n, MOE_ROW_TILE)
    assert tm % rt == 0 and tm % V7X_LANES == 0
    slots = tm + N_GROUPS * V7X_LANES
    ids_row = ids.reshape(n // tm, tm // rt, 2, rt).transpose(0, 2, 1, 3).reshape(n // tm, 2, tm)
    ids_col = jnp.transpose(ids, (0, 2, 1)).reshape(n, 2)
    wts_col = jnp.transpose(wts, (0, 2, 1)).reshape(n, 2)
    low = jnp.tril(jnp.ones((tm, tm), BF16), -1)
    n_pairs = n_exp // 2
    wspec = lambda a, b: pl.BlockSpec((1, 2, a, b), lambda i, s: (layer, s, 0, 0))
    return pl.pallas_call(
        functools.partial(_moe_kernel, per=per, n_pairs=n_pairs),
        grid=(n // tm, n_pairs),
        in_specs=[pl.BlockSpec((tm, d), lambda i, s: (i, 0)),
                  pl.BlockSpec((1, 2, tm), lambda i, s: (i, 0, 0)),
                  pl.BlockSpec((tm, 2), lambda i, s: (i, 0)),
                  pl.BlockSpec((tm, 2), lambda i, s: (i, 0)),
                  pl.BlockSpec((tm, tm), lambda i, s: (0, 0), pipeline_mode=pl.Buffered(1)),
                  wspec(d, ff), wspec(d, ff), wspec(ff, d)],
        out_specs=pl.BlockSpec((tm, d), lambda i, s: (i, 0)),
        out_shape=jax.ShapeDtypeStruct((n, d), F32),
        scratch_shapes=[pltpu.VMEM((slots, d), BF16), pltpu.VMEM((slots, d), F32),
                        pltpu.VMEM((slots, V7X_LANES), F32), pltpu.VMEM((tm, 1), jnp.int32),
                        pltpu.SMEM((2 * N_GROUPS,), jnp.int32)],
        compiler_params=_cparams(("arbitrary", "arbitrary"), V7X_VMEM_LIMIT_BYTES),
        name="moe_ffn",
    )(h, ids_row, ids_col, wts_col, low, p["moe_wg"], p["moe_wu"], p["moe_wd"])


def _final_kernel(x_ref, f_ref, m_ref, g_ref, o_ref):
    x = x_ref[0] + m_ref[0, 0, 5:6, :] * f_ref[0]
    o_ref[0] = _rms(x) * g_ref[...]


def _final(x, f, mods, layer, g):
    b, s, d = x.shape
    tm = _tile(s, 2 * ROW_TILE)
    row = lambda bb, i: (bb, i, 0)
    return pl.pallas_call(
        _final_kernel,
        grid=(b, s // tm),
        in_specs=[pl.BlockSpec((1, tm, d), row), pl.BlockSpec((1, tm, d), row),
                  pl.BlockSpec((1, 1, N_MOD, d), lambda bb, i: (layer, bb + 1, 0, 0)),
                  pl.BlockSpec((1, d), lambda bb, i: (0, 0))],
        out_specs=pl.BlockSpec((1, tm, d), row),
        out_shape=jax.ShapeDtypeStruct((b, s, d), F32),
        compiler_params=_cparams(("arbitrary", "arbitrary")),
        name="final_norm",
    )(x, f, mods, g.reshape(1, d))


def _rope_tables(n_ctx, n_lat, dim, period, lane_off, width):
    quarter = dim // 4
    pos = jnp.arange(n_lat, dtype=F32)
    r, col = jnp.floor(pos / GRID_W), pos - GRID_W * jnp.floor(pos / GRID_W)
    inv_freq = ROPE_THETA ** (-jnp.arange(quarter, dtype=F32) / quarter)
    lane = np.arange(width)
    j = (lane % period) - lane_off
    active = (j >= 0) & (j < dim)
    jj = np.where(active, j, 0)
    use_col = jj >= dim // 2
    upper = (jj % (dim // 2)) >= quarter
    f = jj % quarter
    ang = jnp.where(use_col[None, :], col[:, None], r[:, None]) * inv_freq[f][None, :]
    act = jnp.asarray(active)[None, :]
    cos = jnp.where(act, jnp.cos(ang), 1.0)
    sin = jnp.where(act, jnp.sin(ang), 0.0)
    sin_up = jnp.where(jnp.asarray(~upper)[None, :], -sin, 0.0)
    sin_dn = jnp.where(jnp.asarray(upper)[None, :], sin, 0.0)
    ident = lambda v, fill: jnp.concatenate([jnp.full((n_ctx, width), fill, F32), v], axis=0)
    return ident(cos, 1.0), ident(sin_up, 0.0), ident(sin_dn, 0.0)


def _lru_gate_weights(w_a, w_i):
    depth, _, nb, k, j = w_a.shape
    w = jnp.stack([w_a, w_i], axis=2)
    eye = jnp.eye(nb, dtype=w.dtype)
    dense = jnp.einsum('dsgnkj,nm->dnksgmj', w, eye)
    return (0.5 * dense).reshape(depth, nb * k, 4 * nb * j).astype(BF16)


def kernel(x, c, ctx, c_ctx, w_mod, b_mod, norm_mix, norm_ffn, w_in, gqa_q_norm, gqa_k_norm, conv_w, conv_b,
           lru_w_a, lru_b_a, lru_w_i, lru_b_i, lru_lam, mla_q_a_norm, mla_w_qb, mla_kv_a_norm, mla_w_kvb,
           w_branch_attn, w_branch_lru, w_branch_mla, w_out, router_w, router_bias,
           moe_w_gate, moe_w_up, moe_w_down, final_norm):
    bsz, n_lat, d = x.shape
    n_ctx = ctx.shape[1]
    depth = w_mod.shape[0]
    hd = gqa_q_norm.shape[-1]
    lru_w = conv_w.shape[-1]
    q_lora, kv_lora = mla_q_a_norm.shape[-1], mla_kv_a_norm.shape[-1]
    n_exp = router_w.shape[-1]
    assert bsz + 1 <= MOD_ROWS and d % V7X_LANES == 0

    cc = jnp.zeros((MOD_ROWS, d), F32).at[0].set(c_ctx).at[1:1 + bsz].set(c)
    mods = _modulation(cc, w_mod, b_mod).reshape(depth, MOD_ROWS, N_MOD, d)

    mq_period = MLA_QK * V7X_LANES // math.gcd(MLA_QK, V7X_LANES)
    kr_end = GQA_HEADS * hd + 2 * GQA_KV_HEADS * hd + 2 * lru_w + q_lora + kv_lora + MLA_ROPE
    w_in_b = w_in.astype(BF16)
    w_in_b = jnp.concatenate([w_in_b[..., :kr_end], jnp.zeros((depth, d, V7X_LANES - MLA_ROPE), BF16),
                              w_in_b[..., kr_end:]], axis=-1)
    r_t = router_w.T
    r_hi = r_t.astype(BF16)
    p = dict(
        hd=hd, lru_w=lru_w, q_lora=q_lora, kv_lora=kv_lora, mods=mods,
        norm_mix=norm_mix.reshape(depth, 1, d), norm_ffn=norm_ffn.reshape(depth, 1, d), w_in=w_in_b,
        gq=jnp.tile(gqa_q_norm, (1, GQA_HEADS)).reshape(depth, 1, -1),
        gk=jnp.tile(gqa_k_norm, (1, GQA_KV_HEADS)).reshape(depth, 1, -1),
        gqa_tabs=_rope_tables(n_ctx, n_lat, hd, hd, 0, V7X_LANES),
        mla_q_tabs=_rope_tables(n_ctx, n_lat, MLA_ROPE, MLA_QK, MLA_NOPE, mq_period),
        mla_k_tabs=_rope_tables(n_ctx, n_lat, MLA_ROPE, V7X_LANES, 0, V7X_LANES),
        mla_q_norm=mla_q_a_norm.reshape(depth, 1, q_lora), mla_kv_norm=mla_kv_a_norm.reshape(depth, 1, kv_lora),
        w_qb=mla_w_qb.astype(BF16), w_kvb=mla_w_kvb.astype(BF16),
        conv_w=conv_w, conv_b=conv_b.reshape(depth, 1, lru_w),
        lru_wg=_lru_gate_weights(lru_w_a, lru_w_i),
        lru_bg=0.5 * jnp.stack([lru_b_a, lru_b_i], axis=2).reshape(depth, 1, 4 * lru_w),
        lru_lam=lru_lam,
        w_ba=w_branch_attn.astype(BF16), w_bl=w_branch_lru.astype(BF16), w_bm=w_branch_mla.astype(BF16),
        w_out=w_out.astype(BF16), r_hi=r_hi, r_lo=(r_t - r_hi.astype(F32)).astype(BF16),
        r_bias=router_bias.reshape(n_exp, 1).astype(F32),
        moe_wg=moe_w_gate.astype(BF16), moe_wu=moe_w_up.astype(BF16), moe_wd=moe_w_down.astype(BF16),
    )

    streams = (ctx, x)
    prev = None
    for layer in range(depth):
        last = layer == depth - 1
        streams, (q_g, k_g, v_g, zlru, q_m, k_m, v_m, gates) = _in_proj(streams, prev, layer, p, n_ctx)
        o_attn = _attention(q_g, k_g, v_g, gqa=True, n_ctx=n_ctx, with_ctx=not last, name="gqa_attention")
        o_lru = _lru(zlru, layer, p, n_ctx)
        o_mla = _attention(q_m, k_m, v_m, gqa=False, n_ctx=n_ctx, with_ctx=not last, name="mla_attention")
        x_mid, h2, ids, wts = _merge(streams, o_attn, o_lru, o_mla, gates, layer, p, n_ctx, with_ctx=not last)

        n_tok = x_mid.shape[0] * x_mid.shape[1]
        f = _moe(h2.reshape(n_tok, d), ids, wts, layer, p).reshape(x_mid.shape)
        streams, prev = (x_mid,), (f, layer)

    return _final(streams[0], prev[0], mods, prev[1], final_norm)
```

```python
import functools
import math

import numpy as np
import jax
import jax.numpy as jnp
from jax import lax
from jax.experimental import pallas as pl
from jax.experimental.pallas import tpu as pltpu

F32 = jnp.float32
BF16 = jnp.bfloat16

GRID_W = 64
ROPE_THETA = 10000.0
NORM_EPS = 1e-6
N_MOD = 6
GQA_HEADS = 8
GQA_KV_HEADS = 2
GQA_GROUP = GQA_HEADS // GQA_KV_HEADS
MLA_HEADS = 8
MLA_NOPE = 64
MLA_ROPE = 32
MLA_V = 64
MLA_QK = MLA_NOPE + MLA_ROPE
CONV_WIDTH = 4
LRU_C = 8.0
N_GROUPS = 4
ROUTED_SCALE = 1.0
LOG2E = math.log2(math.e)

V7X_LANES = 128
V7X_SUBLANES = 8
V7X_VMEM_LIMIT_BYTES = 56 * 1024 * 1024

ROW_TILE = 256
ATTN_Q_TILE = 512
ATTN_SUB = 128
ATTN_DEPTH = 2
ATTN_KEY_BLOCK = 256
V_ROWS = 80
MERGE_PARTS = 2
MOE_ROW_TILE = 1024
MOD_ROWS = 16


def _cparams(sem, vmem=None):
    return pltpu.CompilerParams(dimension_semantics=sem, vmem_limit_bytes=vmem)


def _tile(n, pref):
    t = min(n, pref)
    while n % t or t % V7X_SUBLANES:
        t -= 1
    return t


def _sigmoid(x):
    return 0.5 * (1.0 + jnp.tanh(0.5 * x))


def _rms(x):
    return x * lax.rsqrt(jnp.mean(x * x, axis=-1, keepdims=True) + NORM_EPS)


def _tiled(tbl, width):
    reps = width // tbl.shape[-1]
    return tbl if reps == 1 else jnp.concatenate([tbl] * reps, axis=-1)


def _rope_lanes(x, cos, sin_up, sin_dn, half):
    outs = []
    for c in range(x.shape[-1] // V7X_LANES):
        sl = slice(c * V7X_LANES, (c + 1) * V7X_LANES)
        xc = x[:, sl]
        up = pltpu.roll(xc, V7X_LANES - half, 1)
        dn = pltpu.roll(xc, half, 1)
        outs.append(xc * cos[:, sl] + up * sin_up[:, sl] + dn * sin_dn[:, sl])
    return outs[0] if len(outs) == 1 else jnp.concatenate(outs, axis=-1)


def _ones_row_tail(rows, cols):
    r = lax.broadcasted_iota(jnp.int32, (rows, cols), 0)
    return jnp.where(r == 0, 1.0, 0.0).astype(BF16)


def _stream_specs(streams, tm, nc, tile_of, batch_of):
    d = streams[0].shape[-1]
    if len(streams) == 1:
        return [pl.BlockSpec((1, tm, d), lambda *g: (batch_of(*g), tile_of(*g), 0))]
    return [pl.BlockSpec((1, tm, d), lambda *g: (batch_of(*g), jnp.minimum(tile_of(*g), nc - 1), 0)),
            pl.BlockSpec((1, tm, d), lambda *g: (batch_of(*g), jnp.maximum(tile_of(*g) - nc, 0), 0))]


def _read_stream(refs, tile, nc):
    if len(refs) == 1:
        return refs[0][0]
    return jnp.where(tile < nc, refs[0][0], refs[1][0])


def _mod_kernel(c_ref, w_ref, b_ref, o_ref):
    c = c_ref[...]
    a = (c * _sigmoid(c)).astype(BF16)
    o_ref[0] = jnp.dot(a, w_ref[0].astype(BF16), preferred_element_type=F32) + b_ref[0]


def _modulation(cc, w_mod, b_mod):
    depth, d, n = w_mod.shape
    tn = _tile(n, 1536) if n % V7X_LANES == 0 else n
    return pl.pallas_call(
        _mod_kernel,
        grid=(depth, n // tn),
        in_specs=[pl.BlockSpec((MOD_ROWS, d), lambda l, j: (0, 0)),
                  pl.BlockSpec((1, d, tn), lambda l, j: (l, 0, j)),
                  pl.BlockSpec((1, 1, tn), lambda l, j: (l, 0, j))],
        out_specs=pl.BlockSpec((1, MOD_ROWS, tn), lambda l, j: (l, 0, j)),
        out_shape=jax.ShapeDtypeStruct((depth, MOD_ROWS, n), F32),
        compiler_params=_cparams(("arbitrary", "arbitrary"), V7X_VMEM_LIMIT_BYTES),
        name="modulation",
    )(cc, w_mod, b_mod.reshape(depth, 1, n))


def _in_proj_kernel(*refs, n_x, has_prev, nc, hd, lru_w, q_lora, kv_lora, d):
    tile = pl.program_id(0)
    x_refs, refs = refs[:n_x], refs[n_x:]
    if has_prev:
        f_ref, pm_ref, refs = refs[0], refs[1], refs[2:]
    (g_ref, m_ref, w_ref, gq_ref, gk_ref, gc_ref, gu_ref, gd_ref, nqa_ref, nkv_ref, wq_ref, wkv_ref,
     qc_ref, qu_ref, qd_ref, kc_ref, ku_ref, kd_ref) = refs[:18]
    outs = refs[18:]
    x = _read_stream(x_refs, tile, nc)
    if has_prev:
        x = x + pm_ref[0, 0, 5:6, :] * f_ref[0]
        outs[0][0] = x
        outs = outs[1:]
    qg_ref, kg_ref, vg_ref, lru_ref, qm_ref, km_ref, vm_ref, gate_ref = outs
    tm = x.shape[0]
    h = _rms(x) * g_ref[0]
    h = h * (1.0 + m_ref[0, 0, 1:2, :]) + m_ref[0, 0, 0:1, :]
    hb = h.astype(BF16)

    nq, nk = GQA_HEADS * hd, GQA_KV_HEADS * hd
    w_gqa = nq + 2 * nk
    w_mla = q_lora + kv_lora + V7X_LANES
    o_lru, o_mla, o_gate = w_gqa, w_gqa + 2 * lru_w, w_gqa + 2 * lru_w + w_mla

    z = jnp.dot(hb, w_ref[0, :, 0:w_gqa], preferred_element_type=F32)

    def head_norm(a, n_heads):
        return jnp.concatenate([_rms(a[:, i * hd:(i + 1) * hd]) for i in range(n_heads)], axis=-1)

    cos, su, sd = gc_ref[...], gu_ref[...], gd_ref[...]
    q = head_norm(z[:, :nq], GQA_HEADS) * gq_ref[0]
    q = _rope_lanes(q, _tiled(cos, nq), _tiled(su, nq), _tiled(sd, nq), hd // 4)
    qg_ref[0] = (q * (LOG2E * hd ** -0.5)).astype(BF16)
    k = head_norm(z[:, nq:nq + nk], GQA_KV_HEADS) * gk_ref[0]
    k = _rope_lanes(k, _tiled(cos, nk), _tiled(su, nk), _tiled(sd, nk), hd // 4)
    v_t = z[:, nq + nk:nq + 2 * nk].T
    tail = _ones_row_tail(V_ROWS - hd, tm)
    for i in range(GQA_KV_HEADS):
        kg_ref[0, i] = k[:, i * hd:(i + 1) * hd].astype(BF16)
        vg_ref[0, i, 0:hd, :] = v_t[i * hd:(i + 1) * hd, :].astype(BF16)
        vg_ref[0, i, hd:V_ROWS, :] = tail

    lru_ref[0] = jnp.dot(hb, w_ref[0, :, o_lru:o_lru + 2 * lru_w], preferred_element_type=F32)

    z = jnp.dot(hb, w_ref[0, :, o_mla:o_mla + w_mla], preferred_element_type=F32)
    cq = (_rms(z[:, :q_lora]) * nqa_ref[0]).astype(BF16)
    qm = jnp.dot(cq, wq_ref[0], preferred_element_type=F32)
    wq = qm.shape[-1]
    qm = _rope_lanes(qm, _tiled(qc_ref[...], wq), _tiled(qu_ref[...], wq), _tiled(qd_ref[...], wq),
                     MLA_ROPE // 4)
    qm_ref[0] = (qm * (LOG2E * MLA_QK ** -0.5)).astype(BF16)
    ckv = (_rms(z[:, q_lora:q_lora + kv_lora]) * nkv_ref[0]).astype(BF16)
    kv = jnp.dot(ckv, wkv_ref[0], preferred_element_type=F32)
    kr = _rope_lanes(z[:, q_lora + kv_lora:], kc_ref[...], ku_ref[...], kd_ref[...], MLA_ROPE // 4)
    kr = kr[:, :MLA_ROPE]
    per = MLA_NOPE + MLA_V
    tail = _ones_row_tail(V_ROWS - MLA_V, tm)
    for i in range(MLA_HEADS):
        km_ref[0, i] = jnp.concatenate([kv[:, i * per:i * per + MLA_NOPE], kr], axis=-1).astype(BF16)
        head_t = kv[:, i * per:(i + 1) * per].T
        vm_ref[0, i, 0:MLA_V, :] = head_t[MLA_NOPE:per, :].astype(BF16)
        vm_ref[0, i, MLA_V:V_ROWS, :] = tail

    gate_ref[0] = _sigmoid(jnp.dot(hb, w_ref[0, :, o_gate:o_gate + 3 * d],
                                   preferred_element_type=F32)).astype(BF16)


def _in_proj(streams, prev, layer, p, n_ctx):
    b, d = streams[0].shape[0], streams[0].shape[-1]
    t = sum(s.shape[1] for s in streams) if len(streams) == 2 else streams[0].shape[1]
    tm = _tile(math.gcd(n_ctx, t), ROW_TILE)
    nc = n_ctx // tm
    hd, lru_w, q_lora, kv_lora = p["hd"], p["lru_w"], p["q_lora"], p["kv_lora"]
    nq, nk = GQA_HEADS * hd, GQA_KV_HEADS * hd
    qw = MLA_HEADS * MLA_QK
    tile_of, batch_of = (lambda i, bb: i), (lambda i, bb: bb)
    row = lambda i, bb: (bb, i, 0)
    mod_row = lambda i, bb: jnp.where(i < nc, 0, bb + 1)
    lay3 = lambda i, bb: (layer, 0, 0)
    tbl = lambda a: pl.BlockSpec((tm, a.shape[-1]), lambda i, bb: (i, 0))
    x_spec = pl.BlockSpec((1, tm, d), row)

    in_specs = _stream_specs(streams, tm, nc, tile_of, batch_of)
    args = list(streams)
    out_specs, out_shape = [], []
    if prev is not None:
        in_specs += [x_spec, pl.BlockSpec((1, 1, N_MOD, d), lambda i, bb: (prev[1], mod_row(i, bb), 0, 0))]
        args += [prev[0], p["mods"]]
        out_specs.append(x_spec)
        out_shape.append(jax.ShapeDtypeStruct((b, t, d), F32))
    in_specs += [pl.BlockSpec((1, 1, d), lay3),
                 pl.BlockSpec((1, 1, N_MOD, d), lambda i, bb: (layer, mod_row(i, bb), 0, 0)),
                 pl.BlockSpec((1,) + p["w_in"].shape[1:], lay3, pipeline_mode=pl.Buffered(1)),
                 pl.BlockSpec((1, 1, nq), lay3), pl.BlockSpec((1, 1, nk), lay3)]
    args += [p["norm_mix"], p["mods"], p["w_in"], p["gq"], p["gk"]]
    in_specs += [tbl(a) for a in p["gqa_tabs"]]
    args += list(p["gqa_tabs"])
    in_specs += [pl.BlockSpec((1, 1, q_lora), lay3), pl.BlockSpec((1, 1, kv_lora), lay3),
                 pl.BlockSpec((1,) + p["w_qb"].shape[1:], lay3),
                 pl.BlockSpec((1,) + p["w_kvb"].shape[1:], lay3)]
    args += [p["mla_q_norm"], p["mla_kv_norm"], p["w_qb"], p["w_kvb"]]
    in_specs += [tbl(a) for a in p["mla_q_tabs"]] + [tbl(a) for a in p["mla_k_tabs"]]
    args += list(p["mla_q_tabs"]) + list(p["mla_k_tabs"])

    def kv_specs(heads, dk):
        return [pl.BlockSpec((1, heads, tm, dk), lambda i, bb: (bb, 0, i, 0)),
                pl.BlockSpec((1, heads, V_ROWS, tm), lambda i, bb: (bb, 0, 0, i))]

    out_specs += ([pl.BlockSpec((1, tm, nq), row)] + kv_specs(GQA_KV_HEADS, hd)
                  + [pl.BlockSpec((1, tm, 2 * lru_w), row), pl.BlockSpec((1, tm, qw), row)]
                  + kv_specs(MLA_HEADS, MLA_QK) + [pl.BlockSpec((1, tm, 3 * d), row)])
    out_shape += [jax.ShapeDtypeStruct((b, t, nq), BF16),
                  jax.ShapeDtypeStruct((b, GQA_KV_HEADS, t, hd), BF16),
                  jax.ShapeDtypeStruct((b, GQA_KV_HEADS, V_ROWS, t), BF16),
                  jax.ShapeDtypeStruct((b, t, 2 * lru_w), F32),
                  jax.ShapeDtypeStruct((b, t, qw), BF16),
                  jax.ShapeDtypeStruct((b, MLA_HEADS, t, MLA_QK), BF16),
                  jax.ShapeDtypeStruct((b, MLA_HEADS, V_ROWS, t), BF16),
                  jax.ShapeDtypeStruct((b, t, 3 * d), BF16)]
    outs = pl.pallas_call(
        functools.partial(_in_proj_kernel, n_x=len(streams), has_prev=prev is not None, nc=nc, hd=hd,
                          lru_w=lru_w, q_lora=q_lora, kv_lora=kv_lora, d=d),
        grid=(t // tm, b),
        in_specs=in_specs, out_specs=out_specs, out_shape=out_shape,
        compiler_params=_cparams(("arbitrary", "arbitrary"), V7X_VMEM_LIMIT_BYTES),
        name="in_proj",
    )(*args)
    if prev is not None:
        return (outs[0],), outs[1:]
    return streams, outs


def _lru_kernel(z_ref, cw_ref, cb_ref, wg_ref, bg_ref, lam_ref, o_ref, a_f, b_f, a_b, b_b, *, n_ctx, chunk):
    t, width = a_f.shape
    n_chunks = t // chunk
    row = lax.broadcasted_iota(jnp.int32, (chunk, 1), 0)

    def conv_chunk(c):
        r0 = c * chunk
        seg_lo, seg_hi = (0, n_ctx) if r0 < n_ctx else (n_ctx, t)
        u = jnp.zeros((chunk, width), F32) + cb_ref[0]
        for j in range(CONV_WIDTH):
            off = j - 1
            lo = min(max(r0 + off, 0), t - chunk)
            tap = z_ref[0, lo:lo + chunk, 0:width]
            shift = (lo - (r0 + off)) % chunk
            if shift:
                tap = pltpu.roll(tap, shift, 0)
            if r0 + off < seg_lo or r0 + off + chunk > seg_hi:
                pos = row + (r0 + off)
                tap = jnp.where((pos >= seg_lo) & (pos < seg_hi), tap, 0.0)
            u = u + tap * cw_ref[0, j:j + 1, :]
        return u

    scr = ((a_f, b_f), (a_b, b_b))
    half_c_sp = []
    for dd in range(2):
        lam = lam_ref[0, dd:dd + 1, :]
        sp = jnp.maximum(-lam, 0.0) + jnp.log1p(jnp.exp(-jnp.abs(lam)))
        half_c_sp.append((-0.5 * LRU_C) * sp)
    for c in range(n_chunks):
        u = conv_chunk(c)
        g = jnp.dot(u.astype(BF16), wg_ref[0], preferred_element_type=F32) + bg_ref[0]
        half_u = 0.5 * u
        for dd in range(2):
            base = 2 * dd * width
            t_r = jnp.tanh(g[:, base:base + width])
            t_i = jnp.tanh(g[:, base + width:base + 2 * width])
            log_a = (1.0 + t_r) * half_c_sp[dd]
            th = jnp.tanh(log_a)
            v = (-2.0 * th) / (1.0 - th)
            root = jnp.where(v > 0.0, v * lax.rsqrt(v), 0.0)
            scr[dd][0][c * chunk:(c + 1) * chunk, :] = jnp.exp(log_a)
            scr[dd][1][c * chunk:(c + 1) * chunk, :] = root * ((1.0 + t_i) * half_u)

    ctx_blk = n_ctx // V7X_SUBLANES
    all_blk = t // V7X_SUBLANES
    nsub = V7X_SUBLANES

    def body(i, carry):
        hf, hb = carry
        rf = pl.multiple_of(i * nsub, nsub)
        jb = jnp.where(i < ctx_blk, ctx_blk - 1 - i, all_blk - 1 - (i - ctx_blk))
        rb = pl.multiple_of(jb * nsub, nsub)
        af, bf = a_f[pl.ds(rf, nsub), :], b_f[pl.ds(rf, nsub), :]
        ab, bb = a_b[pl.ds(rb, nsub), :], b_b[pl.ds(rb, nsub), :]
        rows_f, rows_b = [None] * nsub, [None] * nsub
        for s in range(nsub):
            hf = af[s:s + 1, :] * hf + bf[s:s + 1, :]
            rows_f[s] = hf
            sb = nsub - 1 - s
            hb = ab[sb:sb + 1, :] * hb + bb[sb:sb + 1, :]
            rows_b[sb] = hb
        a_f[pl.ds(rf, nsub), :] = jnp.concatenate(rows_f, axis=0)
        a_b[pl.ds(rb, nsub), :] = jnp.concatenate(rows_b, axis=0)
        return hf, hb

    h0 = jnp.zeros((1, width), F32)
    lax.fori_loop(0, all_blk, body, (h0, h0), unroll=2 if all_blk % 2 == 0 else 1)

    k0 = math.sqrt(2.0 / math.pi)
    for c in range(n_chunks):
        sl = slice(c * chunk, (c + 1) * chunk)
        y = z_ref[0, sl, width:2 * width]
        gelu = 0.5 * y * (1.0 + jnp.tanh(k0 * (y + 0.044715 * (y * y * y))))
        o_ref[0, sl, :] = ((a_f[sl, :] + a_b[sl, :]) * gelu).astype(BF16)


def _lru(zlru, layer, p, n_ctx):
    b, t, w2 = zlru.shape
    width = w2 // 2
    chunk = _tile(math.gcd(n_ctx, t), ROW_TILE)
    lay3 = lambda bb: (layer, 0, 0)
    return pl.pallas_call(
        functools.partial(_lru_kernel, n_ctx=n_ctx, chunk=chunk),
        grid=(b,),
        in_specs=[pl.BlockSpec((1, t, w2), lambda bb: (bb, 0, 0)),
                  pl.BlockSpec((1, CONV_WIDTH, width), lay3),
                  pl.BlockSpec((1, 1, width), lay3),
                  pl.BlockSpec((1,) + p["lru_wg"].shape[1:], lay3),
                  pl.BlockSpec((1, 1, 4 * width), lay3),
                  pl.BlockSpec((1, 2, width), lay3)],
        out_specs=pl.BlockSpec((1, t, width), lambda bb: (bb, 0, 0)),
        out_shape=jax.ShapeDtypeStruct((b, t, width), BF16),
        scratch_shapes=[pltpu.VMEM((t, width), F32)] * 4,
        compiler_params=_cparams(("arbitrary",), V7X_VMEM_LIMIT_BYTES),
        name="rg_lru",
    )(zlru, p["conv_w"], p["conv_b"], p["lru_wg"], p["lru_bg"], p["lru_lam"])


def _attn_kernel(q_ref, k_ref, vt_ref, o_ref, sbuf, p0, p1, *, gqa, dk, dv, n_ctx, n_lat, with_ctx, tq):
    t = n_ctx + n_lat
    pbuf = (p0, p1)
    out_off = 0 if with_ctx else n_ctx
    nt_dims = (((1,), (1,)), ((), ()))
    key_blk = math.gcd(n_ctx, ATTN_KEY_BLOCK)

    def n_sub(rows):
        return GQA_KV_HEADS * (rows // ATTN_SUB) if gqa else MLA_HEADS

    def load_q(r0, j, rows):
        if gqa:
            kvh, piece = j % GQA_KV_HEADS, j // GQA_KV_HEADS
            blk = q_ref[0, pl.ds(r0 + piece * ATTN_SUB, ATTN_SUB), kvh * GQA_GROUP * dk:(kvh + 1) * GQA_GROUP * dk]
            return jnp.concatenate([blk[:, h * dk:(h + 1) * dk] for h in range(GQA_GROUP)], axis=0), kvh
        return q_ref[0, pl.ds(r0, rows), j * dk:(j + 1) * dk], j

    def qk(r0, j, rows, n_keys, slot):
        q, kh = load_q(r0, j, rows)
        sbuf[slot, 0:n_keys, 0:q.shape[0]] = lax.dot_general(
            k_ref[0, kh, 0:n_keys, :], q, nt_dims, preferred_element_type=F32)

    def finish(r0, j, rows, n_keys, slot, par):
        cols = GQA_GROUP * ATTN_SUB if gqa else rows
        kh = j % GQA_KV_HEADS if gqa else j
        m = None
        for b0 in range(0, n_keys, key_blk):
            mb = jnp.max(sbuf[slot, b0:b0 + key_blk, 0:cols], axis=0, keepdims=True)
            m = mb if m is None else jnp.maximum(m, mb)
        for b0 in range(0, n_keys, key_blk):
            pbuf[par][b0:b0 + key_blk, 0:cols] = jnp.exp2(sbuf[slot, b0:b0 + key_blk, 0:cols] - m).astype(BF16)
        ot = jnp.dot(vt_ref[0, kh, :, 0:n_keys], pbuf[par][0:n_keys, 0:cols], preferred_element_type=F32)
        o = (ot[0:dv, :] / ot[dv:dv + 1, :]).T.astype(o_ref.dtype)
        if gqa:
            piece = j // GQA_KV_HEADS
            out = jnp.concatenate([o[h * ATTN_SUB:(h + 1) * ATTN_SUB, :] for h in range(GQA_GROUP)], axis=-1)
            o_ref[0, pl.ds(r0 - out_off + piece * ATTN_SUB, ATTN_SUB),
                  kh * GQA_GROUP * dv:(kh + 1) * GQA_GROUP * dv] = out
        else:
            o_ref[0, pl.ds(r0 - out_off, rows), j * dv:(j + 1) * dv] = o

    if with_ctx:
        rows_c = min(tq, n_ctx)
        for r in range(0, n_ctx, rows_c):
            for j in range(n_sub(rows_c)):
                qk(r, j, rows_c, n_ctx, 0)
                finish(r, j, rows_c, n_ctx, 0, 0)

    n_tiles = n_lat // tq
    ns = n_sub(tq)
    assert ns % ATTN_DEPTH == 0
    qk(n_ctx, 0, tq, t, 0)

    def body(i, carry):
        r0 = pl.multiple_of(n_ctx + i * tq, ATTN_SUB)
        r_next = pl.multiple_of(n_ctx + jnp.minimum(i + 1, n_tiles - 1) * tq, ATTN_SUB)
        for j in range(ns):
            if j + 1 < ns:
                qk(r0, j + 1, tq, t, (j + 1) % ATTN_DEPTH)
            else:
                qk(r_next, 0, tq, t, 0)
            finish(r0, j, tq, t, j % ATTN_DEPTH, j % 2)
        return carry

    lax.fori_loop(0, n_tiles, body, 0)


def _attention(q, k, vt, *, gqa, n_ctx, with_ctx, name):
    b, t, qw = q.shape
    hk, dk = k.shape[1], k.shape[-1]
    dv = MLA_V if not gqa else dk
    n_lat = t - n_ctx
    tq = _tile(n_lat, ATTN_Q_TILE)
    cols = GQA_GROUP * ATTN_SUB if gqa else tq
    t_out = t if with_ctx else n_lat
    w_out = (qw // dk) * dv
    kern = functools.partial(_attn_kernel, gqa=gqa, dk=dk, dv=dv, n_ctx=n_ctx, n_lat=n_lat,
                             with_ctx=with_ctx, tq=tq)
    return pl.pallas_call(
        kern,
        grid=(b,),
        in_specs=[pl.BlockSpec((1, t, qw), lambda bb: (bb, 0, 0)),
                  pl.BlockSpec((1, hk, t, dk), lambda bb: (bb, 0, 0, 0)),
                  pl.BlockSpec((1, hk, V_ROWS, t), lambda bb: (bb, 0, 0, 0))],
        out_specs=pl.BlockSpec((1, t_out, w_out), lambda bb: (bb, 0, 0)),
        out_shape=jax.ShapeDtypeStruct((b, t_out, w_out), BF16),
        scratch_shapes=[pltpu.VMEM((ATTN_DEPTH, t, cols), F32),
                        pltpu.VMEM((t, cols), BF16), pltpu.VMEM((t, cols), BF16)],
        compiler_params=_cparams(("arbitrary",), V7X_VMEM_LIMIT_BYTES),
        name=name,
    )(q, k, vt)


def _merge_kernel(*refs, n_x, nc, d, n_exp):
    x_refs, refs = refs[:n_x], refs[n_x:]
    (oa_ref, ol_ref, om_ref, gt_ref, m_ref, g_ref, wa_ref, wl_ref, wm_ref, wo_ref, rh_ref, rl_ref, rb_ref,
     xo_ref, h_ref, ids_ref, wts_ref) = refs
    tm = xo_ref.shape[1]
    parts = [slice(k * tm // MERGE_PARTS, (k + 1) * tm // MERGE_PARTS) for k in range(MERGE_PARTS)]
    x_in = _read_stream(x_refs, pl.program_id(1), nc)
    merged = []
    for r in parts:
        gates = gt_ref[0, r, :]
        merged.append(
            gates[:, 0:d].astype(F32) * jnp.dot(oa_ref[0, r, :], wa_ref[0], preferred_element_type=F32)
            + gates[:, d:2 * d].astype(F32) * jnp.dot(ol_ref[0, r, :], wl_ref[0], preferred_element_type=F32)
            + gates[:, 2 * d:3 * d].astype(F32) * jnp.dot(om_ref[0, r, :], wm_ref[0], preferred_element_type=F32))
    ys = [jnp.dot(mg.astype(BF16), wo_ref[0], preferred_element_type=F32) for mg in merged]
    his, los = [], []
    for r, y in zip(parts, ys):
        x = x_in[r, :] + m_ref[0, 0, 2:3, :] * y
        xo_ref[0, r, :] = x
        h = _rms(x) * g_ref[0]
        h = h * (1.0 + m_ref[0, 0, 4:5, :]) + m_ref[0, 0, 3:4, :]
        h_hi = h.astype(BF16)
        h_ref[0, r, :] = h_hi
        his.append(h_hi)
        los.append((h - h_hi.astype(F32)).astype(BF16))

    dn = (((1,), (1,)), ((), ()))
    logits = jnp.concatenate(
        [lax.dot_general(rh_ref[...], h_hi, dn, preferred_element_type=F32)
         + lax.dot_general(rh_ref[...], h_lo, dn, preferred_element_type=F32)
         + lax.dot_general(rl_ref[...], h_hi, dn, preferred_element_type=F32)
         for h_hi, h_lo in zip(his, los)], axis=-1)
    scores = _sigmoid(logits)
    sel = scores + rb_ref[...]
    per = n_exp // N_GROUPS
    gs = []
    for g in range(N_GROUPS):
        r = [sel[g * per + j:g * per + j + 1, :] for j in range(per)]
        best = None
        for a in range(per):
            for bq in range(a + 1, per):
                pair = r[a] + r[bq]
                best = pair if best is None else jnp.maximum(best, pair)
        gs.append(best)
    gmax = functools.reduce(jnp.maximum, gs)
    gbest = jnp.full(gmax.shape, N_GROUPS - 1, jnp.int32)
    for g in range(N_GROUPS - 2, -1, -1):
        gbest = jnp.where(gs[g] == gmax, g, gbest)
    eid = lax.broadcasted_iota(jnp.int32, sel.shape, 0)
    gid = jnp.zeros(sel.shape, jnp.int32)
    for g in range(1, N_GROUPS):
        gid = gid + (eid >= g * per).astype(jnp.int32)
    masked = jnp.where(gid == gbest, sel, -jnp.inf)
    m1 = jnp.max(masked, axis=0, keepdims=True)
    i1 = jnp.min(jnp.where(masked == m1, eid, n_exp), axis=0, keepdims=True)
    masked2 = jnp.where(eid == i1, -jnp.inf, masked)
    m2 = jnp.max(masked2, axis=0, keepdims=True)
    i2 = jnp.min(jnp.where(masked2 == m2, eid, n_exp), axis=0, keepdims=True)
    s1 = jnp.sum(jnp.where(eid == i1, scores, 0.0), axis=0, keepdims=True)
    s2 = jnp.sum(jnp.where(eid == i2, scores, 0.0), axis=0, keepdims=True)
    tot = s1 + s2
    ids_ref[0] = jnp.concatenate([i1, i2], axis=0)
    wts_ref[0] = jnp.concatenate([ROUTED_SCALE * s1 / tot, ROUTED_SCALE * s2 / tot], axis=0)


def _merge(streams, o_attn, o_lru, o_mla, gates, layer, p, n_ctx, with_ctx):
    b, d = streams[0].shape[0], streams[0].shape[-1]
    t = o_lru.shape[1]
    n_exp = p["r_hi"].shape[0]
    tm = _tile(math.gcd(n_ctx, t), ROW_TILE)
    off = 0 if with_ctx else n_ctx // tm
    nc = n_ctx // tm
    tq = t if with_ctx else t - n_ctx
    nt = tq // tm
    full = lambda bb, i: (bb, i + off, 0)
    qrow = lambda bb, i: (bb, i, 0)
    const = lambda bb, i: (0, 0)
    lay3 = lambda bb, i: (layer, 0, 0)
    bw = o_attn.shape[-1]
    route_spec = pl.BlockSpec((1, 2, tm), lambda bb, i: (bb * nt + i, 0, 0))
    wspec = lambda a: pl.BlockSpec((1,) + a.shape[1:], lay3)
    in_specs = _stream_specs(streams, tm, nc, lambda bb, i: i + off, lambda bb, i: bb)
    in_specs += [pl.BlockSpec((1, tm, bw), qrow),
                 pl.BlockSpec((1, tm, o_lru.shape[-1]), full),
                 pl.BlockSpec((1, tm, bw), qrow),
                 pl.BlockSpec((1, tm, 3 * d), full),
                 pl.BlockSpec((1, 1, N_MOD, d), lambda bb, i: (layer, jnp.where(i + off < nc, 0, bb + 1), 0, 0)),
                 pl.BlockSpec((1, 1, d), lay3),
                 wspec(p["w_ba"]), wspec(p["w_bl"]), wspec(p["w_bm"]), wspec(p["w_out"]),
                 pl.BlockSpec(p["r_hi"].shape, const), pl.BlockSpec(p["r_lo"].shape, const),
                 pl.BlockSpec(p["r_bias"].shape, const)]
    return pl.pallas_call(
        functools.partial(_merge_kernel, n_x=len(streams), nc=nc - off, d=d, n_exp=n_exp),
        grid=(b, nt),
        in_specs=in_specs,
        out_specs=[pl.BlockSpec((1, tm, d), qrow), pl.BlockSpec((1, tm, d), qrow),
                   route_spec, route_spec],
        out_shape=[jax.ShapeDtypeStruct((b, tq, d), F32), jax.ShapeDtypeStruct((b, tq, d), BF16),
                   jax.ShapeDtypeStruct((b * nt, 2, tm), jnp.int32),
                   jax.ShapeDtypeStruct((b * nt, 2, tm), F32)],
        compiler_params=_cparams(("arbitrary", "arbitrary"), V7X_VMEM_LIMIT_BYTES),
        name="merge_router",
    )(*streams, o_attn, o_lru, o_mla, gates, p["mods"], p["norm_ffn"], p["w_ba"], p["w_bl"], p["w_bm"],
      p["w_out"], p["r_hi"], p["r_lo"], p["r_bias"])


def _moe_kernel(h_ref, idr_ref, idc_ref, wtc_ref, low_ref, wg_ref, wu_ref, wd_ref, o_ref,
                xs, ys, cs, posc, meta, *, per, n_pairs):
    s = pl.program_id(1)
    tm = h_ref.shape[0]
    slots = xs.shape[0]
    pairs_per_group = per // 2

    @pl.when(s == 0)
    def _sort_tokens():
        e_row = idr_ref[0, 0:1, :]
        e_col = idc_ref[:, 0:1]
        g_row = jnp.zeros(e_row.shape, jnp.int32)
        g_col = jnp.zeros(e_col.shape, jnp.int32)
        for g in range(1, N_GROUPS):
            g_row = g_row + (e_row >= g * per).astype(jnp.int32)
            g_col = g_col + (e_col >= g * per).astype(jnp.int32)
        sub8 = lax.broadcasted_iota(jnp.int32, (V7X_SUBLANES, tm), 0)
        lane = lax.broadcasted_iota(jnp.int32, (tm, V7X_LANES), 1)
        onehot_row = jnp.where(sub8 == g_row, 1.0, 0.0)
        onehot_col = jnp.where(lane == g_col, 1.0, 0.0)
        rank_col = jnp.dot(low_ref[...], onehot_col.astype(BF16), preferred_element_type=F32)
        rank_row = lax.dot_general(onehot_row.astype(BF16), low_ref[...], (((1,), (1,)), ((), ())),
                                   preferred_element_type=F32)
        start = 0
        start_row = jnp.zeros((V7X_SUBLANES, 1), F32)
        start_col = jnp.zeros((1, V7X_LANES), F32)
        for g in range(N_GROUPS):
            cnt = jnp.sum((g_row == g).astype(jnp.int32))
            n_sub = lax.shift_right_logical(cnt + (V7X_LANES - 1), 7)
            meta[g] = start
            meta[N_GROUPS + g] = n_sub
            start_f = jnp.asarray(start, jnp.int32).astype(F32)
            start_row = jnp.where(lax.broadcasted_iota(jnp.int32, start_row.shape, 0) == g, start_f, start_row)
            start_col = jnp.where(lax.broadcasted_iota(jnp.int32, start_col.shape, 1) == g, start_f, start_col)
            start = start + n_sub * V7X_LANES
        pos_row = jnp.sum(onehot_row * (rank_row + start_row), axis=0, keepdims=True).astype(jnp.int32)
        pos_col = jnp.sum(onehot_col * (rank_col + start_col), axis=1, keepdims=True).astype(jnp.int32)
        posc[...] = pos_col
        slot = lax.broadcasted_iota(jnp.int32, (slots, tm), 0)
        perm = jnp.where(slot == pos_row, 1.0, 0.0).astype(BF16)
        xs[...] = jnp.dot(perm, h_ref[...], preferred_element_type=F32).astype(BF16)
        ids_c, wts_c = idc_ref[...], wtc_ref[...]
        comb = jnp.zeros((tm, V7X_LANES), F32)
        for k in range(2):
            local = jnp.bitwise_and(ids_c[:, k:k + 1], per - 1)
            comb = comb + jnp.where(lane == local, wts_c[:, k:k + 1], 0.0)
        c_hi = comb.astype(BF16)
        c_lo = (comb - c_hi.astype(F32)).astype(BF16)
        both = jnp.dot(perm, jnp.concatenate([c_hi, c_lo], axis=-1), preferred_element_type=F32)
        cs[...] = both[:, :V7X_LANES] + both[:, V7X_LANES:]
        ys[...] = jnp.zeros_like(ys)

    grp = s // pairs_per_group
    first_local = (s % pairs_per_group) * 2
    seg_start = meta[grp]
    lane_sub = lax.broadcasted_iota(jnp.int32, (V7X_LANES, V7X_LANES), 1)

    def sub_tile(k, carry):
        r0 = pl.multiple_of(seg_start + k * V7X_LANES, V7X_LANES)
        x = xs[pl.ds(r0, V7X_LANES), :]
        c_all = cs[pl.ds(r0, V7X_LANES), :]
        acc = None
        for jj in range(2):
            c = jnp.sum(jnp.where(lane_sub == first_local + jj, c_all, 0.0), axis=-1, keepdims=True)
            g = jnp.dot(x, wg_ref[0, jj], preferred_element_type=F32)
            u = jnp.dot(x, wu_ref[0, jj], preferred_element_type=F32)
            act = ((g * _sigmoid(g)) * u * c).astype(BF16)
            part = jnp.dot(act, wd_ref[0, jj], preferred_element_type=F32)
            acc = part if acc is None else acc + part
        ys[pl.ds(r0, V7X_LANES), :] = ys[pl.ds(r0, V7X_LANES), :] + acc
        return carry

    lax.fori_loop(0, meta[N_GROUPS + grp], sub_tile, 0)

    @pl.when(s == n_pairs - 1)
    def _unsort():
        slot_l = lax.broadcasted_iota(jnp.int32, (tm, slots), 1)
        perm_t = jnp.where(slot_l == posc[...], 1.0, 0.0).astype(BF16)
        y = ys[...]
        y_hi = y.astype(BF16)
        y_lo = (y - y_hi.astype(F32)).astype(BF16)
        o_ref[...] = (jnp.dot(perm_t, y_hi, preferred_element_type=F32)
                      + jnp.dot(perm_t, y_lo, preferred_element_type=F32))


def _moe(h, ids, wts, layer, p):
    n, d = h.shape
    _, n_exp, _, ff = p["moe_wg"].shape
    per = n_exp // N_GROUPS
    assert per % 2 == 0 and per & (per - 1) == 0
    rt = ids.shape[-1]
    tm = _tile(n, MOE_ROW_TILE)
    assert tm % rt == 0 and tm % V7X_LANES == 0
    slots = tm + N_GROUPS * V7X_LANES
    ids_row = ids.reshape(n // tm, tm // rt, 2, rt).transpose(0, 2, 1, 3).reshape(n // tm, 2, tm)
    ids_col = jnp.transpose(ids, (0, 2, 1)).reshape(n, 2)
    wts_col = jnp.transpose(wts, (0, 2, 1)).reshape(n, 2)
    low = jnp.tril(jnp.ones((tm, tm), BF16), -1)
    n_pairs = n_exp // 2
    wspec = lambda a, b: pl.BlockSpec((1, 2, a, b), lambda i, s: (layer, s, 0, 0))
    return pl.pallas_call(
        functools.partial(_moe_kernel, per=per, n_pairs=n_pairs),
        grid=(n // tm, n_pairs),
        in_specs=[pl.BlockSpec((tm, d), lambda i, s: (i, 0)),
                  pl.BlockSpec((1, 2, tm), lambda i, s: (i, 0, 0)),
                  pl.BlockSpec((tm, 2), lambda i, s: (i, 0)),
                  pl.BlockSpec((tm, 2), lambda i, s: (i, 0)),
                  pl.BlockSpec((tm, tm), lambda i, s: (0, 0), pipeline_mode=pl.Buffered(1)),
                  wspec(d, ff), wspec(d, ff), wspec(ff, d)],
        out_specs=pl.BlockSpec((tm, d), lambda i, s: (i, 0)),
        out_shape=jax.ShapeDtypeStruct((n, d), F32),
        scratch_shapes=[pltpu.VMEM((slots, d), BF16), pltpu.VMEM((slots, d), F32),
                        pltpu.VMEM((slots, V7X_LANES), F32), pltpu.VMEM((tm, 1), jnp.int32),
                        pltpu.SMEM((2 * N_GROUPS,), jnp.int32)],
        compiler_params=_cparams(("arbitrary", "arbitrary"), V7X_VMEM_LIMIT_BYTES),
        name="moe_ffn",
    )(h, ids_row, ids_col, wts_col, low, p["moe_wg"], p["moe_wu"], p["moe_wd"])


def _final_kernel(x_ref, f_ref, m_ref, g_ref, o_ref):
    x = x_ref[0] + m_ref[0, 0, 5:6, :] * f_ref[0]
    o_ref[0] = _rms(x) * g_ref[...]


def _final(x, f, mods, layer, g):
    b, s, d = x.shape
    tm = _tile(s, 2 * ROW_TILE)
    row = lambda bb, i: (bb, i, 0)
    return pl.pallas_call(
        _final_kernel,
        grid=(b, s // tm),
        in_specs=[pl.BlockSpec((1, tm, d), row), pl.BlockSpec((1, tm, d), row),
                  pl.BlockSpec((1, 1, N_MOD, d), lambda bb, i: (layer, bb + 1, 0, 0)),
                  pl.BlockSpec((1, d), lambda bb, i: (0, 0))],
        out_specs=pl.BlockSpec((1, tm, d), row),
        out_shape=jax.ShapeDtypeStruct((b, s, d), F32),
        compiler_params=_cparams(("arbitrary", "arbitrary")),
        name="final_norm",
    )(x, f, mods, g.reshape(1, d))


def _rope_tables(n_ctx, n_lat, dim, period, lane_off, width):
    quarter = dim // 4
    pos = jnp.arange(n_lat, dtype=F32)
    r, col = jnp.floor(pos / GRID_W), pos - GRID_W * jnp.floor(pos / GRID_W)
    inv_freq = ROPE_THETA ** (-jnp.arange(quarter, dtype=F32) / quarter)
    lane = np.arange(width)
    j = (lane % period) - lane_off
    active = (j >= 0) & (j < dim)
    jj = np.where(active, j, 0)
    use_col = jj >= dim // 2
    upper = (jj % (dim // 2)) >= quarter
    f = jj % quarter
    ang = jnp.where(use_col[None, :], col[:, None], r[:, None]) * inv_freq[f][None, :]
    act = jnp.asarray(active)[None, :]
    cos = jnp.where(act, jnp.cos(ang), 1.0)
    sin = jnp.where(act, jnp.sin(ang), 0.0)
    sin_up = jnp.where(jnp.asarray(~upper)[None, :], -sin, 0.0)
    sin_dn = jnp.where(jnp.asarray(upper)[None, :], sin, 0.0)
    ident = lambda v, fill: jnp.concatenate([jnp.full((n_ctx, width), fill, F32), v], axis=0)
    return ident(cos, 1.0), ident(sin_up, 0.0), ident(sin_dn, 0.0)


def _lru_gate_weights(w_a, w_i):
    depth, _, nb, k, j = w_a.shape
    w = jnp.stack([w_a, w_i], axis=2)
    eye = jnp.eye(nb, dtype=w.dtype)
    dense = jnp.einsum('dsgnkj,nm->dnksgmj', w, eye)
    return (0.5 * dense).reshape(depth, nb * k, 4 * nb * j).astype(BF16)


def kernel(x, c, ctx, c_ctx, w_mod, b_mod, norm_mix, norm_ffn, w_in, gqa_q_norm, gqa_k_norm, conv_w, conv_b,
           lru_w_a, lru_b_a, lru_w_i, lru_b_i, lru_lam, mla_q_a_norm, mla_w_qb, mla_kv_a_norm, mla_w_kvb,
           w_branch_attn, w_branch_lru, w_branch_mla, w_out, router_w, router_bias,
           moe_w_gate, moe_w_up, moe_w_down, final_norm):
    bsz, n_lat, d = x.shape
    n_ctx = ctx.shape[1]
    depth = w_mod.shape[0]
    hd = gqa_q_norm.shape[-1]
    lru_w = conv_w.shape[-1]
    q_lora, kv_lora = mla_q_a_norm.shape[-1], mla_kv_a_norm.shape[-1]
    n_exp = router_w.shape[-1]
    assert bsz + 1 <= MOD_ROWS and d % V7X_LANES == 0

    cc = jnp.zeros((MOD_ROWS, d), F32).at[0].set(c_ctx).at[1:1 + bsz].set(c)
    mods = _modulation(cc, w_mod, b_mod).reshape(depth, MOD_ROWS, N_MOD, d)

    mq_period = MLA_QK * V7X_LANES // math.gcd(MLA_QK, V7X_LANES)
    kr_end = GQA_HEADS * hd + 2 * GQA_KV_HEADS * hd + 2 * lru_w + q_lora + kv_lora + MLA_ROPE
    w_in_b = w_in.astype(BF16)
    w_in_b = jnp.concatenate([w_in_b[..., :kr_end], jnp.zeros((depth, d, V7X_LANES - MLA_ROPE), BF16),
                              w_in_b[..., kr_end:]], axis=-1)
    r_t = router_w.T
    r_hi = r_t.astype(BF16)
    p = dict(
        hd=hd, lru_w=lru_w, q_lora=q_lora, kv_lora=kv_lora, mods=mods,
        norm_mix=norm_mix.reshape(depth, 1, d), norm_ffn=norm_ffn.reshape(depth, 1, d), w_in=w_in_b,
        gq=jnp.tile(gqa_q_norm, (1, GQA_HEADS)).reshape(depth, 1, -1),
        gk=jnp.tile(gqa_k_norm, (1, GQA_KV_HEADS)).reshape(depth, 1, -1),
        gqa_tabs=_rope_tables(n_ctx, n_lat, hd, hd, 0, V7X_LANES),
        mla_q_tabs=_rope_tables(n_ctx, n_lat, MLA_ROPE, MLA_QK, MLA_NOPE, mq_period),
        mla_k_tabs=_rope_tables(n_ctx, n_lat, MLA_ROPE, V7X_LANES, 0, V7X_LANES),
        mla_q_norm=mla_q_a_norm.reshape(depth, 1, q_lora), mla_kv_norm=mla_kv_a_norm.reshape(depth, 1, kv_lora),
        w_qb=mla_w_qb.astype(BF16), w_kvb=mla_w_kvb.astype(BF16),
        conv_w=conv_w, conv_b=conv_b.reshape(depth, 1, lru_w),
        lru_wg=_lru_gate_weights(lru_w_a, lru_w_i),
        lru_bg=0.5 * jnp.stack([lru_b_a, lru_b_i], axis=2).reshape(depth, 1, 4 * lru_w),
        lru_lam=lru_lam,
        w_ba=w_branch_attn.astype(BF16), w_bl=w_branch_lru.astype(BF16), w_bm=w_branch_mla.astype(BF16),
        w_out=w_out.astype(BF16), r_hi=r_hi, r_lo=(r_t - r_hi.astype(F32)).astype(BF16),
        r_bias=router_bias.reshape(n_exp, 1).astype(F32),
        moe_wg=moe_w_gate.astype(BF16), moe_wu=moe_w_up.astype(BF16), moe_wd=moe_w_down.astype(BF16),
    )

    streams = (ctx, x)
    prev = None
    for layer in range(depth):
        last = layer == depth - 1
        streams, (q_g, k_g, v_g, zlru, q_m, k_m, v_m, gates) = _in_proj(streams, prev, layer, p, n_ctx)
        o_attn = _attention(q_g, k_g, v_g, gqa=True, n_ctx=n_ctx, with_ctx=not last, name="gqa_attention")
        o_lru = _lru(zlru, layer, p, n_ctx)
        o_mla = _attention(q_m, k_m, v_m, gqa=False, n_ctx=n_ctx, with_ctx=not last, name="mla_attention")
        x_mid, h2, ids, wts = _merge(streams, o_attn, o_lru, o_mla, gates, layer, p, n_ctx, with_ctx=not last)

        n_tok = x_mid.shape[0] * x_mid.shape[1]
        f = _moe(h2.reshape(n_tok, d), ids, wts, layer, p).reshape(x_mid.shape)
        streams, prev = (x_mid,), (f, layer)

    return _final(streams[0], prev[0], mods, prev[1], final_norm)
```

```python
import functools
import math

import numpy as np
import jax
import jax.numpy as jnp
from jax import lax
from jax.experimental import pallas as pl
from jax.experimental.pallas import tpu as pltpu

F32 = jnp.float32
BF16 = jnp.bfloat16

GRID_W = 64
ROPE_THETA = 10000.0
NORM_EPS = 1e-6
N_MOD = 6
GQA_HEADS = 8
GQA_KV_HEADS = 2
GQA_GROUP = GQA_HEADS // GQA_KV_HEADS
MLA_HEADS = 8
MLA_NOPE = 64
MLA_ROPE = 32
MLA_V = 64
MLA_QK = MLA_NOPE + MLA_ROPE
CONV_WIDTH = 4
LRU_C = 8.0
N_GROUPS = 4
ROUTED_SCALE = 1.0
LOG2E = math.log2(math.e)

V7X_LANES = 128
V7X_SUBLANES = 8
V7X_VMEM_LIMIT_BYTES = 56 * 1024 * 1024

ROW_TILE = 256
ATTN_Q_TILE = 512
ATTN_SUB = 128
ATTN_DEPTH = 2
ATTN_KEY_BLOCK = 256
V_ROWS = 80
MERGE_PARTS = 2
MOE_ROW_TILE = 1024
MOD_ROWS = 16


def _cparams(sem, vmem=None):
    return pltpu.CompilerParams(dimension_semantics=sem, vmem_limit_bytes=vmem)


def _tile(n, pref):
    t = min(n, pref)
    while n % t or t % V7X_SUBLANES:
        t -= 1
    return t


def _sigmoid(x):
    return 0.5 * (1.0 + jnp.tanh(0.5 * x))


def _rms(x):
    return x * lax.rsqrt(jnp.mean(x * x, axis=-1, keepdims=True) + NORM_EPS)


def _tiled(tbl, width):
    reps = width // tbl.shape[-1]
    return tbl if reps == 1 else jnp.concatenate([tbl] * reps, axis=-1)


def _rope_lanes(x, cos, sin_up, sin_dn, half):
    outs = []
    for c in range(x.shape[-1] // V7X_LANES):
        sl = slice(c * V7X_LANES, (c + 1) * V7X_LANES)
        xc = x[:, sl]
        up = pltpu.roll(xc, V7X_LANES - half, 1)
        dn = pltpu.roll(xc, half, 1)
        outs.append(xc * cos[:, sl] + up * sin_up[:, sl] + dn * sin_dn[:, sl])
    return outs[0] if len(outs) == 1 else jnp.concatenate(outs, axis=-1)


def _ones_row_tail(rows, cols):
    r = lax.broadcasted_iota(jnp.int32, (rows, cols), 0)
    return jnp.where(r == 0, 1.0, 0.0).astype(BF16)


def _stream_specs(streams, tm, nc, tile_of, batch_of):
    d = streams[0].shape[-1]
    if len(streams) == 1:
        return [pl.BlockSpec((1, tm, d), lambda *g: (batch_of(*g), tile_of(*g), 0))]
    return [pl.BlockSpec((1, tm, d), lambda *g: (batch_of(*g), jnp.minimum(tile_of(*g), nc - 1), 0)),
            pl.BlockSpec((1, tm, d), lambda *g: (batch_of(*g), jnp.maximum(tile_of(*g) - nc, 0), 0))]


def _read_stream(refs, tile, nc):
    if len(refs) == 1:
        return refs[0][0]
    return jnp.where(tile < nc, refs[0][0], refs[1][0])


def _mod_kernel(c_ref, w_ref, b_ref, o_ref):
    c = c_ref[...]
    a = (c * _sigmoid(c)).astype(BF16)
    o_ref[0] = jnp.dot(a, w_ref[0].astype(BF16), preferred_element_type=F32) + b_ref[0]


def _modulation(cc, w_mod, b_mod):
    depth, d, n = w_mod.shape
    tn = _tile(n, 1536) if n % V7X_LANES == 0 else n
    return pl.pallas_call(
        _mod_kernel,
        grid=(depth, n // tn),
        in_specs=[pl.BlockSpec((MOD_ROWS, d), lambda l, j: (0, 0)),
                  pl.BlockSpec((1, d, tn), lambda l, j: (l, 0, j)),
                  pl.BlockSpec((1, 1, tn), lambda l, j: (l, 0, j))],
        out_specs=pl.BlockSpec((1, MOD_ROWS, tn), lambda l, j: (l, 0, j)),
        out_shape=jax.ShapeDtypeStruct((depth, MOD_ROWS, n), F32),
        compiler_params=_cparams(("arbitrary", "arbitrary"), V7X_VMEM_LIMIT_BYTES),
        name="modulation",
    )(cc, w_mod, b_mod.reshape(depth, 1, n))


def _pack_w_in_kernel(w_ref, o_ref, *, kr_end):
    w = w_ref[0].astype(BF16)
    pad = jnp.zeros((w.shape[0], o_ref.shape[-1] - w.shape[-1]), BF16)
    o_ref[0] = jnp.concatenate([w[:, :kr_end], pad, w[:, kr_end:]], axis=-1)


def _pack_w_in(w_in, kr_end):
    depth, d, n = w_in.shape
    n_out = n + V7X_LANES - MLA_ROPE
    rt = _tile(d, V7X_LANES)
    return pl.pallas_call(
        functools.partial(_pack_w_in_kernel, kr_end=kr_end),
        grid=(depth, d // rt),
        in_specs=[pl.BlockSpec((1, rt, n), lambda l, i: (l, i, 0))],
        out_specs=pl.BlockSpec((1, rt, n_out), lambda l, i: (l, i, 0)),
        out_shape=jax.ShapeDtypeStruct((depth, d, n_out), BF16),
        compiler_params=_cparams(("arbitrary", "arbitrary")),
        name="pack_w_in",
    )(w_in)


def _in_proj_kernel(*refs, n_x, has_prev, nc, hd, lru_w, q_lora, kv_lora, d):
    tile = pl.program_id(0)
    x_refs, refs = refs[:n_x], refs[n_x:]
    if has_prev:
        f_ref, pm_ref, refs = refs[0], refs[1], refs[2:]
    (g_ref, m_ref, w_ref, gq_ref, gk_ref, gc_ref, gu_ref, gd_ref, nqa_ref, nkv_ref, wq_ref, wkv_ref,
     qc_ref, qu_ref, qd_ref, kc_ref, ku_ref, kd_ref) = refs[:18]
    outs = refs[18:]
    x = _read_stream(x_refs, tile, nc)
    if has_prev:
        x = x + pm_ref[0, 0, 5:6, :] * f_ref[0]
        outs[0][0] = x
        outs = outs[1:]
    qg_ref, kg_ref, vg_ref, lru_ref, qm_ref, km_ref, vm_ref, gate_ref = outs
    tm = x.shape[0]
    h = _rms(x) * g_ref[0]
    h = h * (1.0 + m_ref[0, 0, 1:2, :]) + m_ref[0, 0, 0:1, :]
    hb = h.astype(BF16)

    nq, nk = GQA_HEADS * hd, GQA_KV_HEADS * hd
    w_gqa = nq + 2 * nk
    w_mla = q_lora + kv_lora + V7X_LANES
    o_lru, o_mla, o_gate = w_gqa, w_gqa + 2 * lru_w, w_gqa + 2 * lru_w + w_mla

    z = jnp.dot(hb, w_ref[0, :, 0:w_gqa], preferred_element_type=F32)

    def head_norm(a, n_heads):
        return jnp.concatenate([_rms(a[:, i * hd:(i + 1) * hd]) for i in range(n_heads)], axis=-1)

    cos, su, sd = gc_ref[...], gu_ref[...], gd_ref[...]
    q = head_norm(z[:, :nq], GQA_HEADS) * gq_ref[0]
    q = _rope_lanes(q, _tiled(cos, nq), _tiled(su, nq), _tiled(sd, nq), hd // 4)
    qg_ref[0] = (q * (LOG2E * hd ** -0.5)).astype(BF16)
    k = head_norm(z[:, nq:nq + nk], GQA_KV_HEADS) * gk_ref[0]
    k = _rope_lanes(k, _tiled(cos, nk), _tiled(su, nk), _tiled(sd, nk), hd // 4)
    v_t = z[:, nq + nk:nq + 2 * nk].T
    tail = _ones_row_tail(V_ROWS - hd, tm)
    for i in range(GQA_KV_HEADS):
        kg_ref[0, i] = k[:, i * hd:(i + 1) * hd].astype(BF16)
        vg_ref[0, i, 0:hd, :] = v_t[i * hd:(i + 1) * hd, :].astype(BF16)
        vg_ref[0, i, hd:V_ROWS, :] = tail

    lru_ref[0] = jnp.dot(hb, w_ref[0, :, o_lru:o_lru + 2 * lru_w], preferred_element_type=F32)

    z = jnp.dot(hb, w_ref[0, :, o_mla:o_mla + w_mla], preferred_element_type=F32)
    cq = (_rms(z[:, :q_lora]) * nqa_ref[0]).astype(BF16)
    qm = jnp.dot(cq, wq_ref[0], preferred_element_type=F32)
    wq = qm.shape[-1]
    qm = _rope_lanes(qm, _tiled(qc_ref[...], wq), _tiled(qu_ref[...], wq), _tiled(qd_ref[...], wq),
                     MLA_ROPE // 4)
    qm_ref[0] = (qm * (LOG2E * MLA_QK ** -0.5)).astype(BF16)
    ckv = (_rms(z[:, q_lora:q_lora + kv_lora]) * nkv_ref[0]).astype(BF16)
    kv = jnp.dot(ckv, wkv_ref[0], preferred_element_type=F32)
    kr = _rope_lanes(z[:, q_lora + kv_lora:], kc_ref[...], ku_ref[...], kd_ref[...], MLA_ROPE // 4)
    kr = kr[:, :MLA_ROPE]
    per = MLA_NOPE + MLA_V
    tail = _ones_row_tail(V_ROWS - MLA_V, tm)
    for i in range(MLA_HEADS):
        km_ref[0, i] = jnp.concatenate([kv[:, i * per:i * per + MLA_NOPE], kr], axis=-1).astype(BF16)
        head_t = kv[:, i * per:(i + 1) * per].T
        vm_ref[0, i, 0:MLA_V, :] = head_t[MLA_NOPE:per, :].astype(BF16)
        vm_ref[0, i, MLA_V:V_ROWS, :] = tail

    gate_ref[0] = _sigmoid(jnp.dot(hb, w_ref[0, :, o_gate:o_gate + 3 * d],
                                   preferred_element_type=F32)).astype(BF16)


def _in_proj(streams, prev, layer, p, n_ctx):
    b, d = streams[0].shape[0], streams[0].shape[-1]
    t = sum(s.shape[1] for s in streams) if len(streams) == 2 else streams[0].shape[1]
    tm = _tile(math.gcd(n_ctx, t), ROW_TILE)
    nc = n_ctx // tm
    hd, lru_w, q_lora, kv_lora = p["hd"], p["lru_w"], p["q_lora"], p["kv_lora"]
    nq, nk = GQA_HEADS * hd, GQA_KV_HEADS * hd
    qw = MLA_HEADS * MLA_QK
    tile_of, batch_of = (lambda i, bb: i), (lambda i, bb: bb)
    row = lambda i, bb: (bb, i, 0)
    mod_row = lambda i, bb: jnp.where(i < nc, 0, bb + 1)
    lay3 = lambda i, bb: (layer, 0, 0)
    tbl = lambda a: pl.BlockSpec((tm, a.shape[-1]), lambda i, bb: (i, 0))
    x_spec = pl.BlockSpec((1, tm, d), row)

    in_specs = _stream_specs(streams, tm, nc, tile_of, batch_of)
    args = list(streams)
    out_specs, out_shape = [], []
    if prev is not None:
        in_specs += [x_spec, pl.BlockSpec((1, 1, N_MOD, d), lambda i, bb: (prev[1], mod_row(i, bb), 0, 0))]
        args += [prev[0], p["mods"]]
        out_specs.append(x_spec)
        out_shape.append(jax.ShapeDtypeStruct((b, t, d), F32))
    in_specs += [pl.BlockSpec((1, 1, d), lay3),
                 pl.BlockSpec((1, 1, N_MOD, d), lambda i, bb: (layer, mod_row(i, bb), 0, 0)),
                 pl.BlockSpec((1,) + p["w_in"].shape[1:], lay3, pipeline_mode=pl.Buffered(1)),
                 pl.BlockSpec((1, 1, nq), lay3), pl.BlockSpec((1, 1, nk), lay3)]
    args += [p["norm_mix"], p["mods"], p["w_in"], p["gq"], p["gk"]]
    in_specs += [tbl(a) for a in p["gqa_tabs"]]
    args += list(p["gqa_tabs"])
    in_specs += [pl.BlockSpec((1, 1, q_lora), lay3), pl.BlockSpec((1, 1, kv_lora), lay3),
                 pl.BlockSpec((1,) + p["w_qb"].shape[1:], lay3),
                 pl.BlockSpec((1,) + p["w_kvb"].shape[1:], lay3)]
    args += [p["mla_q_norm"], p["mla_kv_norm"], p["w_qb"], p["w_kvb"]]
    in_specs += [tbl(a) for a in p["mla_q_tabs"]] + [tbl(a) for a in p["mla_k_tabs"]]
    args += list(p["mla_q_tabs"]) + list(p["mla_k_tabs"])

    def kv_specs(heads, dk):
        return [pl.BlockSpec((1, heads, tm, dk), lambda i, bb: (bb, 0, i, 0)),
                pl.BlockSpec((1, heads, V_ROWS, tm), lambda i, bb: (bb, 0, 0, i))]

    out_specs += ([pl.BlockSpec((1, tm, nq), row)] + kv_specs(GQA_KV_HEADS, hd)
                  + [pl.BlockSpec((1, tm, 2 * lru_w), row), pl.BlockSpec((1, tm, qw), row)]
                  + kv_specs(MLA_HEADS, MLA_QK) + [pl.BlockSpec((1, tm, 3 * d), row)])
    out_shape += [jax.ShapeDtypeStruct((b, t, nq), BF16),
                  jax.ShapeDtypeStruct((b, GQA_KV_HEADS, t, hd), BF16),
                  jax.ShapeDtypeStruct((b, GQA_KV_HEADS, V_ROWS, t), BF16),
                  jax.ShapeDtypeStruct((b, t, 2 * lru_w), F32),
                  jax.ShapeDtypeStruct((b, t, qw), BF16),
                  jax.ShapeDtypeStruct((b, MLA_HEADS, t, MLA_QK), BF16),
                  jax.ShapeDtypeStruct((b, MLA_HEADS, V_ROWS, t), BF16),
                  jax.ShapeDtypeStruct((b, t, 3 * d), BF16)]
    outs = pl.pallas_call(
        functools.partial(_in_proj_kernel, n_x=len(streams), has_prev=prev is not None, nc=nc, hd=hd,
                          lru_w=lru_w, q_lora=q_lora, kv_lora=kv_lora, d=d),
        grid=(t // tm, b),
        in_specs=in_specs, out_specs=out_specs, out_shape=out_shape,
        compiler_params=_cparams(("arbitrary", "arbitrary"), V7X_VMEM_LIMIT_BYTES),
        name="in_proj",
    )(*args)
    if prev is not None:
        return (outs[0],), outs[1:]
    return streams, outs


def _lru_kernel(z_ref, cw_ref, cb_ref, wg_ref, bg_ref, lam_ref, o_ref, a_f, b_f, a_b, b_b, *, n_ctx, chunk):
    t, width = a_f.shape
    n_chunks = t // chunk
    row = lax.broadcasted_iota(jnp.int32, (chunk, 1), 0)

    def conv_chunk(c):
        r0 = c * chunk
        seg_lo, seg_hi = (0, n_ctx) if r0 < n_ctx else (n_ctx, t)
        u = jnp.zeros((chunk, width), F32) + cb_ref[0]
        for j in range(CONV_WIDTH):
            off = j - 1
            lo = min(max(r0 + off, 0), t - chunk)
            tap = z_ref[0, lo:lo + chunk, 0:width]
            shift = (lo - (r0 + off)) % chunk
            if shift:
                tap = pltpu.roll(tap, shift, 0)
            if r0 + off < seg_lo or r0 + off + chunk > seg_hi:
                pos = row + (r0 + off)
                tap = jnp.where((pos >= seg_lo) & (pos < seg_hi), tap, 0.0)
            u = u + tap * cw_ref[0, j:j + 1, :]
        return u

    scr = ((a_f, b_f), (a_b, b_b))
    half_c_sp = []
    for dd in range(2):
        lam = lam_ref[0, dd:dd + 1, :]
        sp = jnp.maximum(-lam, 0.0) + jnp.log1p(jnp.exp(-jnp.abs(lam)))
        half_c_sp.append((-0.5 * LRU_C) * sp)
    for c in range(n_chunks):
        u = conv_chunk(c)
        g = jnp.dot(u.astype(BF16), wg_ref[0], preferred_element_type=F32) + bg_ref[0]
        half_u = 0.5 * u
        for dd in range(2):
            base = 2 * dd * width
            t_r = jnp.tanh(g[:, base:base + width])
            t_i = jnp.tanh(g[:, base + width:base + 2 * width])
            log_a = (1.0 + t_r) * half_c_sp[dd]
            th = jnp.tanh(log_a)
            v = (-2.0 * th) / (1.0 - th)
            root = jnp.where(v > 0.0, v * lax.rsqrt(v), 0.0)
            scr[dd][0][c * chunk:(c + 1) * chunk, :] = jnp.exp(log_a)
            scr[dd][1][c * chunk:(c + 1) * chunk, :] = root * ((1.0 + t_i) * half_u)

    ctx_blk = n_ctx // V7X_SUBLANES
    all_blk = t // V7X_SUBLANES
    nsub = V7X_SUBLANES

    def body(i, carry):
        hf, hb = carry
        rf = pl.multiple_of(i * nsub, nsub)
        jb = jnp.where(i < ctx_blk, ctx_blk - 1 - i, all_blk - 1 - (i - ctx_blk))
        rb = pl.multiple_of(jb * nsub, nsub)
        af, bf = a_f[pl.ds(rf, nsub), :], b_f[pl.ds(rf, nsub), :]
        ab, bb = a_b[pl.ds(rb, nsub), :], b_b[pl.ds(rb, nsub), :]
        rows_f, rows_b = [None] * nsub, [None] * nsub
        for s in range(nsub):
            hf = af[s:s + 1, :] * hf + bf[s:s + 1, :]
            rows_f[s] = hf
            sb = nsub - 1 - s
            hb = ab[sb:sb + 1, :] * hb + bb[sb:sb + 1, :]
            rows_b[sb] = hb
        a_f[pl.ds(rf, nsub), :] = jnp.concatenate(rows_f, axis=0)
        a_b[pl.ds(rb, nsub), :] = jnp.concatenate(rows_b, axis=0)
        return hf, hb

    h0 = jnp.zeros((1, width), F32)
    lax.fori_loop(0, all_blk, body, (h0, h0), unroll=2 if all_blk % 2 == 0 else 1)

    k0 = math.sqrt(2.0 / math.pi)
    for c in range(n_chunks):
        sl = slice(c * chunk, (c + 1) * chunk)
        y = z_ref[0, sl, width:2 * width]
        gelu = 0.5 * y * (1.0 + jnp.tanh(k0 * (y + 0.044715 * (y * y * y))))
        o_ref[0, sl, :] = ((a_f[sl, :] + a_b[sl, :]) * gelu).astype(BF16)


def _lru(zlru, layer, p, n_ctx):
    b, t, w2 = zlru.shape
    width = w2 // 2
    chunk = _tile(math.gcd(n_ctx, t), ROW_TILE)
    lay3 = lambda bb: (layer, 0, 0)
    return pl.pallas_call(
        functools.partial(_lru_kernel, n_ctx=n_ctx, chunk=chunk),
        grid=(b,),
        in_specs=[pl.BlockSpec((1, t, w2), lambda bb: (bb, 0, 0)),
                  pl.BlockSpec((1, CONV_WIDTH, width), lay3),
                  pl.BlockSpec((1, 1, width), lay3),
                  pl.BlockSpec((1,) + p["lru_wg"].shape[1:], lay3),
                  pl.BlockSpec((1, 1, 4 * width), lay3),
                  pl.BlockSpec((1, 2, width), lay3)],
        out_specs=pl.BlockSpec((1, t, width), lambda bb: (bb, 0, 0)),
        out_shape=jax.ShapeDtypeStruct((b, t, width), BF16),
        scratch_shapes=[pltpu.VMEM((t, width), F32)] * 4,
        compiler_params=_cparams(("arbitrary",), V7X_VMEM_LIMIT_BYTES),
        name="rg_lru",
    )(zlru, p["conv_w"], p["conv_b"], p["lru_wg"], p["lru_bg"], p["lru_lam"])


def _attn_kernel(q_ref, k_ref, vt_ref, o_ref, sbuf, p0, p1, *, gqa, dk, dv, n_ctx, n_lat, with_ctx, tq):
    t = n_ctx + n_lat
    pbuf = (p0, p1)
    out_off = 0 if with_ctx else n_ctx
    nt_dims = (((1,), (1,)), ((), ()))
    key_blk = math.gcd(n_ctx, ATTN_KEY_BLOCK)

    def n_sub(rows):
        return GQA_KV_HEADS * (rows // ATTN_SUB) if gqa else MLA_HEADS

    def load_q(r0, j, rows):
        if gqa:
            kvh, piece = j % GQA_KV_HEADS, j // GQA_KV_HEADS
            blk = q_ref[0, pl.ds(r0 + piece * ATTN_SUB, ATTN_SUB), kvh * GQA_GROUP * dk:(kvh + 1) * GQA_GROUP * dk]
            return jnp.concatenate([blk[:, h * dk:(h + 1) * dk] for h in range(GQA_GROUP)], axis=0), kvh
        return q_ref[0, pl.ds(r0, rows), j * dk:(j + 1) * dk], j

    def qk(r0, j, rows, n_keys, slot):
        q, kh = load_q(r0, j, rows)
        sbuf[slot, 0:n_keys, 0:q.shape[0]] = lax.dot_general(
            k_ref[0, kh, 0:n_keys, :], q, nt_dims, preferred_element_type=F32)

    def finish(r0, j, rows, n_keys, slot, par):
        cols = GQA_GROUP * ATTN_SUB if gqa else rows
        kh = j % GQA_KV_HEADS if gqa else j
        m = None
        for b0 in range(0, n_keys, key_blk):
            mb = jnp.max(sbuf[slot, b0:b0 + key_blk, 0:cols], axis=0, keepdims=True)
            m = mb if m is None else jnp.maximum(m, mb)
        for b0 in range(0, n_keys, key_blk):
            pbuf[par][b0:b0 + key_blk, 0:cols] = jnp.exp2(sbuf[slot, b0:b0 + key_blk, 0:cols] - m).astype(BF16)
        ot = jnp.dot(vt_ref[0, kh, :, 0:n_keys], pbuf[par][0:n_keys, 0:cols], preferred_element_type=F32)
        o = (ot[0:dv, :] / ot[dv:dv + 1, :]).T.astype(o_ref.dtype)
        if gqa:
            piece = j // GQA_KV_HEADS
            out = jnp.concatenate([o[h * ATTN_SUB:(h + 1) * ATTN_SUB, :] for h in range(GQA_GROUP)], axis=-1)
            o_ref[0, pl.ds(r0 - out_off + piece * ATTN_SUB, ATTN_SUB),
                  kh * GQA_GROUP * dv:(kh + 1) * GQA_GROUP * dv] = out
        else:
            o_ref[0, pl.ds(r0 - out_off, rows), j * dv:(j + 1) * dv] = o

    if with_ctx:
        rows_c = min(tq, n_ctx)
        for r in range(0, n_ctx, rows_c):
            for j in range(n_sub(rows_c)):
                qk(r, j, rows_c, n_ctx, 0)
                finish(r, j, rows_c, n_ctx, 0, 0)

    n_tiles = n_lat // tq
    ns = n_sub(tq)
    assert ns % ATTN_DEPTH == 0
    qk(n_ctx, 0, tq, t, 0)

    def body(i, carry):
        r0 = pl.multiple_of(n_ctx + i * tq, ATTN_SUB)
        r_next = pl.multiple_of(n_ctx + jnp.minimum(i + 1, n_tiles - 1) * tq, ATTN_SUB)
        for j in range(ns):
            if j + 1 < ns:
                qk(r0, j + 1, tq, t, (j + 1) % ATTN_DEPTH)
            else:
                qk(r_next, 0, tq, t, 0)
            finish(r0, j, tq, t, j % ATTN_DEPTH, j % 2)
        return carry

    lax.fori_loop(0, n_tiles, body, 0)


def _attention(q, k, vt, *, gqa, n_ctx, with_ctx, name):
    b, t, qw = q.shape
    hk, dk = k.shape[1], k.shape[-1]
    dv = MLA_V if not gqa else dk
    n_lat = t - n_ctx
    tq = _tile(n_lat, ATTN_Q_TILE)
    cols = GQA_GROUP * ATTN_SUB if gqa else tq
    t_out = t if with_ctx else n_lat
    w_out = (qw // dk) * dv
    kern = functools.partial(_attn_kernel, gqa=gqa, dk=dk, dv=dv, n_ctx=n_ctx, n_lat=n_lat,
                             with_ctx=with_ctx, tq=tq)
    return pl.pallas_call(
        kern,
        grid=(b,),
        in_specs=[pl.BlockSpec((1, t, qw), lambda bb: (bb, 0, 0)),
                  pl.BlockSpec((1, hk, t, dk), lambda bb: (bb, 0, 0, 0)),
                  pl.BlockSpec((1, hk, V_ROWS, t), lambda bb: (bb, 0, 0, 0))],
        out_specs=pl.BlockSpec((1, t_out, w_out), lambda bb: (bb, 0, 0)),
        out_shape=jax.ShapeDtypeStruct((b, t_out, w_out), BF16),
        scratch_shapes=[pltpu.VMEM((ATTN_DEPTH, t, cols), F32),
                        pltpu.VMEM((t, cols), BF16), pltpu.VMEM((t, cols), BF16)],
        compiler_params=_cparams(("arbitrary",), V7X_VMEM_LIMIT_BYTES),
        name=name,
    )(q, k, vt)


def _merge_kernel(*refs, n_x, nc, d, n_exp):
    x_refs, refs = refs[:n_x], refs[n_x:]
    (oa_ref, ol_ref, om_ref, gt_ref, m_ref, g_ref, wa_ref, wl_ref, wm_ref, wo_ref, rw_ref, rb_ref,
     xo_ref, h_ref, ids_ref, wts_ref) = refs
    tm = xo_ref.shape[1]
    parts = [slice(k * tm // MERGE_PARTS, (k + 1) * tm // MERGE_PARTS) for k in range(MERGE_PARTS)]
    x_in = _read_stream(x_refs, pl.program_id(1), nc)
    merged = []
    for r in parts:
        gates = gt_ref[0, r, :]
        merged.append(
            gates[:, 0:d].astype(F32) * jnp.dot(oa_ref[0, r, :], wa_ref[0], preferred_element_type=F32)
            + gates[:, d:2 * d].astype(F32) * jnp.dot(ol_ref[0, r, :], wl_ref[0], preferred_element_type=F32)
            + gates[:, 2 * d:3 * d].astype(F32) * jnp.dot(om_ref[0, r, :], wm_ref[0], preferred_element_type=F32))
    ys = [jnp.dot(mg.astype(BF16), wo_ref[0], preferred_element_type=F32) for mg in merged]
    his, los = [], []
    for r, y in zip(parts, ys):
        x = x_in[r, :] + m_ref[0, 0, 2:3, :] * y
        xo_ref[0, r, :] = x
        h = _rms(x) * g_ref[0]
        h = h * (1.0 + m_ref[0, 0, 4:5, :]) + m_ref[0, 0, 3:4, :]
        h_hi = h.astype(BF16)
        h_ref[0, r, :] = h_hi
        his.append(h_hi)
        los.append((h - h_hi.astype(F32)).astype(BF16))

    scores = jnp.concatenate(
        [_sigmoid(jnp.dot(jnp.concatenate([h_hi, h_lo, h_hi], axis=-1), rw_ref[...],
                          preferred_element_type=F32)).T[0:n_exp, :]
         for h_hi, h_lo in zip(his, los)], axis=-1)
    sel = scores + rb_ref[...]
    per = n_exp // N_GROUPS
    gs = []
    for g in range(N_GROUPS):
        r = [sel[g * per + j:g * per + j + 1, :] for j in range(per)]
        best = None
        for a in range(per):
            for bq in range(a + 1, per):
                pair = r[a] + r[bq]
                best = pair if best is None else jnp.maximum(best, pair)
        gs.append(best)
    gmax = functools.reduce(jnp.maximum, gs)
    gbest = jnp.full(gmax.shape, N_GROUPS - 1, jnp.int32)
    for g in range(N_GROUPS - 2, -1, -1):
        gbest = jnp.where(gs[g] == gmax, g, gbest)
    eid = lax.broadcasted_iota(jnp.int32, sel.shape, 0)
    gid = jnp.zeros(sel.shape, jnp.int32)
    for g in range(1, N_GROUPS):
        gid = gid + (eid >= g * per).astype(jnp.int32)
    masked = jnp.where(gid == gbest, sel, -jnp.inf)
    m1 = jnp.max(masked, axis=0, keepdims=True)
    i1 = jnp.min(jnp.where(masked == m1, eid, n_exp), axis=0, keepdims=True)
    masked2 = jnp.where(eid == i1, -jnp.inf, masked)
    m2 = jnp.max(masked2, axis=0, keepdims=True)
    i2 = jnp.min(jnp.where(masked2 == m2, eid, n_exp), axis=0, keepdims=True)
    s1 = jnp.sum(jnp.where(eid == i1, scores, 0.0), axis=0, keepdims=True)
    s2 = jnp.sum(jnp.where(eid == i2, scores, 0.0), axis=0, keepdims=True)
    tot = s1 + s2
    ids_ref[0] = jnp.concatenate([i1, i2], axis=0)
    wts_ref[0] = jnp.concatenate([ROUTED_SCALE * s1 / tot, ROUTED_SCALE * s2 / tot], axis=0)


def _merge(streams, o_attn, o_lru, o_mla, gates, layer, p, n_ctx, with_ctx):
    b, d = streams[0].shape[0], streams[0].shape[-1]
    t = o_lru.shape[1]
    n_exp = p["r_bias"].shape[0]
    tm = _tile(math.gcd(n_ctx, t), ROW_TILE)
    off = 0 if with_ctx else n_ctx // tm
    nc = n_ctx // tm
    tq = t if with_ctx else t - n_ctx
    nt = tq // tm
    full = lambda bb, i: (bb, i + off, 0)
    qrow = lambda bb, i: (bb, i, 0)
    const = lambda bb, i: (0, 0)
    lay3 = lambda bb, i: (layer, 0, 0)
    bw = o_attn.shape[-1]
    route_spec = pl.BlockSpec((1, 2, tm), lambda bb, i: (bb * nt + i, 0, 0))
    wspec = lambda a: pl.BlockSpec((1,) + a.shape[1:], lay3)
    in_specs = _stream_specs(streams, tm, nc, lambda bb, i: i + off, lambda bb, i: bb)
    in_specs += [pl.BlockSpec((1, tm, bw), qrow),
                 pl.BlockSpec((1, tm, o_lru.shape[-1]), full),
                 pl.BlockSpec((1, tm, bw), qrow),
                 pl.BlockSpec((1, tm, 3 * d), full),
                 pl.BlockSpec((1, 1, N_MOD, d), lambda bb, i: (layer, jnp.where(i + off < nc, 0, bb + 1), 0, 0)),
                 pl.BlockSpec((1, 1, d), lay3),
                 wspec(p["w_ba"]), wspec(p["w_bl"]), wspec(p["w_bm"]), wspec(p["w_out"]),
                 pl.BlockSpec(p["r_w"].shape, const),
                 pl.BlockSpec(p["r_bias"].shape, const)]
    return pl.pallas_call(
        functools.partial(_merge_kernel, n_x=len(streams), nc=nc - off, d=d, n_exp=n_exp),
        grid=(b, nt),
        in_specs=in_specs,
        out_specs=[pl.BlockSpec((1, tm, d), qrow), pl.BlockSpec((1, tm, d), qrow),
                   route_spec, route_spec],
        out_shape=[jax.ShapeDtypeStruct((b, tq, d), F32), jax.ShapeDtypeStruct((b, tq, d), BF16),
                   jax.ShapeDtypeStruct((b * nt, 2, tm), jnp.int32),
                   jax.ShapeDtypeStruct((b * nt, 2, tm), F32)],
        compiler_params=_cparams(("arbitrary", "arbitrary"), V7X_VMEM_LIMIT_BYTES),
        name="merge_router",
    )(*streams, o_attn, o_lru, o_mla, gates, p["mods"], p["norm_ffn"], p["w_ba"], p["w_bl"], p["w_bm"],
      p["w_out"], p["r_w"], p["r_bias"])


def _moe_kernel(h_ref, idr_ref, idc_ref, wtc_ref, low_ref, wg_ref, wu_ref, wd_ref, o_ref,
                xs, ys, cs, posc, meta, *, per, n_pairs):
    s = pl.program_id(1)
    tm = h_ref.shape[0]
    slots = xs.shape[0]
    pairs_per_group = per // 2

    @pl.when(s == 0)
    def _sort_tokens():
        e_row = idr_ref[0, 0:1, :]
        e_col = idc_ref[:, 0:1]
        g_row = jnp.zeros(e_row.shape, jnp.int32)
        g_col = jnp.zeros(e_col.shape, jnp.int32)
        for g in range(1, N_GROUPS):
            g_row = g_row + (e_row >= g * per).astype(jnp.int32)
            g_col = g_col + (e_col >= g * per).astype(jnp.int32)
        sub8 = lax.broadcasted_iota(jnp.int32, (V7X_SUBLANES, tm), 0)
        lane = lax.broadcasted_iota(jnp.int32, (tm, V7X_LANES), 1)
        onehot_row = jnp.where(sub8 == g_row, 1.0, 0.0)
        onehot_col = jnp.where(lane == g_col, 1.0, 0.0)
        rank_col = jnp.dot(low_ref[...], onehot_col.astype(BF16), preferred_element_type=F32)
        rank_row = lax.dot_general(onehot_row.astype(BF16), low_ref[...], (((1,), (1,)), ((), ())),
                                   preferred_element_type=F32)
        start = 0
        start_row = jnp.zeros((V7X_SUBLANES, 1), F32)
        start_col = jnp.zeros((1, V7X_LANES), F32)
        for g in range(N_GROUPS):
            cnt = jnp.sum((g_row == g).astype(jnp.int32))
            n_sub = lax.shift_right_logical(cnt + (V7X_LANES - 1), 7)
            meta[g] = start
            meta[N_GROUPS + g] = n_sub
            start_f = jnp.asarray(start, jnp.int32).astype(F32)
            start_row = jnp.where(lax.broadcasted_iota(jnp.int32, start_row.shape, 0) == g, start_f, start_row)
            start_col = jnp.where(lax.broadcasted_iota(jnp.int32, start_col.shape, 1) == g, start_f, start_col)
            start = start + n_sub * V7X_LANES
        pos_row = jnp.sum(onehot_row * (rank_row + start_row), axis=0, keepdims=True).astype(jnp.int32)
        pos_col = jnp.sum(onehot_col * (rank_col + start_col), axis=1, keepdims=True).astype(jnp.int32)
        posc[...] = pos_col
        slot = lax.broadcasted_iota(jnp.int32, (slots, tm), 0)
        perm = jnp.where(slot == pos_row, 1.0, 0.0).astype(BF16)
        xs[...] = jnp.dot(perm, h_ref[...], preferred_element_type=F32).astype(BF16)
        ids_c, wts_c = idc_ref[...], wtc_ref[...]
        comb = jnp.zeros((tm, V7X_LANES), F32)
        for k in range(2):
            local = jnp.bitwise_and(ids_c[:, k:k + 1], per - 1)
            comb = comb + jnp.where(lane == local, wts_c[:, k:k + 1], 0.0)
        c_hi = comb.astype(BF16)
        c_lo = (comb - c_hi.astype(F32)).astype(BF16)
        both = jnp.dot(perm, jnp.concatenate([c_hi, c_lo], axis=-1), preferred_element_type=F32)
        cs[...] = both[:, :V7X_LANES] + both[:, V7X_LANES:]
        ys[...] = jnp.zeros_like(ys)

    grp = s // pairs_per_group
    first_local = (s % pairs_per_group) * 2
    seg_start = meta[grp]
    lane_sub = lax.broadcasted_iota(jnp.int32, (V7X_LANES, V7X_LANES), 1)

    def sub_tile(k, carry):
        r0 = pl.multiple_of(seg_start + k * V7X_LANES, V7X_LANES)
        x = xs[pl.ds(r0, V7X_LANES), :]
        c_all = cs[pl.ds(r0, V7X_LANES), :]
        acc = None
        for jj in range(2):
            c = jnp.sum(jnp.where(lane_sub == first_local + jj, c_all, 0.0), axis=-1, keepdims=True)
            g = jnp.dot(x, wg_ref[0, jj], preferred_element_type=F32)
            u = jnp.dot(x, wu_ref[0, jj], preferred_element_type=F32)
            act = ((g * _sigmoid(g)) * u * c).astype(BF16)
            part = jnp.dot(act, wd_ref[0, jj], preferred_element_type=F32)
            acc = part if acc is None else acc + part
        ys[pl.ds(r0, V7X_LANES), :] = ys[pl.ds(r0, V7X_LANES), :] + acc
        return carry

    lax.fori_loop(0, meta[N_GROUPS + grp], sub_tile, 0)

    @pl.when(s == n_pairs - 1)
    def _unsort():
        slot_l = lax.broadcasted_iota(jnp.int32, (tm, slots), 1)
        perm_t = jnp.where(slot_l == posc[...], 1.0, 0.0).astype(BF16)
        y = ys[...]
        y_hi = y.astype(BF16)
        y_lo = (y - y_hi.astype(F32)).astype(BF16)
        o_ref[...] = (jnp.dot(perm_t, y_hi, preferred_element_type=F32)
                      + jnp.dot(perm_t, y_lo, preferred_element_type=F32))


def _moe(h, ids, wts, layer, p):
    n, d = h.shape
    _, n_exp, _, ff = p["moe_wg"].shape
    per = n_exp // N_GROUPS
    assert per % 2 == 0 and per & (per - 1) == 0
    rt = ids.shape[-1]
    tm = _tile(n, MOE_ROW_TILE)
    assert tm % rt == 0 and tm % V7X_LANES == 0
    slots = tm + N_GROUPS * V7X_LANES
    ids_row = ids.reshape(n // tm, tm // rt, 2, rt).transpose(0, 2, 1, 3).reshape(n // tm, 2, tm)
    ids_col = jnp.transpose(ids, (0, 2, 1)).reshape(n, 2)
    wts_col = jnp.transpose(wts, (0, 2, 1)).reshape(n, 2)
    low = jnp.tril(jnp.ones((tm, tm), BF16), -1)
    n_pairs = n_exp // 2
    wspec = lambda a, b: pl.BlockSpec((1, 2, a, b), lambda i, s: (layer, s, 0, 0))
    return pl.pallas_call(
        functools.partial(_moe_kernel, per=per, n_pairs=n_pairs),
        grid=(n // tm, n_pairs),
        in_specs=[pl.BlockSpec((tm, d), lambda i, s: (i, 0)),
                  pl.BlockSpec((1, 2, tm), lambda i, s: (i, 0, 0)),
                  pl.BlockSpec((tm, 2), lambda i, s: (i, 0)),
                  pl.BlockSpec((tm, 2), lambda i, s: (i, 0)),
                  pl.BlockSpec((tm, tm), lambda i, s: (0, 0), pipeline_mode=pl.Buffered(1)),
                  wspec(d, ff), wspec(d, ff), wspec(ff, d)],
        out_specs=pl.BlockSpec((tm, d), lambda i, s: (i, 0)),
        out_shape=jax.ShapeDtypeStruct((n, d), F32),
        scratch_shapes=[pltpu.VMEM((slots, d), BF16), pltpu.VMEM((slots, d), F32),
                        pltpu.VMEM((slots, V7X_LANES), F32), pltpu.VMEM((tm, 1), jnp.int32),
                        pltpu.SMEM((2 * N_GROUPS,), jnp.int32)],
        compiler_params=_cparams(("arbitrary", "arbitrary"), V7X_VMEM_LIMIT_BYTES),
        name="moe_ffn",
    )(h, ids_row, ids_col, wts_col, low, p["moe_wg"], p["moe_wu"], p["moe_wd"])


def _final_kernel(x_ref, f_ref, m_ref, g_ref, o_ref):
    x = x_ref[0] + m_ref[0, 0, 5:6, :] * f_ref[0]
    o_ref[0] = _rms(x) * g_ref[...]


def _final(x, f, mods, layer, g):
    b, s, d = x.shape
    tm = _tile(s, 2 * ROW_TILE)
    row = lambda bb, i: (bb, i, 0)
    return pl.pallas_call(
        _final_kernel,
        grid=(b, s // tm),
        in_specs=[pl.BlockSpec((1, tm, d), row), pl.BlockSpec((1, tm, d), row),
                  pl.BlockSpec((1, 1, N_MOD, d), lambda bb, i: (layer, bb + 1, 0, 0)),
                  pl.BlockSpec((1, d), lambda bb, i: (0, 0))],
        out_specs=pl.BlockSpec((1, tm, d), row),
        out_shape=jax.ShapeDtypeStruct((b, s, d), F32),
        compiler_params=_cparams(("arbitrary", "arbitrary")),
        name="final_norm",
    )(x, f, mods, g.reshape(1, d))


def _rope_tables(n_ctx, n_lat, dim, period, lane_off, width):
    quarter = dim // 4
    pos = jnp.arange(n_lat, dtype=F32)
    r, col = jnp.floor(pos / GRID_W), pos - GRID_W * jnp.floor(pos / GRID_W)
    inv_freq = ROPE_THETA ** (-jnp.arange(quarter, dtype=F32) / quarter)
    lane = np.arange(width)
    j = (lane % period) - lane_off
    active = (j >= 0) & (j < dim)
    jj = np.where(active, j, 0)
    use_col = jj >= dim // 2
    upper = (jj % (dim // 2)) >= quarter
    f = jj % quarter
    ang = jnp.where(use_col[None, :], col[:, None], r[:, None]) * inv_freq[f][None, :]
    act = jnp.asarray(active)[None, :]
    cos = jnp.where(act, jnp.cos(ang), 1.0)
    sin = jnp.where(act, jnp.sin(ang), 0.0)
    sin_up = jnp.where(jnp.asarray(~upper)[None, :], -sin, 0.0)
    sin_dn = jnp.where(jnp.asarray(upper)[None, :], sin, 0.0)
    ident = lambda v, fill: jnp.concatenate([jnp.full((n_ctx, width), fill, F32), v], axis=0)
    return ident(cos, 1.0), ident(sin_up, 0.0), ident(sin_dn, 0.0)


def _lru_gate_weights(w_a, w_i):
    depth, _, nb, k, j = w_a.shape
    w = jnp.stack([w_a, w_i], axis=2)
    eye = jnp.eye(nb, dtype=w.dtype)
    dense = jnp.einsum('dsgnkj,nm->dnksgmj', w, eye)
    return (0.5 * dense).reshape(depth, nb * k, 4 * nb * j).astype(BF16)


def kernel(x, c, ctx, c_ctx, w_mod, b_mod, norm_mix, norm_ffn, w_in, gqa_q_norm, gqa_k_norm, conv_w, conv_b,
           lru_w_a, lru_b_a, lru_w_i, lru_b_i, lru_lam, mla_q_a_norm, mla_w_qb, mla_kv_a_norm, mla_w_kvb,
           w_branch_attn, w_branch_lru, w_branch_mla, w_out, router_w, router_bias,
           moe_w_gate, moe_w_up, moe_w_down, final_norm):
    bsz, n_lat, d = x.shape
    n_ctx = ctx.shape[1]
    depth = w_mod.shape[0]
    hd = gqa_q_norm.shape[-1]
    lru_w = conv_w.shape[-1]
    q_lora, kv_lora = mla_q_a_norm.shape[-1], mla_kv_a_norm.shape[-1]
    n_exp = router_w.shape[-1]
    assert bsz + 1 <= MOD_ROWS and d % V7X_LANES == 0

    cc = jnp.zeros((MOD_ROWS, d), F32).at[0].set(c_ctx).at[1:1 + bsz].set(c)
    mods = _modulation(cc, w_mod, b_mod).reshape(depth, MOD_ROWS, N_MOD, d)

    mq_period = MLA_QK * V7X_LANES // math.gcd(MLA_QK, V7X_LANES)
    kr_end = GQA_HEADS * hd + 2 * GQA_KV_HEADS * hd + 2 * lru_w + q_lora + kv_lora + MLA_ROPE
    w_in_b = _pack_w_in(w_in, kr_end)
    r_pad = jnp.pad(router_w, ((0, 0), (0, V7X_LANES - n_exp)))
    r_hi = r_pad.astype(BF16)
    r_w = jnp.concatenate([r_hi, r_hi, (r_pad - r_hi.astype(F32)).astype(BF16)], axis=0)
    p = dict(
        hd=hd, lru_w=lru_w, q_lora=q_lora, kv_lora=kv_lora, mods=mods,
        norm_mix=norm_mix.reshape(depth, 1, d), norm_ffn=norm_ffn.reshape(depth, 1, d), w_in=w_in_b,
        gq=jnp.tile(gqa_q_norm, (1, GQA_HEADS)).reshape(depth, 1, -1),
        gk=jnp.tile(gqa_k_norm, (1, GQA_KV_HEADS)).reshape(depth, 1, -1),
        gqa_tabs=_rope_tables(n_ctx, n_lat, hd, hd, 0, V7X_LANES),
        mla_q_tabs=_rope_tables(n_ctx, n_lat, MLA_ROPE, MLA_QK, MLA_NOPE, mq_period),
        mla_k_tabs=_rope_tables(n_ctx, n_lat, MLA_ROPE, V7X_LANES, 0, V7X_LANES),
        mla_q_norm=mla_q_a_norm.reshape(depth, 1, q_lora), mla_kv_norm=mla_kv_a_norm.reshape(depth, 1, kv_lora),
        w_qb=mla_w_qb.astype(BF16), w_kvb=mla_w_kvb.astype(BF16),
        conv_w=conv_w, conv_b=conv_b.reshape(depth, 1, lru_w),
        lru_wg=_lru_gate_weights(lru_w_a, lru_w_i),
        lru_bg=0.5 * jnp.stack([lru_b_a, lru_b_i], axis=2).reshape(depth, 1, 4 * lru_w),
        lru_lam=lru_lam,
        w_ba=w_branch_attn.astype(BF16), w_bl=w_branch_lru.astype(BF16), w_bm=w_branch_mla.astype(BF16),
        w_out=w_out.astype(BF16), r_w=r_w,
        r_bias=router_bias.reshape(n_exp, 1).astype(F32),
        moe_wg=moe_w_gate.astype(BF16), moe_wu=moe_w_up.astype(BF16), moe_wd=moe_w_down.astype(BF16),
    )

    streams = (ctx, x)
    prev = None
    for layer in range(depth):
        last = layer == depth - 1
        streams, (q_g, k_g, v_g, zlru, q_m, k_m, v_m, gates) = _in_proj(streams, prev, layer, p, n_ctx)
        o_attn = _attention(q_g, k_g, v_g, gqa=True, n_ctx=n_ctx, with_ctx=not last, name="gqa_attention")
        o_lru = _lru(zlru, layer, p, n_ctx)
        o_mla = _attention(q_m, k_m, v_m, gqa=False, n_ctx=n_ctx, with_ctx=not last, name="mla_attention")
        x_mid, h2, ids, wts = _merge(streams, o_attn, o_lru, o_mla, gates, layer, p, n_ctx, with_ctx=not last)

        n_tok = x_mid.shape[0] * x_mid.shape[1]
        f = _moe(h2.reshape(n_tok, d), ids, wts, layer, p).reshape(x_mid.shape)
        streams, prev = (x_mid,), (f, layer)

    return _final(streams[0], prev[0], mods, prev[1], final_norm)
```

```python
import functools
import math

import numpy as np
import jax
import jax.numpy as jnp
from jax import lax
from jax.experimental import pallas as pl
from jax.experimental.pallas import tpu as pltpu

F32 = jnp.float32
BF16 = jnp.bfloat16

GRID_W = 64
ROPE_THETA = 10000.0
NORM_EPS = 1e-6
N_MOD = 6
GQA_HEADS = 8
GQA_KV_HEADS = 2
GQA_GROUP = GQA_HEADS // GQA_KV_HEADS
MLA_HEADS = 8
MLA_NOPE = 64
MLA_ROPE = 32
MLA_V = 64
MLA_QK = MLA_NOPE + MLA_ROPE
CONV_WIDTH = 4
LRU_C = 8.0
N_GROUPS = 4
ROUTED_SCALE = 1.0
LOG2E = math.log2(math.e)

V7X_LANES = 128
V7X_SUBLANES = 8
V7X_VMEM_LIMIT_BYTES = 56 * 1024 * 1024

ROW_TILE = 256
ATTN_Q_TILE = 512
ATTN_SUB = 128
ATTN_DEPTH = 2
ATTN_KEY_BLOCK = 256
V_ROWS = 80
MERGE_PARTS = 2
MOE_ROW_TILE = 1024
MOD_ROWS = 16


def _cparams(sem, vmem=None):
    return pltpu.CompilerParams(dimension_semantics=sem, vmem_limit_bytes=vmem)


def _tile(n, pref):
    t = min(n, pref)
    while n % t or t % V7X_SUBLANES:
        t -= 1
    return t


def _sigmoid(x):
    return 0.5 * (1.0 + jnp.tanh(0.5 * x))


def _rms(x):
    return x * lax.rsqrt(jnp.mean(x * x, axis=-1, keepdims=True) + NORM_EPS)


def _tiled(tbl, width):
    reps = width // tbl.shape[-1]
    return tbl if reps == 1 else jnp.concatenate([tbl] * reps, axis=-1)


def _rope_lanes(x, cos, sin_up, sin_dn, half):
    outs = []
    for c in range(x.shape[-1] // V7X_LANES):
        sl = slice(c * V7X_LANES, (c + 1) * V7X_LANES)
        xc = x[:, sl]
        up = pltpu.roll(xc, V7X_LANES - half, 1)
        dn = pltpu.roll(xc, half, 1)
        outs.append(xc * cos[:, sl] + up * sin_up[:, sl] + dn * sin_dn[:, sl])
    return outs[0] if len(outs) == 1 else jnp.concatenate(outs, axis=-1)


def _ones_row_tail(rows, cols):
    r = lax.broadcasted_iota(jnp.int32, (rows, cols), 0)
    return jnp.where(r == 0, 1.0, 0.0).astype(BF16)


def _stream_specs(streams, tm, nc, tile_of, batch_of):
    d = streams[0].shape[-1]
    if len(streams) == 1:
        return [pl.BlockSpec((1, tm, d), lambda *g: (batch_of(*g), tile_of(*g), 0))]
    return [pl.BlockSpec((1, tm, d), lambda *g: (batch_of(*g), jnp.minimum(tile_of(*g), nc - 1), 0)),
            pl.BlockSpec((1, tm, d), lambda *g: (batch_of(*g), jnp.maximum(tile_of(*g) - nc, 0), 0))]


def _read_stream(refs, tile, nc):
    if len(refs) == 1:
        return refs[0][0]
    return jnp.where(tile < nc, refs[0][0], refs[1][0])


def _mod_kernel(c_ref, w_ref, b_ref, o_ref):
    c = c_ref[...]
    a = (c * _sigmoid(c)).astype(BF16)
    o_ref[0] = jnp.dot(a, w_ref[0].astype(BF16), preferred_element_type=F32) + b_ref[0]


def _modulation(cc, w_mod, b_mod):
    depth, d, n = w_mod.shape
    tn = _tile(n, 1536) if n % V7X_LANES == 0 else n
    return pl.pallas_call(
        _mod_kernel,
        grid=(depth, n // tn),
        in_specs=[pl.BlockSpec((MOD_ROWS, d), lambda l, j: (0, 0)),
                  pl.BlockSpec((1, d, tn), lambda l, j: (l, 0, j)),
                  pl.BlockSpec((1, 1, tn), lambda l, j: (l, 0, j))],
        out_specs=pl.BlockSpec((1, MOD_ROWS, tn), lambda l, j: (l, 0, j)),
        out_shape=jax.ShapeDtypeStruct((depth, MOD_ROWS, n), F32),
        compiler_params=_cparams(("arbitrary", "arbitrary"), V7X_VMEM_LIMIT_BYTES),
        name="modulation",
    )(cc, w_mod, b_mod.reshape(depth, 1, n))


def _pack_w_in_kernel(w_ref, o_ref, *, kr_end):
    w = w_ref[0].astype(BF16)
    pad = jnp.zeros((w.shape[0], o_ref.shape[-1] - w.shape[-1]), BF16)
    o_ref[0] = jnp.concatenate([w[:, :kr_end], pad, w[:, kr_end:]], axis=-1)


def _pack_w_in(w_in, kr_end):
    depth, d, n = w_in.shape
    n_out = n + V7X_LANES - MLA_ROPE
    rt = _tile(d, V7X_LANES)
    return pl.pallas_call(
        functools.partial(_pack_w_in_kernel, kr_end=kr_end),
        grid=(depth, d // rt),
        in_specs=[pl.BlockSpec((1, rt, n), lambda l, i: (l, i, 0))],
        out_specs=pl.BlockSpec((1, rt, n_out), lambda l, i: (l, i, 0)),
        out_shape=jax.ShapeDtypeStruct((depth, d, n_out), BF16),
        compiler_params=_cparams(("arbitrary", "arbitrary")),
        name="pack_w_in",
    )(w_in)


def _in_proj_kernel(*refs, n_x, has_prev, nc, hd, lru_w, q_lora, kv_lora, d):
    tile = pl.program_id(0)
    x_refs, refs = refs[:n_x], refs[n_x:]
    if has_prev:
        f_ref, pm_ref, refs = refs[0], refs[1], refs[2:]
    (g_ref, m_ref, w_ref, gq_ref, gk_ref, gc_ref, gu_ref, gd_ref, nqa_ref, nkv_ref, wq_ref, wkv_ref,
     qc_ref, qu_ref, qd_ref, kc_ref, ku_ref, kd_ref) = refs[:18]
    outs = refs[18:]
    x = _read_stream(x_refs, tile, nc)
    if has_prev:
        x = x + pm_ref[0, 0, 5:6, :] * f_ref[0]
        outs[0][0] = x
        outs = outs[1:]
    qg_ref, kg_ref, vg_ref, lru_ref, qm_ref, km_ref, vm_ref, gate_ref = outs
    tm = x.shape[0]
    h = _rms(x) * g_ref[0]
    h = h * (1.0 + m_ref[0, 0, 1:2, :]) + m_ref[0, 0, 0:1, :]
    hb = h.astype(BF16)

    nq, nk = GQA_HEADS * hd, GQA_KV_HEADS * hd
    w_gqa = nq + 2 * nk
    w_mla = q_lora + kv_lora + V7X_LANES
    o_lru, o_mla, o_gate = w_gqa, w_gqa + 2 * lru_w, w_gqa + 2 * lru_w + w_mla

    z = jnp.dot(hb, w_ref[0, :, 0:w_gqa], preferred_element_type=F32)

    def head_norm(a, n_heads):
        return jnp.concatenate([_rms(a[:, i * hd:(i + 1) * hd]) for i in range(n_heads)], axis=-1)

    cos, su, sd = gc_ref[...], gu_ref[...], gd_ref[...]
    q = head_norm(z[:, :nq], GQA_HEADS) * gq_ref[0]
    q = _rope_lanes(q, _tiled(cos, nq), _tiled(su, nq), _tiled(sd, nq), hd // 4)
    qg_ref[0] = (q * (LOG2E * hd ** -0.5)).astype(BF16)
    k = head_norm(z[:, nq:nq + nk], GQA_KV_HEADS) * gk_ref[0]
    k = _rope_lanes(k, _tiled(cos, nk), _tiled(su, nk), _tiled(sd, nk), hd // 4)
    v_t = z[:, nq + nk:nq + 2 * nk].T
    tail = _ones_row_tail(V_ROWS - hd, tm)
    for i in range(GQA_KV_HEADS):
        kg_ref[0, i] = k[:, i * hd:(i + 1) * hd].astype(BF16)
        vg_ref[0, i, 0:hd, :] = v_t[i * hd:(i + 1) * hd, :].astype(BF16)
        vg_ref[0, i, hd:V_ROWS, :] = tail

    lru_ref[0] = jnp.dot(hb, w_ref[0, :, o_lru:o_lru + 2 * lru_w], preferred_element_type=F32)

    z = jnp.dot(hb, w_ref[0, :, o_mla:o_mla + w_mla], preferred_element_type=F32)
    cq = (_rms(z[:, :q_lora]) * nqa_ref[0]).astype(BF16)
    qm = jnp.dot(cq, wq_ref[0], preferred_element_type=F32)
    wq = qm.shape[-1]
    qm = _rope_lanes(qm, _tiled(qc_ref[...], wq), _tiled(qu_ref[...], wq), _tiled(qd_ref[...], wq),
                     MLA_ROPE // 4)
    qm_ref[0] = (qm * (LOG2E * MLA_QK ** -0.5)).astype(BF16)
    ckv = (_rms(z[:, q_lora:q_lora + kv_lora]) * nkv_ref[0]).astype(BF16)
    kv = jnp.dot(ckv, wkv_ref[0], preferred_element_type=F32)
    kr = _rope_lanes(z[:, q_lora + kv_lora:], kc_ref[...], ku_ref[...], kd_ref[...], MLA_ROPE // 4)
    kr = kr[:, :MLA_ROPE]
    per = MLA_NOPE + MLA_V
    tail = _ones_row_tail(V_ROWS - MLA_V, tm)
    for i in range(MLA_HEADS):
        km_ref[0, i] = jnp.concatenate([kv[:, i * per:i * per + MLA_NOPE], kr], axis=-1).astype(BF16)
        head_t = kv[:, i * per:(i + 1) * per].T
        vm_ref[0, i, 0:MLA_V, :] = head_t[MLA_NOPE:per, :].astype(BF16)
        vm_ref[0, i, MLA_V:V_ROWS, :] = tail

    gate_ref[0] = _sigmoid(jnp.dot(hb, w_ref[0, :, o_gate:o_gate + 3 * d],
                                   preferred_element_type=F32)).astype(BF16)


def _in_proj(streams, prev, layer, p, n_ctx):
    b, d = streams[0].shape[0], streams[0].shape[-1]
    t = sum(s.shape[1] for s in streams) if len(streams) == 2 else streams[0].shape[1]
    tm = _tile(math.gcd(n_ctx, t), ROW_TILE)
    nc = n_ctx // tm
    hd, lru_w, q_lora, kv_lora = p["hd"], p["lru_w"], p["q_lora"], p["kv_lora"]
    nq, nk = GQA_HEADS * hd, GQA_KV_HEADS * hd
    qw = MLA_HEADS * MLA_QK
    tile_of, batch_of = (lambda i, bb: i), (lambda i, bb: bb)
    row = lambda i, bb: (bb, i, 0)
    mod_row = lambda i, bb: jnp.where(i < nc, 0, bb + 1)
    lay3 = lambda i, bb: (layer, 0, 0)
    tbl = lambda a: pl.BlockSpec((tm, a.shape[-1]), lambda i, bb: (i, 0))
    x_spec = pl.BlockSpec((1, tm, d), row)

    in_specs = _stream_specs(streams, tm, nc, tile_of, batch_of)
    args = list(streams)
    out_specs, out_shape = [], []
    if prev is not None:
        in_specs += [x_spec, pl.BlockSpec((1, 1, N_MOD, d), lambda i, bb: (prev[1], mod_row(i, bb), 0, 0))]
        args += [prev[0], p["mods"]]
        out_specs.append(x_spec)
        out_shape.append(jax.ShapeDtypeStruct((b, t, d), F32))
    in_specs += [pl.BlockSpec((1, 1, d), lay3),
                 pl.BlockSpec((1, 1, N_MOD, d), lambda i, bb: (layer, mod_row(i, bb), 0, 0)),
                 pl.BlockSpec((1,) + p["w_in"].shape[1:], lay3, pipeline_mode=pl.Buffered(1)),
                 pl.BlockSpec((1, 1, nq), lay3), pl.BlockSpec((1, 1, nk), lay3)]
    args += [p["norm_mix"], p["mods"], p["w_in"], p["gq"], p["gk"]]
    in_specs += [tbl(a) for a in p["gqa_tabs"]]
    args += list(p["gqa_tabs"])
    in_specs += [pl.BlockSpec((1, 1, q_lora), lay3), pl.BlockSpec((1, 1, kv_lora), lay3),
                 pl.BlockSpec((1,) + p["w_qb"].shape[1:], lay3),
                 pl.BlockSpec((1,) + p["w_kvb"].shape[1:], lay3)]
    args += [p["mla_q_norm"], p["mla_kv_norm"], p["w_qb"], p["w_kvb"]]
    in_specs += [tbl(a) for a in p["mla_q_tabs"]] + [tbl(a) for a in p["mla_k_tabs"]]
    args += list(p["mla_q_tabs"]) + list(p["mla_k_tabs"])

    def kv_specs(heads, dk):
        return [pl.BlockSpec((1, heads, tm, dk), lambda i, bb: (bb, 0, i, 0)),
                pl.BlockSpec((1, heads, V_ROWS, tm), lambda i, bb: (bb, 0, 0, i))]

    out_specs += ([pl.BlockSpec((1, tm, nq), row)] + kv_specs(GQA_KV_HEADS, hd)
                  + [pl.BlockSpec((1, tm, 2 * lru_w), row), pl.BlockSpec((1, tm, qw), row)]
                  + kv_specs(MLA_HEADS, MLA_QK) + [pl.BlockSpec((1, tm, 3 * d), row)])
    out_shape += [jax.ShapeDtypeStruct((b, t, nq), BF16),
                  jax.ShapeDtypeStruct((b, GQA_KV_HEADS, t, hd), BF16),
                  jax.ShapeDtypeStruct((b, GQA_KV_HEADS, V_ROWS, t), BF16),
                  jax.ShapeDtypeStruct((b, t, 2 * lru_w), F32),
                  jax.ShapeDtypeStruct((b, t, qw), BF16),
                  jax.ShapeDtypeStruct((b, MLA_HEADS, t, MLA_QK), BF16),
                  jax.ShapeDtypeStruct((b, MLA_HEADS, V_ROWS, t), BF16),
                  jax.ShapeDtypeStruct((b, t, 3 * d), BF16)]
    outs = pl.pallas_call(
        functools.partial(_in_proj_kernel, n_x=len(streams), has_prev=prev is not None, nc=nc, hd=hd,
                          lru_w=lru_w, q_lora=q_lora, kv_lora=kv_lora, d=d),
        grid=(t // tm, b),
        in_specs=in_specs, out_specs=out_specs, out_shape=out_shape,
        compiler_params=_cparams(("arbitrary", "arbitrary"), V7X_VMEM_LIMIT_BYTES),
        name="in_proj",
    )(*args)
    if prev is not None:
        return (outs[0],), outs[1:]
    return streams, outs


def _lru_kernel(z_ref, cw_ref, cb_ref, wg_ref, bg_ref, lam_ref, o_ref, a_f, b_f, a_b, b_b, *, n_ctx, chunk):
    t, width = a_f.shape
    n_chunks = t // chunk
    row = lax.broadcasted_iota(jnp.int32, (chunk, 1), 0)

    def conv_chunk(c):
        r0 = c * chunk
        seg_lo, seg_hi = (0, n_ctx) if r0 < n_ctx else (n_ctx, t)
        u = jnp.zeros((chunk, width), F32) + cb_ref[0]
        for j in range(CONV_WIDTH):
            off = j - 1
            lo = min(max(r0 + off, 0), t - chunk)
            tap = z_ref[0, lo:lo + chunk, 0:width]
            shift = (lo - (r0 + off)) % chunk
            if shift:
                tap = pltpu.roll(tap, shift, 0)
            if r0 + off < seg_lo or r0 + off + chunk > seg_hi:
                pos = row + (r0 + off)
                tap = jnp.where((pos >= seg_lo) & (pos < seg_hi), tap, 0.0)
            u = u + tap * cw_ref[0, j:j + 1, :]
        return u

    scr = ((a_f, b_f), (a_b, b_b))
    half_c_sp = []
    for dd in range(2):
        lam = lam_ref[0, dd:dd + 1, :]
        sp = jnp.maximum(-lam, 0.0) + jnp.log1p(jnp.exp(-jnp.abs(lam)))
        half_c_sp.append((-0.5 * LRU_C) * sp)
    for c in range(n_chunks):
        u = conv_chunk(c)
        g = jnp.dot(u.astype(BF16), wg_ref[0], preferred_element_type=F32) + bg_ref[0]
        half_u = 0.5 * u
        for dd in range(2):
            base = 2 * dd * width
            t_r = jnp.tanh(g[:, base:base + width])
            t_i = jnp.tanh(g[:, base + width:base + 2 * width])
            log_a = (1.0 + t_r) * half_c_sp[dd]
            th = jnp.tanh(log_a)
            v = (-2.0 * th) / (1.0 - th)
            root = jnp.where(v > 0.0, v * lax.rsqrt(v), 0.0)
            scr[dd][0][c * chunk:(c + 1) * chunk, :] = jnp.exp(log_a)
            scr[dd][1][c * chunk:(c + 1) * chunk, :] = root * ((1.0 + t_i) * half_u)

    ctx_blk = n_ctx // V7X_SUBLANES
    all_blk = t // V7X_SUBLANES
    nsub = V7X_SUBLANES

    def body(i, carry):
        hf, hb = carry
        rf = pl.multiple_of(i * nsub, nsub)
        jb = jnp.where(i < ctx_blk, ctx_blk - 1 - i, all_blk - 1 - (i - ctx_blk))
        rb = pl.multiple_of(jb * nsub, nsub)
        af, bf = a_f[pl.ds(rf, nsub), :], b_f[pl.ds(rf, nsub), :]
        ab, bb = a_b[pl.ds(rb, nsub), :], b_b[pl.ds(rb, nsub), :]
        rows_f, rows_b = [None] * nsub, [None] * nsub
        for s in range(nsub):
            hf = af[s:s + 1, :] * hf + bf[s:s + 1, :]
            rows_f[s] = hf
            sb = nsub - 1 - s
            hb = ab[sb:sb + 1, :] * hb + bb[sb:sb + 1, :]
            rows_b[sb] = hb
        a_f[pl.ds(rf, nsub), :] = jnp.concatenate(rows_f, axis=0)
        a_b[pl.ds(rb, nsub), :] = jnp.concatenate(rows_b, axis=0)
        return hf, hb

    h0 = jnp.zeros((1, width), F32)
    lax.fori_loop(0, all_blk, body, (h0, h0), unroll=2 if all_blk % 2 == 0 else 1)

    k0 = math.sqrt(2.0 / math.pi)
    for c in range(n_chunks):
        sl = slice(c * chunk, (c + 1) * chunk)
        y = z_ref[0, sl, width:2 * width]
        gelu = (0.5 * y) * (1.0 + jnp.tanh(y * (k0 + (k0 * 0.044715) * (y * y))))
        o_ref[0, sl, :] = ((a_f[sl, :] + a_b[sl, :]) * gelu).astype(BF16)


def _lru(zlru, layer, p, n_ctx):
    b, t, w2 = zlru.shape
    width = w2 // 2
    chunk = _tile(math.gcd(n_ctx, t), ROW_TILE)
    lay3 = lambda bb: (layer, 0, 0)
    return pl.pallas_call(
        functools.partial(_lru_kernel, n_ctx=n_ctx, chunk=chunk),
        grid=(b,),
        in_specs=[pl.BlockSpec((1, t, w2), lambda bb: (bb, 0, 0)),
                  pl.BlockSpec((1, CONV_WIDTH, width), lay3),
                  pl.BlockSpec((1, 1, width), lay3),
                  pl.BlockSpec((1,) + p["lru_wg"].shape[1:], lay3),
                  pl.BlockSpec((1, 1, 4 * width), lay3),
                  pl.BlockSpec((1, 2, width), lay3)],
        out_specs=pl.BlockSpec((1, t, width), lambda bb: (bb, 0, 0)),
        out_shape=jax.ShapeDtypeStruct((b, t, width), BF16),
        scratch_shapes=[pltpu.VMEM((t, width), F32)] * 4,
        compiler_params=_cparams(("arbitrary",), V7X_VMEM_LIMIT_BYTES),
        name="rg_lru",
    )(zlru, p["conv_w"], p["conv_b"], p["lru_wg"], p["lru_bg"], p["lru_lam"])


def _attn_kernel(q_ref, k_ref, vt_ref, o_ref, sbuf, p0, p1, *, gqa, dk, dv, n_ctx, n_lat, with_ctx, tq):
    t = n_ctx + n_lat
    pbuf = (p0, p1)
    out_off = 0 if with_ctx else n_ctx
    nt_dims = (((1,), (1,)), ((), ()))
    key_blk = math.gcd(n_ctx, ATTN_KEY_BLOCK)

    def n_sub(rows):
        return GQA_KV_HEADS * (rows // ATTN_SUB) if gqa else MLA_HEADS

    def load_q(r0, j, rows):
        if gqa:
            kvh, piece = j % GQA_KV_HEADS, j // GQA_KV_HEADS
            blk = q_ref[0, pl.ds(r0 + piece * ATTN_SUB, ATTN_SUB), kvh * GQA_GROUP * dk:(kvh + 1) * GQA_GROUP * dk]
            return jnp.concatenate([blk[:, h * dk:(h + 1) * dk] for h in range(GQA_GROUP)], axis=0), kvh
        return q_ref[0, pl.ds(r0, rows), j * dk:(j + 1) * dk], j

    def qk(r0, j, rows, n_keys, slot):
        q, kh = load_q(r0, j, rows)
        sbuf[slot, 0:n_keys, 0:q.shape[0]] = lax.dot_general(
            k_ref[0, kh, 0:n_keys, :], q, nt_dims, preferred_element_type=F32)

    def finish(r0, j, rows, n_keys, slot, par):
        cols = GQA_GROUP * ATTN_SUB if gqa else rows
        kh = j % GQA_KV_HEADS if gqa else j
        m = None
        for b0 in range(0, n_keys, key_blk):
            mb = jnp.max(sbuf[slot, b0:b0 + key_blk, 0:cols], axis=0, keepdims=True)
            m = mb if m is None else jnp.maximum(m, mb)
        for b0 in range(0, n_keys, key_blk):
            pbuf[par][b0:b0 + key_blk, 0:cols] = jnp.exp2(sbuf[slot, b0:b0 + key_blk, 0:cols] - m).astype(BF16)
        ot = jnp.dot(vt_ref[0, kh, :, 0:n_keys], pbuf[par][0:n_keys, 0:cols], preferred_element_type=F32)
        o = (ot[0:dv, :] / ot[dv:dv + 1, :]).T.astype(o_ref.dtype)
        if gqa:
            piece = j // GQA_KV_HEADS
            out = jnp.concatenate([o[h * ATTN_SUB:(h + 1) * ATTN_SUB, :] for h in range(GQA_GROUP)], axis=-1)
            o_ref[0, pl.ds(r0 - out_off + piece * ATTN_SUB, ATTN_SUB),
                  kh * GQA_GROUP * dv:(kh + 1) * GQA_GROUP * dv] = out
        else:
            o_ref[0, pl.ds(r0 - out_off, rows), j * dv:(j + 1) * dv] = o

    if with_ctx:
        rows_c = min(tq, n_ctx)
        for r in range(0, n_ctx, rows_c):
            for j in range(n_sub(rows_c)):
                qk(r, j, rows_c, n_ctx, 0)
                finish(r, j, rows_c, n_ctx, 0, 0)

    n_tiles = n_lat // tq
    ns = n_sub(tq)
    assert ns % ATTN_DEPTH == 0
    qk(n_ctx, 0, tq, t, 0)

    def body(i, carry):
        r0 = pl.multiple_of(n_ctx + i * tq, ATTN_SUB)
        r_next = pl.multiple_of(n_ctx + jnp.minimum(i + 1, n_tiles - 1) * tq, ATTN_SUB)
        for j in range(ns):
            if j + 1 < ns:
                qk(r0, j + 1, tq, t, (j + 1) % ATTN_DEPTH)
            else:
                qk(r_next, 0, tq, t, 0)
            finish(r0, j, tq, t, j % ATTN_DEPTH, j % 2)
        return carry

    lax.fori_loop(0, n_tiles, body, 0)


def _attention(q, k, vt, *, gqa, n_ctx, with_ctx, name):
    b, t, qw = q.shape
    hk, dk = k.shape[1], k.shape[-1]
    dv = MLA_V if not gqa else dk
    n_lat = t - n_ctx
    tq = _tile(n_lat, ATTN_Q_TILE)
    cols = GQA_GROUP * ATTN_SUB if gqa else tq
    t_out = t if with_ctx else n_lat
    w_out = (qw // dk) * dv
    kern = functools.partial(_attn_kernel, gqa=gqa, dk=dk, dv=dv, n_ctx=n_ctx, n_lat=n_lat,
                             with_ctx=with_ctx, tq=tq)
    return pl.pallas_call(
        kern,
        grid=(b,),
        in_specs=[pl.BlockSpec((1, t, qw), lambda bb: (bb, 0, 0)),
                  pl.BlockSpec((1, hk, t, dk), lambda bb: (bb, 0, 0, 0)),
                  pl.BlockSpec((1, hk, V_ROWS, t), lambda bb: (bb, 0, 0, 0))],
        out_specs=pl.BlockSpec((1, t_out, w_out), lambda bb: (bb, 0, 0)),
        out_shape=jax.ShapeDtypeStruct((b, t_out, w_out), BF16),
        scratch_shapes=[pltpu.VMEM((ATTN_DEPTH, t, cols), F32),
                        pltpu.VMEM((t, cols), BF16), pltpu.VMEM((t, cols), BF16)],
        compiler_params=_cparams(("arbitrary",), V7X_VMEM_LIMIT_BYTES),
        name=name,
    )(q, k, vt)


def _merge_kernel(*refs, n_x, nc, d, n_exp):
    x_refs, refs = refs[:n_x], refs[n_x:]
    (oa_ref, ol_ref, om_ref, gt_ref, m_ref, g_ref, wa_ref, wl_ref, wm_ref, wo_ref, rw_ref, rb_ref,
     xo_ref, h_ref, ids_ref, wts_ref) = refs
    tm = xo_ref.shape[1]
    parts = [slice(k * tm // MERGE_PARTS, (k + 1) * tm // MERGE_PARTS) for k in range(MERGE_PARTS)]
    x_in = _read_stream(x_refs, pl.program_id(1), nc)
    merged = []
    for r in parts:
        gates = gt_ref[0, r, :]
        merged.append(
            gates[:, 0:d].astype(F32) * jnp.dot(oa_ref[0, r, :], wa_ref[0], preferred_element_type=F32)
            + gates[:, d:2 * d].astype(F32) * jnp.dot(ol_ref[0, r, :], wl_ref[0], preferred_element_type=F32)
            + gates[:, 2 * d:3 * d].astype(F32) * jnp.dot(om_ref[0, r, :], wm_ref[0], preferred_element_type=F32))
    ys = [jnp.dot(mg.astype(BF16), wo_ref[0], preferred_element_type=F32) for mg in merged]
    his, los = [], []
    for r, y in zip(parts, ys):
        x = x_in[r, :] + m_ref[0, 0, 2:3, :] * y
        xo_ref[0, r, :] = x
        h = _rms(x) * g_ref[0]
        h = h * (1.0 + m_ref[0, 0, 4:5, :]) + m_ref[0, 0, 3:4, :]
        h_hi = h.astype(BF16)
        h_ref[0, r, :] = h_hi
        his.append(h_hi)
        los.append((h - h_hi.astype(F32)).astype(BF16))

    scores = jnp.concatenate(
        [_sigmoid(jnp.dot(jnp.concatenate([h_hi, h_lo, h_hi], axis=-1), rw_ref[...],
                          preferred_element_type=F32)).T[0:n_exp, :]
         for h_hi, h_lo in zip(his, los)], axis=-1)
    sel = scores + rb_ref[...]
    per = n_exp // N_GROUPS
    gs = []
    for g in range(N_GROUPS):
        r = [sel[g * per + j:g * per + j + 1, :] for j in range(per)]
        best = None
        for a in range(per):
            for bq in range(a + 1, per):
                pair = r[a] + r[bq]
                best = pair if best is None else jnp.maximum(best, pair)
        gs.append(best)
    gmax = functools.reduce(jnp.maximum, gs)
    gbest = jnp.full(gmax.shape, N_GROUPS - 1, jnp.int32)
    for g in range(N_GROUPS - 2, -1, -1):
        gbest = jnp.where(gs[g] == gmax, g, gbest)
    eid = lax.broadcasted_iota(jnp.int32, sel.shape, 0)
    gid = jnp.zeros(sel.shape, jnp.int32)
    for g in range(1, N_GROUPS):
        gid = gid + (eid >= g * per).astype(jnp.int32)
    masked = jnp.where(gid == gbest, sel, -jnp.inf)
    m1 = jnp.max(masked, axis=0, keepdims=True)
    i1 = jnp.min(jnp.where(masked == m1, eid, n_exp), axis=0, keepdims=True)
    masked2 = jnp.where(eid == i1, -jnp.inf, masked)
    m2 = jnp.max(masked2, axis=0, keepdims=True)
    i2 = jnp.min(jnp.where(masked2 == m2, eid, n_exp), axis=0, keepdims=True)
    s1 = jnp.sum(jnp.where(eid == i1, scores, 0.0), axis=0, keepdims=True)
    s2 = jnp.sum(jnp.where(eid == i2, scores, 0.0), axis=0, keepdims=True)
    tot = s1 + s2
    ids_ref[0] = jnp.concatenate([i1, i2], axis=0)
    wts_ref[0] = jnp.concatenate([ROUTED_SCALE * s1 / tot, ROUTED_SCALE * s2 / tot], axis=0)


def _merge(streams, o_attn, o_lru, o_mla, gates, layer, p, n_ctx, with_ctx):
    b, d = streams[0].shape[0], streams[0].shape[-1]
    t = o_lru.shape[1]
    n_exp = p["r_bias"].shape[0]
    tm = _tile(math.gcd(n_ctx, t), ROW_TILE)
    off = 0 if with_ctx else n_ctx // tm
    nc = n_ctx // tm
    tq = t if with_ctx else t - n_ctx
    nt = tq // tm
    full = lambda bb, i: (bb, i + off, 0)
    qrow = lambda bb, i: (bb, i, 0)
    const = lambda bb, i: (0, 0)
    lay3 = lambda bb, i: (layer, 0, 0)
    bw = o_attn.shape[-1]
    route_spec = pl.BlockSpec((1, 2, tm), lambda bb, i: (bb * nt + i, 0, 0))
    wspec = lambda a: pl.BlockSpec((1,) + a.shape[1:], lay3)
    in_specs = _stream_specs(streams, tm, nc, lambda bb, i: i + off, lambda bb, i: bb)
    in_specs += [pl.BlockSpec((1, tm, bw), qrow),
                 pl.BlockSpec((1, tm, o_lru.shape[-1]), full),
                 pl.BlockSpec((1, tm, bw), qrow),
                 pl.BlockSpec((1, tm, 3 * d), full),
                 pl.BlockSpec((1, 1, N_MOD, d), lambda bb, i: (layer, jnp.where(i + off < nc, 0, bb + 1), 0, 0)),
                 pl.BlockSpec((1, 1, d), lay3),
                 wspec(p["w_ba"]), wspec(p["w_bl"]), wspec(p["w_bm"]), wspec(p["w_out"]),
                 pl.BlockSpec(p["r_w"].shape, const),
                 pl.BlockSpec(p["r_bias"].shape, const)]
    return pl.pallas_call(
        functools.partial(_merge_kernel, n_x=len(streams), nc=nc - off, d=d, n_exp=n_exp),
        grid=(b, nt),
        in_specs=in_specs,
        out_specs=[pl.BlockSpec((1, tm, d), qrow), pl.BlockSpec((1, tm, d), qrow),
                   route_spec, route_spec],
        out_shape=[jax.ShapeDtypeStruct((b, tq, d), F32), jax.ShapeDtypeStruct((b, tq, d), BF16),
                   jax.ShapeDtypeStruct((b * nt, 2, tm), jnp.int32),
                   jax.ShapeDtypeStruct((b * nt, 2, tm), F32)],
        compiler_params=_cparams(("arbitrary", "arbitrary"), V7X_VMEM_LIMIT_BYTES),
        name="merge_router",
    )(*streams, o_attn, o_lru, o_mla, gates, p["mods"], p["norm_ffn"], p["w_ba"], p["w_bl"], p["w_bm"],
      p["w_out"], p["r_w"], p["r_bias"])


def _moe_kernel(h_ref, idr_ref, idc_ref, wtc_ref, low_ref, wg_ref, wu_ref, wd_ref, o_ref,
                xs, ys, cs, posc, meta, *, per, n_pairs):
    s = pl.program_id(1)
    tm = h_ref.shape[0]
    slots = xs.shape[0]
    pairs_per_group = per // 2

    @pl.when(s == 0)
    def _sort_tokens():
        e_row = idr_ref[0, 0:1, :]
        e_col = idc_ref[:, 0:1]
        g_row = jnp.zeros(e_row.shape, jnp.int32)
        g_col = jnp.zeros(e_col.shape, jnp.int32)
        for g in range(1, N_GROUPS):
            g_row = g_row + (e_row >= g * per).astype(jnp.int32)
            g_col = g_col + (e_col >= g * per).astype(jnp.int32)
        sub8 = lax.broadcasted_iota(jnp.int32, (V7X_SUBLANES, tm), 0)
        lane = lax.broadcasted_iota(jnp.int32, (tm, V7X_LANES), 1)
        onehot_row = jnp.where(sub8 == g_row, 1.0, 0.0)
        onehot_col = jnp.where(lane == g_col, 1.0, 0.0)
        rank_col = jnp.dot(low_ref[...], onehot_col.astype(BF16), preferred_element_type=F32)
        rank_row = lax.dot_general(onehot_row.astype(BF16), low_ref[...], (((1,), (1,)), ((), ())),
                                   preferred_element_type=F32)
        start = 0
        start_row = jnp.zeros((V7X_SUBLANES, 1), F32)
        start_col = jnp.zeros((1, V7X_LANES), F32)
        for g in range(N_GROUPS):
            cnt = jnp.sum((g_row == g).astype(jnp.int32))
            n_sub = lax.shift_right_logical(cnt + (V7X_LANES - 1), 7)
            meta[g] = start
            meta[N_GROUPS + g] = n_sub
            start_f = jnp.asarray(start, jnp.int32).astype(F32)
            start_row = jnp.where(lax.broadcasted_iota(jnp.int32, start_row.shape, 0) == g, start_f, start_row)
            start_col = jnp.where(lax.broadcasted_iota(jnp.int32, start_col.shape, 1) == g, start_f, start_col)
            start = start + n_sub * V7X_LANES
        pos_row = jnp.sum(onehot_row * (rank_row + start_row), axis=0, keepdims=True).astype(jnp.int32)
        pos_col = jnp.sum(onehot_col * (rank_col + start_col), axis=1, keepdims=True).astype(jnp.int32)
        posc[...] = pos_col
        slot = lax.broadcasted_iota(jnp.int32, (slots, tm), 0)
        perm = jnp.where(slot == pos_row, 1.0, 0.0).astype(BF16)
        xs[...] = jnp.dot(perm, h_ref[...], preferred_element_type=F32).astype(BF16)
        ids_c, wts_c = idc_ref[...], wtc_ref[...]
        comb = jnp.zeros((tm, V7X_LANES), F32)
        for k in range(2):
            local = jnp.bitwise_and(ids_c[:, k:k + 1], per - 1)
            comb = comb + jnp.where(lane == local, wts_c[:, k:k + 1], 0.0)
        c_hi = comb.astype(BF16)
        c_lo = (comb - c_hi.astype(F32)).astype(BF16)
        both = jnp.dot(perm, jnp.concatenate([c_hi, c_lo], axis=-1), preferred_element_type=F32)
        cs[...] = both[:, :V7X_LANES] + both[:, V7X_LANES:]
        ys[...] = jnp.zeros_like(ys)

    grp = s // pairs_per_group
    first_local = (s % pairs_per_group) * 2
    seg_start = meta[grp]
    lane_sub = lax.broadcasted_iota(jnp.int32, (V7X_LANES, V7X_LANES), 1)

    def sub_tile(k, carry):
        r0 = pl.multiple_of(seg_start + k * V7X_LANES, V7X_LANES)
        x = xs[pl.ds(r0, V7X_LANES), :]
        c_all = cs[pl.ds(r0, V7X_LANES), :]
        acts = []
        for jj in range(2):
            c = jnp.sum(jnp.where(lane_sub == first_local + jj, c_all, 0.0), axis=-1, keepdims=True)
            g = jnp.dot(x, wg_ref[0, jj], preferred_element_type=F32)
            u = jnp.dot(x, wu_ref[0, jj], preferred_element_type=F32)
            acts.append(((g * _sigmoid(g)) * u * c).astype(BF16))
        w_down = wd_ref[0].reshape(2 * wd_ref.shape[2], wd_ref.shape[3])
        acc = jnp.dot(jnp.concatenate(acts, axis=-1), w_down, preferred_element_type=F32)
        ys[pl.ds(r0, V7X_LANES), :] = ys[pl.ds(r0, V7X_LANES), :] + acc
        return carry

    lax.fori_loop(0, meta[N_GROUPS + grp], sub_tile, 0)

    @pl.when(s == n_pairs - 1)
    def _unsort():
        slot_l = lax.broadcasted_iota(jnp.int32, (tm, slots), 1)
        perm_t = jnp.where(slot_l == posc[...], 1.0, 0.0).astype(BF16)
        y = ys[...]
        y_hi = y.astype(BF16)
        y_lo = (y - y_hi.astype(F32)).astype(BF16)
        o_ref[...] = (jnp.dot(perm_t, y_hi, preferred_element_type=F32)
                      + jnp.dot(perm_t, y_lo, preferred_element_type=F32))


def _moe(h, ids, wts, layer, p):
    n, d = h.shape
    _, n_exp, _, ff = p["moe_wg"].shape
    per = n_exp // N_GROUPS
    assert per % 2 == 0 and per & (per - 1) == 0
    rt = ids.shape[-1]
    tm = _tile(n, MOE_ROW_TILE)
    assert tm % rt == 0 and tm % V7X_LANES == 0
    slots = tm + N_GROUPS * V7X_LANES
    ids_row = ids.reshape(n // tm, tm // rt, 2, rt).transpose(0, 2, 1, 3).reshape(n // tm, 2, tm)
    ids_col = jnp.transpose(ids, (0, 2, 1)).reshape(n, 2)
    wts_col = jnp.transpose(wts, (0, 2, 1)).reshape(n, 2)
    low = jnp.tril(jnp.ones((tm, tm), BF16), -1)
    n_pairs = n_exp // 2
    wspec = lambda a, b: pl.BlockSpec((1, 2, a, b), lambda i, s: (layer, s, 0, 0))
    return pl.pallas_call(
        functools.partial(_moe_kernel, per=per, n_pairs=n_pairs),
        grid=(n // tm, n_pairs),
        in_specs=[pl.BlockSpec((tm, d), lambda i, s: (i, 0)),
                  pl.BlockSpec((1, 2, tm), lambda i, s: (i, 0, 0)),
                  pl.BlockSpec((tm, 2), lambda i, s: (i, 0)),
                  pl.BlockSpec((tm, 2), lambda i, s: (i, 0)),
                  pl.BlockSpec((tm, tm), lambda i, s: (0, 0), pipeline_mode=pl.Buffered(1)),
                  wspec(d, ff), wspec(d, ff), wspec(ff, d)],
        out_specs=pl.BlockSpec((tm, d), lambda i, s: (i, 0)),
        out_shape=jax.ShapeDtypeStruct((n, d), F32),
        scratch_shapes=[pltpu.VMEM((slots, d), BF16), pltpu.VMEM((slots, d), F32),
                        pltpu.VMEM((slots, V7X_LANES), F32), pltpu.VMEM((tm, 1), jnp.int32),
                        pltpu.SMEM((2 * N_GROUPS,), jnp.int32)],
        compiler_params=_cparams(("arbitrary", "arbitrary"), V7X_VMEM_LIMIT_BYTES),
        name="moe_ffn",
    )(h, ids_row, ids_col, wts_col, low, p["moe_wg"], p["moe_wu"], p["moe_wd"])


def _final_kernel(x_ref, f_ref, m_ref, g_ref, o_ref):
    x = x_ref[0] + m_ref[0, 0, 5:6, :] * f_ref[0]
    o_ref[0] = _rms(x) * g_ref[...]


def _final(x, f, mods, layer, g):
    b, s, d = x.shape
    tm = _tile(s, 2 * ROW_TILE)
    row = lambda bb, i: (bb, i, 0)
    return pl.pallas_call(
        _final_kernel,
        grid=(b, s // tm),
        in_specs=[pl.BlockSpec((1, tm, d), row), pl.BlockSpec((1, tm, d), row),
                  pl.BlockSpec((1, 1, N_MOD, d), lambda bb, i: (layer, bb + 1, 0, 0)),
                  pl.BlockSpec((1, d), lambda bb, i: (0, 0))],
        out_specs=pl.BlockSpec((1, tm, d), row),
        out_shape=jax.ShapeDtypeStruct((b, s, d), F32),
        compiler_params=_cparams(("arbitrary", "arbitrary")),
        name="final_norm",
    )(x, f, mods, g.reshape(1, d))


def _rope_tables(n_ctx, n_lat, dim, period, lane_off, width):
    quarter = dim // 4
    pos = jnp.arange(n_lat, dtype=F32)
    r, col = jnp.floor(pos / GRID_W), pos - GRID_W * jnp.floor(pos / GRID_W)
    inv_freq = ROPE_THETA ** (-jnp.arange(quarter, dtype=F32) / quarter)
    lane = np.arange(width)
    j = (lane % period) - lane_off
    active = (j >= 0) & (j < dim)
    jj = np.where(active, j, 0)
    use_col = jj >= dim // 2
    upper = (jj % (dim // 2)) >= quarter
    f = jj % quarter
    ang = jnp.where(use_col[None, :], col[:, None], r[:, None]) * inv_freq[f][None, :]
    act = jnp.asarray(active)[None, :]
    cos = jnp.where(act, jnp.cos(ang), 1.0)
    sin = jnp.where(act, jnp.sin(ang), 0.0)
    sin_up = jnp.where(jnp.asarray(~upper)[None, :], -sin, 0.0)
    sin_dn = jnp.where(jnp.asarray(upper)[None, :], sin, 0.0)
    ident = lambda v, fill: jnp.concatenate([jnp.full((n_ctx, width), fill, F32), v], axis=0)
    return ident(cos, 1.0), ident(sin_up, 0.0), ident(sin_dn, 0.0)


def _lru_gate_weights(w_a, w_i):
    depth, _, nb, k, j = w_a.shape
    w = jnp.stack([w_a, w_i], axis=2)
    eye = jnp.eye(nb, dtype=w.dtype)
    dense = jnp.einsum('dsgnkj,nm->dnksgmj', w, eye)
    return (0.5 * dense).reshape(depth, nb * k, 4 * nb * j).astype(BF16)


def kernel(x, c, ctx, c_ctx, w_mod, b_mod, norm_mix, norm_ffn, w_in, gqa_q_norm, gqa_k_norm, conv_w, conv_b,
           lru_w_a, lru_b_a, lru_w_i, lru_b_i, lru_lam, mla_q_a_norm, mla_w_qb, mla_kv_a_norm, mla_w_kvb,
           w_branch_attn, w_branch_lru, w_branch_mla, w_out, router_w, router_bias,
           moe_w_gate, moe_w_up, moe_w_down, final_norm):
    bsz, n_lat, d = x.shape
    n_ctx = ctx.shape[1]
    depth = w_mod.shape[0]
    hd = gqa_q_norm.shape[-1]
    lru_w = conv_w.shape[-1]
    q_lora, kv_lora = mla_q_a_norm.shape[-1], mla_kv_a_norm.shape[-1]
    n_exp = router_w.shape[-1]
    assert bsz + 1 <= MOD_ROWS and d % V7X_LANES == 0

    cc = jnp.zeros((MOD_ROWS, d), F32).at[0].set(c_ctx).at[1:1 + bsz].set(c)
    mods = _modulation(cc, w_mod, b_mod).reshape(depth, MOD_ROWS, N_MOD, d)

    mq_period = MLA_QK * V7X_LANES // math.gcd(MLA_QK, V7X_LANES)
    kr_end = GQA_HEADS * hd + 2 * GQA_KV_HEADS * hd + 2 * lru_w + q_lora + kv_lora + MLA_ROPE
    w_in_b = _pack_w_in(w_in, kr_end)
    r_pad = jnp.pad(router_w, ((0, 0), (0, V7X_LANES - n_exp)))
    r_hi = r_pad.astype(BF16)
    r_w = jnp.concatenate([r_hi, r_hi, (r_pad - r_hi.astype(F32)).astype(BF16)], axis=0)
    p = dict(
        hd=hd, lru_w=lru_w, q_lora=q_lora, kv_lora=kv_lora, mods=mods,
        norm_mix=norm_mix.reshape(depth, 1, d), norm_ffn=norm_ffn.reshape(depth, 1, d), w_in=w_in_b,
        gq=jnp.tile(gqa_q_norm, (1, GQA_HEADS)).reshape(depth, 1, -1),
        gk=jnp.tile(gqa_k_norm, (1, GQA_KV_HEADS)).reshape(depth, 1, -1),
        gqa_tabs=_rope_tables(n_ctx, n_lat, hd, hd, 0, V7X_LANES),
        mla_q_tabs=_rope_tables(n_ctx, n_lat, MLA_ROPE, MLA_QK, MLA_NOPE, mq_period),
        mla_k_tabs=_rope_tables(n_ctx, n_lat, MLA_ROPE, V7X_LANES, 0, V7X_LANES),
        mla_q_norm=mla_q_a_norm.reshape(depth, 1, q_lora), mla_kv_norm=mla_kv_a_norm.reshape(depth, 1, kv_lora),
        w_qb=mla_w_qb.astype(BF16), w_kvb=mla_w_kvb.astype(BF16),
        conv_w=conv_w, conv_b=conv_b.reshape(depth, 1, lru_w),
        lru_wg=_lru_gate_weights(lru_w_a, lru_w_i),
        lru_bg=0.5 * jnp.stack([lru_b_a, lru_b_i], axis=2).reshape(depth, 1, 4 * lru_w),
        lru_lam=lru_lam,
        w_ba=w_branch_attn.astype(BF16), w_bl=w_branch_lru.astype(BF16), w_bm=w_branch_mla.astype(BF16),
        w_out=w_out.astype(BF16), r_w=r_w,
        r_bias=router_bias.reshape(n_exp, 1).astype(F32),
        moe_wg=moe_w_gate.astype(BF16), moe_wu=moe_w_up.astype(BF16), moe_wd=moe_w_down.astype(BF16),
    )

    streams = (ctx, x)
    prev = None
    for layer in range(depth):
        last = layer == depth - 1
        streams, (q_g, k_g, v_g, zlru, q_m, k_m, v_m, gates) = _in_proj(streams, prev, layer, p, n_ctx)
        o_attn = _attention(q_g, k_g, v_g, gqa=True, n_ctx=n_ctx, with_ctx=not last, name="gqa_attention")
        o_lru = _lru(zlru, layer, p, n_ctx)
        o_mla = _attention(q_m, k_m, v_m, gqa=False, n_ctx=n_ctx, with_ctx=not last, name="mla_attention")
        x_mid, h2, ids, wts = _merge(streams, o_attn, o_lru, o_mla, gates, layer, p, n_ctx, with_ctx=not last)

        n_tok = x_mid.shape[0] * x_mid.shape[1]
        f = _moe(h2.reshape(n_tok, d), ids, wts, layer, p).reshape(x_mid.shape)
        streams, prev = (x_mid,), (f, layer)

    return _final(streams[0], prev[0], mods, prev[1], final_norm)
```

```python
import functools
import math

import numpy as np
import jax
import jax.numpy as jnp
from jax import lax
from jax.experimental import pallas as pl
from jax.experimental.pallas import tpu as pltpu

F32 = jnp.float32
BF16 = jnp.bfloat16

GRID_W = 64
ROPE_THETA = 10000.0
NORM_EPS = 1e-6
N_MOD = 6
GQA_HEADS = 8
GQA_KV_HEADS = 2
GQA_GROUP = GQA_HEADS // GQA_KV_HEADS
MLA_HEADS = 8
MLA_NOPE = 64
MLA_ROPE = 32
MLA_V = 64
MLA_QK = MLA_NOPE + MLA_ROPE
CONV_WIDTH = 4
LRU_C = 8.0
N_GROUPS = 4
ROUTED_SCALE = 1.0
LOG2E = math.log2(math.e)

V7X_LANES = 128
V7X_SUBLANES = 8
V7X_VMEM_LIMIT_BYTES = 56 * 1024 * 1024

ROW_TILE = 256
ATTN_Q_TILE = 512
ATTN_SUB = 128
ATTN_DEPTH = 2
ATTN_KEY_BLOCK = 256
V_ROWS = 80
MERGE_PARTS = 2
MOE_ROW_TILE = 1024
MOD_ROWS = 16


def _cparams(sem, vmem=None):
    return pltpu.CompilerParams(dimension_semantics=sem, vmem_limit_bytes=vmem)


def _tile(n, pref):
    t = min(n, pref)
    while n % t or t % V7X_SUBLANES:
        t -= 1
    return t


def _sigmoid(x):
    return 0.5 * (1.0 + jnp.tanh(0.5 * x))


def _rms(x):
    return x * lax.rsqrt(jnp.mean(x * x, axis=-1, keepdims=True) + NORM_EPS)


def _tiled(tbl, width):
    reps = width // tbl.shape[-1]
    return tbl if reps == 1 else jnp.concatenate([tbl] * reps, axis=-1)


def _rope_lanes(x, cos, sin_up, sin_dn, half):
    outs = []
    for c in range(x.shape[-1] // V7X_LANES):
        sl = slice(c * V7X_LANES, (c + 1) * V7X_LANES)
        xc = x[:, sl]
        up = pltpu.roll(xc, V7X_LANES - half, 1)
        dn = pltpu.roll(xc, half, 1)
        outs.append(xc * cos[:, sl] + up * sin_up[:, sl] + dn * sin_dn[:, sl])
    return outs[0] if len(outs) == 1 else jnp.concatenate(outs, axis=-1)


def _ones_row_tail(rows, cols):
    r = lax.broadcasted_iota(jnp.int32, (rows, cols), 0)
    return jnp.where(r == 0, 1.0, 0.0).astype(BF16)


def _stream_specs(streams, tm, nc, tile_of, batch_of):
    d = streams[0].shape[-1]
    if len(streams) == 1:
        return [pl.BlockSpec((1, tm, d), lambda *g: (batch_of(*g), tile_of(*g), 0))]
    return [pl.BlockSpec((1, tm, d), lambda *g: (batch_of(*g), jnp.minimum(tile_of(*g), nc - 1), 0)),
            pl.BlockSpec((1, tm, d), lambda *g: (batch_of(*g), jnp.maximum(tile_of(*g) - nc, 0), 0))]


def _read_stream(refs, tile, nc):
    if len(refs) == 1:
        return refs[0][0]
    return jnp.where(tile < nc, refs[0][0], refs[1][0])


def _mod_kernel(c_ref, w_ref, b_ref, o_ref):
    c = c_ref[...]
    a = (c * _sigmoid(c)).astype(BF16)
    o_ref[0] = jnp.dot(a, w_ref[0].astype(BF16), preferred_element_type=F32) + b_ref[0]


def _modulation(cc, w_mod, b_mod):
    depth, d, n = w_mod.shape
    tn = _tile(n, 1536) if n % V7X_LANES == 0 else n
    return pl.pallas_call(
        _mod_kernel,
        grid=(depth, n // tn),
        in_specs=[pl.BlockSpec((MOD_ROWS, d), lambda l, j: (0, 0)),
                  pl.BlockSpec((1, d, tn), lambda l, j: (l, 0, j)),
                  pl.BlockSpec((1, 1, tn), lambda l, j: (l, 0, j))],
        out_specs=pl.BlockSpec((1, MOD_ROWS, tn), lambda l, j: (l, 0, j)),
        out_shape=jax.ShapeDtypeStruct((depth, MOD_ROWS, n), F32),
        compiler_params=_cparams(("arbitrary", "arbitrary"), V7X_VMEM_LIMIT_BYTES),
        name="modulation",
    )(cc, w_mod, b_mod.reshape(depth, 1, n))


def _pack_w_in_kernel(w_ref, o_ref, *, kr_end):
    w = w_ref[0].astype(BF16)
    pad = jnp.zeros((w.shape[0], o_ref.shape[-1] - w.shape[-1]), BF16)
    o_ref[0] = jnp.concatenate([w[:, :kr_end], pad, w[:, kr_end:]], axis=-1)


def _pack_w_in(w_in, kr_end):
    depth, d, n = w_in.shape
    n_out = n + V7X_LANES - MLA_ROPE
    rt = _tile(d, V7X_LANES)
    return pl.pallas_call(
        functools.partial(_pack_w_in_kernel, kr_end=kr_end),
        grid=(depth, d // rt),
        in_specs=[pl.BlockSpec((1, rt, n), lambda l, i: (l, i, 0))],
        out_specs=pl.BlockSpec((1, rt, n_out), lambda l, i: (l, i, 0)),
        out_shape=jax.ShapeDtypeStruct((depth, d, n_out), BF16),
        compiler_params=_cparams(("arbitrary", "arbitrary")),
        name="pack_w_in",
    )(w_in)


def _in_proj_kernel(*refs, n_x, has_prev, nc, hd, lru_w, q_lora, kv_lora, d):
    tile = pl.program_id(0)
    x_refs, refs = refs[:n_x], refs[n_x:]
    if has_prev:
        f_ref, pm_ref, refs = refs[0], refs[1], refs[2:]
    (g_ref, m_ref, w_ref, gq_ref, gk_ref, gc_ref, gu_ref, gd_ref, nqa_ref, nkv_ref, wq_ref, wkv_ref,
     qc_ref, qu_ref, qd_ref, kc_ref, ku_ref, kd_ref) = refs[:18]
    outs = refs[18:]
    x = _read_stream(x_refs, tile, nc)
    if has_prev:
        x = x + pm_ref[0, 0, 5:6, :] * f_ref[0]
        outs[0][0] = x
        outs = outs[1:]
    qg_ref, kg_ref, vg_ref, lru_ref, qm_ref, km_ref, vm_ref, gate_ref = outs
    tm = x.shape[0]
    h = _rms(x) * g_ref[0]
    h = h * (1.0 + m_ref[0, 0, 1:2, :]) + m_ref[0, 0, 0:1, :]
    hb = h.astype(BF16)

    nq, nk = GQA_HEADS * hd, GQA_KV_HEADS * hd
    w_gqa = nq + 2 * nk
    w_mla = q_lora + kv_lora + V7X_LANES
    o_lru, o_mla, o_gate = w_gqa, w_gqa + 2 * lru_w, w_gqa + 2 * lru_w + w_mla

    z = jnp.dot(hb, w_ref[0, :, 0:w_gqa], preferred_element_type=F32)

    def head_norm(a, n_heads):
        return jnp.concatenate([_rms(a[:, i * hd:(i + 1) * hd]) for i in range(n_heads)], axis=-1)

    cos, su, sd = gc_ref[...], gu_ref[...], gd_ref[...]
    q = head_norm(z[:, :nq], GQA_HEADS) * gq_ref[0]
    q = _rope_lanes(q, _tiled(cos, nq), _tiled(su, nq), _tiled(sd, nq), hd // 4)
    qg_ref[0] = (q * (LOG2E * hd ** -0.5)).astype(BF16)
    k = head_norm(z[:, nq:nq + nk], GQA_KV_HEADS) * gk_ref[0]
    k = _rope_lanes(k, _tiled(cos, nk), _tiled(su, nk), _tiled(sd, nk), hd // 4)
    v_t = z[:, nq + nk:nq + 2 * nk].T
    tail = _ones_row_tail(V_ROWS - hd, tm)
    for i in range(GQA_KV_HEADS):
        kg_ref[0, i] = k[:, i * hd:(i + 1) * hd].astype(BF16)
        vg_ref[0, i, 0:hd, :] = v_t[i * hd:(i + 1) * hd, :].astype(BF16)
        vg_ref[0, i, hd:V_ROWS, :] = tail

    lru_ref[0] = jnp.dot(hb, w_ref[0, :, o_lru:o_lru + 2 * lru_w], preferred_element_type=F32)

    z = jnp.dot(hb, w_ref[0, :, o_mla:o_mla + w_mla], preferred_element_type=F32)
    cq = (_rms(z[:, :q_lora]) * nqa_ref[0]).astype(BF16)
    qm = jnp.dot(cq, wq_ref[0], preferred_element_type=F32)
    wq = qm.shape[-1]
    qm = _rope_lanes(qm, _tiled(qc_ref[...], wq), _tiled(qu_ref[...], wq), _tiled(qd_ref[...], wq),
                     MLA_ROPE // 4)
    qm_ref[0] = (qm * (LOG2E * MLA_QK ** -0.5)).astype(BF16)
    ckv = (_rms(z[:, q_lora:q_lora + kv_lora]) * nkv_ref[0]).astype(BF16)
    kv = jnp.dot(ckv, wkv_ref[0], preferred_element_type=F32)
    kr = _rope_lanes(z[:, q_lora + kv_lora:], kc_ref[...], ku_ref[...], kd_ref[...], MLA_ROPE // 4)
    kr = kr[:, :MLA_ROPE]
    per = MLA_NOPE + MLA_V
    tail = _ones_row_tail(V_ROWS - MLA_V, tm)
    for i in range(MLA_HEADS):
        km_ref[0, i] = jnp.concatenate([kv[:, i * per:i * per + MLA_NOPE], kr], axis=-1).astype(BF16)
        head_t = kv[:, i * per:(i + 1) * per].T
        vm_ref[0, i, 0:MLA_V, :] = head_t[MLA_NOPE:per, :].astype(BF16)
        vm_ref[0, i, MLA_V:V_ROWS, :] = tail

    gate_ref[0] = _sigmoid(jnp.dot(hb, w_ref[0, :, o_gate:o_gate + 3 * d],
                                   preferred_element_type=F32)).astype(BF16)


def _in_proj(streams, prev, layer, p, n_ctx):
    b, d = streams[0].shape[0], streams[0].shape[-1]
    t = sum(s.shape[1] for s in streams) if len(streams) == 2 else streams[0].shape[1]
    tm = _tile(math.gcd(n_ctx, t), ROW_TILE)
    nc = n_ctx // tm
    hd, lru_w, q_lora, kv_lora = p["hd"], p["lru_w"], p["q_lora"], p["kv_lora"]
    nq, nk = GQA_HEADS * hd, GQA_KV_HEADS * hd
    qw = MLA_HEADS * MLA_QK
    tile_of, batch_of = (lambda i, bb: i), (lambda i, bb: bb)
    row = lambda i, bb: (bb, i, 0)
    mod_row = lambda i, bb: jnp.where(i < nc, 0, bb + 1)
    lay3 = lambda i, bb: (layer, 0, 0)
    tbl = lambda a: pl.BlockSpec((tm, a.shape[-1]), lambda i, bb: (i, 0))
    x_spec = pl.BlockSpec((1, tm, d), row)

    in_specs = _stream_specs(streams, tm, nc, tile_of, batch_of)
    args = list(streams)
    out_specs, out_shape = [], []
    if prev is not None:
        in_specs += [x_spec, pl.BlockSpec((1, 1, N_MOD, d), lambda i, bb: (prev[1], mod_row(i, bb), 0, 0))]
        args += [prev[0], p["mods"]]
        out_specs.append(x_spec)
        out_shape.append(jax.ShapeDtypeStruct((b, t, d), F32))
    in_specs += [pl.BlockSpec((1, 1, d), lay3),
                 pl.BlockSpec((1, 1, N_MOD, d), lambda i, bb: (layer, mod_row(i, bb), 0, 0)),
                 pl.BlockSpec((1,) + p["w_in"].shape[1:], lay3, pipeline_mode=pl.Buffered(1)),
                 pl.BlockSpec((1, 1, nq), lay3), pl.BlockSpec((1, 1, nk), lay3)]
    args += [p["norm_mix"], p["mods"], p["w_in"], p["gq"], p["gk"]]
    in_specs += [tbl(a) for a in p["gqa_tabs"]]
    args += list(p["gqa_tabs"])
    in_specs += [pl.BlockSpec((1, 1, q_lora), lay3), pl.BlockSpec((1, 1, kv_lora), lay3),
                 pl.BlockSpec((1,) + p["w_qb"].shape[1:], lay3),
                 pl.BlockSpec((1,) + p["w_kvb"].shape[1:], lay3)]
    args += [p["mla_q_norm"], p["mla_kv_norm"], p["w_qb"], p["w_kvb"]]
    in_specs += [tbl(a) for a in p["mla_q_tabs"]] + [tbl(a) for a in p["mla_k_tabs"]]
    args += list(p["mla_q_tabs"]) + list(p["mla_k_tabs"])

    def kv_specs(heads, dk):
        return [pl.BlockSpec((1, heads, tm, dk), lambda i, bb: (bb, 0, i, 0)),
                pl.BlockSpec((1, heads, V_ROWS, tm), lambda i, bb: (bb, 0, 0, i))]

    out_specs += ([pl.BlockSpec((1, tm, nq), row)] + kv_specs(GQA_KV_HEADS, hd)
                  + [pl.BlockSpec((1, tm, 2 * lru_w), row), pl.BlockSpec((1, tm, qw), row)]
                  + kv_specs(MLA_HEADS, MLA_QK) + [pl.BlockSpec((1, tm, 3 * d), row)])
    out_shape += [jax.ShapeDtypeStruct((b, t, nq), BF16),
                  jax.ShapeDtypeStruct((b, GQA_KV_HEADS, t, hd), BF16),
                  jax.ShapeDtypeStruct((b, GQA_KV_HEADS, V_ROWS, t), BF16),
                  jax.ShapeDtypeStruct((b, t, 2 * lru_w), F32),
                  jax.ShapeDtypeStruct((b, t, qw), BF16),
                  jax.ShapeDtypeStruct((b, MLA_HEADS, t, MLA_QK), BF16),
                  jax.ShapeDtypeStruct((b, MLA_HEADS, V_ROWS, t), BF16),
                  jax.ShapeDtypeStruct((b, t, 3 * d), BF16)]
    outs = pl.pallas_call(
        functools.partial(_in_proj_kernel, n_x=len(streams), has_prev=prev is not None, nc=nc, hd=hd,
                          lru_w=lru_w, q_lora=q_lora, kv_lora=kv_lora, d=d),
        grid=(t // tm, b),
        in_specs=in_specs, out_specs=out_specs, out_shape=out_shape,
        compiler_params=_cparams(("arbitrary", "arbitrary"), V7X_VMEM_LIMIT_BYTES),
        name="in_proj",
    )(*args)
    if prev is not None:
        return (outs[0],), outs[1:]
    return streams, outs


def _lru_kernel(z_ref, cw_ref, cb_ref, wg_ref, bg_ref, lam_ref, o_ref, a_f, b_f, a_b, b_b, *, n_ctx, chunk):
    t, width = a_f.shape
    n_chunks = t // chunk
    row = lax.broadcasted_iota(jnp.int32, (chunk, 1), 0)

    def conv_chunk(c):
        r0 = c * chunk
        seg_lo, seg_hi = (0, n_ctx) if r0 < n_ctx else (n_ctx, t)
        u = jnp.zeros((chunk, width), F32) + cb_ref[0]
        for j in range(CONV_WIDTH):
            off = j - 1
            lo = min(max(r0 + off, 0), t - chunk)
            tap = z_ref[0, lo:lo + chunk, 0:width]
            shift = (lo - (r0 + off)) % chunk
            if shift:
                tap = pltpu.roll(tap, shift, 0)
            if r0 + off < seg_lo or r0 + off + chunk > seg_hi:
                pos = row + (r0 + off)
                tap = jnp.where((pos >= seg_lo) & (pos < seg_hi), tap, 0.0)
            u = u + tap * cw_ref[0, j:j + 1, :]
        return u

    scr = ((a_f, b_f), (a_b, b_b))
    half_c_sp = []
    for dd in range(2):
        lam = lam_ref[0, dd:dd + 1, :]
        sp = jnp.maximum(-lam, 0.0) + jnp.log1p(jnp.exp(-jnp.abs(lam)))
        half_c_sp.append((-0.5 * LRU_C) * sp)
    for c in range(n_chunks):
        u = conv_chunk(c)
        g = jnp.dot(u.astype(BF16), wg_ref[0], preferred_element_type=F32) + bg_ref[0]
        half_u = 0.5 * u
        for dd in range(2):
            base = 2 * dd * width
            t_r = jnp.tanh(g[:, base:base + width])
            t_i = jnp.tanh(g[:, base + width:base + 2 * width])
            log_a = (1.0 + t_r) * half_c_sp[dd]
            th = jnp.tanh(log_a)
            v = (-2.0 * th) / (1.0 - th)
            root = jnp.where(v > 0.0, v * lax.rsqrt(v), 0.0)
            scr[dd][0][c * chunk:(c + 1) * chunk, :] = jnp.exp(log_a)
            scr[dd][1][c * chunk:(c + 1) * chunk, :] = root * ((1.0 + t_i) * half_u)

    ctx_blk = n_ctx // V7X_SUBLANES
    all_blk = t // V7X_SUBLANES
    nsub = V7X_SUBLANES

    def body(i, carry):
        hf, hb = carry
        rf = pl.multiple_of(i * nsub, nsub)
        jb = jnp.where(i < ctx_blk, ctx_blk - 1 - i, all_blk - 1 - (i - ctx_blk))
        rb = pl.multiple_of(jb * nsub, nsub)
        af, bf = a_f[pl.ds(rf, nsub), :], b_f[pl.ds(rf, nsub), :]
        ab, bb = a_b[pl.ds(rb, nsub), :], b_b[pl.ds(rb, nsub), :]
        rows_f, rows_b = [None] * nsub, [None] * nsub
        for s in range(nsub):
            hf = af[s:s + 1, :] * hf + bf[s:s + 1, :]
            rows_f[s] = hf
            sb = nsub - 1 - s
            hb = ab[sb:sb + 1, :] * hb + bb[sb:sb + 1, :]
            rows_b[sb] = hb
        a_f[pl.ds(rf, nsub), :] = jnp.concatenate(rows_f, axis=0)
        a_b[pl.ds(rb, nsub), :] = jnp.concatenate(rows_b, axis=0)
        return hf, hb

    h0 = jnp.zeros((1, width), F32)
    lax.fori_loop(0, all_blk, body, (h0, h0), unroll=2 if all_blk % 2 == 0 else 1)

    k0 = math.sqrt(2.0 / math.pi)
    for c in range(n_chunks):
        sl = slice(c * chunk, (c + 1) * chunk)
        y = z_ref[0, sl, width:2 * width]
        gelu = (0.5 * y) * (1.0 + jnp.tanh(y * (k0 + (k0 * 0.044715) * (y * y))))
        o_ref[0, sl, :] = ((a_f[sl, :] + a_b[sl, :]) * gelu).astype(BF16)


def _lru(zlru, layer, p, n_ctx):
    b, t, w2 = zlru.shape
    width = w2 // 2
    chunk = _tile(math.gcd(n_ctx, t), ROW_TILE)
    lay3 = lambda bb: (layer, 0, 0)
    return pl.pallas_call(
        functools.partial(_lru_kernel, n_ctx=n_ctx, chunk=chunk),
        grid=(b,),
        in_specs=[pl.BlockSpec((1, t, w2), lambda bb: (bb, 0, 0)),
                  pl.BlockSpec((1, CONV_WIDTH, width), lay3),
                  pl.BlockSpec((1, 1, width), lay3),
                  pl.BlockSpec((1,) + p["lru_wg"].shape[1:], lay3),
                  pl.BlockSpec((1, 1, 4 * width), lay3),
                  pl.BlockSpec((1, 2, width), lay3)],
        out_specs=pl.BlockSpec((1, t, width), lambda bb: (bb, 0, 0)),
        out_shape=jax.ShapeDtypeStruct((b, t, width), BF16),
        scratch_shapes=[pltpu.VMEM((t, width), F32)] * 4,
        compiler_params=_cparams(("arbitrary",), V7X_VMEM_LIMIT_BYTES),
        name="rg_lru",
    )(zlru, p["conv_w"], p["conv_b"], p["lru_wg"], p["lru_bg"], p["lru_lam"])


def _attn_kernel(q_ref, k_ref, vt_ref, o_ref, sbuf, p0, p1, *, gqa, dk, dv, n_ctx, n_lat, with_ctx, tq):
    t = n_ctx + n_lat
    pbuf = (p0, p1)
    out_off = 0 if with_ctx else n_ctx
    nt_dims = (((1,), (1,)), ((), ()))
    key_blk = math.gcd(n_ctx, ATTN_KEY_BLOCK)

    def n_sub(rows):
        return GQA_KV_HEADS * (rows // ATTN_SUB) if gqa else MLA_HEADS

    def load_q(r0, j, rows):
        if gqa:
            kvh, piece = j % GQA_KV_HEADS, j // GQA_KV_HEADS
            blk = q_ref[0, pl.ds(r0 + piece * ATTN_SUB, ATTN_SUB), kvh * GQA_GROUP * dk:(kvh + 1) * GQA_GROUP * dk]
            return jnp.concatenate([blk[:, h * dk:(h + 1) * dk] for h in range(GQA_GROUP)], axis=0), kvh
        return q_ref[0, pl.ds(r0, rows), j * dk:(j + 1) * dk], j

    def qk(r0, j, rows, n_keys, slot):
        q, kh = load_q(r0, j, rows)
        sbuf[slot, 0:n_keys, 0:q.shape[0]] = lax.dot_general(
            k_ref[0, kh, 0:n_keys, :], q, nt_dims, preferred_element_type=F32)

    def finish(r0, j, rows, n_keys, slot, par):
        cols = GQA_GROUP * ATTN_SUB if gqa else rows
        kh = j % GQA_KV_HEADS if gqa else j
        m = None
        for b0 in range(0, n_keys, key_blk):
            mb = jnp.max(sbuf[slot, b0:b0 + key_blk, 0:cols], axis=0, keepdims=True)
            m = mb if m is None else jnp.maximum(m, mb)
        for b0 in range(0, n_keys, key_blk):
            pbuf[par][b0:b0 + key_blk, 0:cols] = jnp.exp2(sbuf[slot, b0:b0 + key_blk, 0:cols] - m).astype(BF16)
        ot = jnp.dot(vt_ref[0, kh, :, 0:n_keys], pbuf[par][0:n_keys, 0:cols], preferred_element_type=F32)
        o = (ot[0:dv, :] / ot[dv:dv + 1, :]).T.astype(o_ref.dtype)
        if gqa:
            piece = j // GQA_KV_HEADS
            out = jnp.concatenate([o[h * ATTN_SUB:(h + 1) * ATTN_SUB, :] for h in range(GQA_GROUP)], axis=-1)
            o_ref[0, pl.ds(r0 - out_off + piece * ATTN_SUB, ATTN_SUB),
                  kh * GQA_GROUP * dv:(kh + 1) * GQA_GROUP * dv] = out
        else:
            o_ref[0, pl.ds(r0 - out_off, rows), j * dv:(j + 1) * dv] = o

    if with_ctx:
        rows_c = min(tq, n_ctx)
        for r in range(0, n_ctx, rows_c):
            for j in range(n_sub(rows_c)):
                qk(r, j, rows_c, n_ctx, 0)
                finish(r, j, rows_c, n_ctx, 0, 0)

    n_tiles = n_lat // tq
    ns = n_sub(tq)
    assert ns % ATTN_DEPTH == 0
    qk(n_ctx, 0, tq, t, 0)

    def body(i, carry):
        r0 = pl.multiple_of(n_ctx + i * tq, ATTN_SUB)
        r_next = pl.multiple_of(n_ctx + jnp.minimum(i + 1, n_tiles - 1) * tq, ATTN_SUB)
        for j in range(ns):
            if j + 1 < ns:
                qk(r0, j + 1, tq, t, (j + 1) % ATTN_DEPTH)
            else:
                qk(r_next, 0, tq, t, 0)
            finish(r0, j, tq, t, j % ATTN_DEPTH, j % 2)
        return carry

    lax.fori_loop(0, n_tiles, body, 0)


def _attention(q, k, vt, *, gqa, n_ctx, with_ctx, name):
    b, t, qw = q.shape
    hk, dk = k.shape[1], k.shape[-1]
    dv = MLA_V if not gqa else dk
    n_lat = t - n_ctx
    tq = _tile(n_lat, ATTN_Q_TILE)
    cols = GQA_GROUP * ATTN_SUB if gqa else tq
    t_out = t if with_ctx else n_lat
    w_out = (qw // dk) * dv
    kern = functools.partial(_attn_kernel, gqa=gqa, dk=dk, dv=dv, n_ctx=n_ctx, n_lat=n_lat,
                             with_ctx=with_ctx, tq=tq)
    return pl.pallas_call(
        kern,
        grid=(b,),
        in_specs=[pl.BlockSpec((1, t, qw), lambda bb: (bb, 0, 0)),
                  pl.BlockSpec((1, hk, t, dk), lambda bb: (bb, 0, 0, 0)),
                  pl.BlockSpec((1, hk, V_ROWS, t), lambda bb: (bb, 0, 0, 0))],
        out_specs=pl.BlockSpec((1, t_out, w_out), lambda bb: (bb, 0, 0)),
        out_shape=jax.ShapeDtypeStruct((b, t_out, w_out), BF16),
        scratch_shapes=[pltpu.VMEM((ATTN_DEPTH, t, cols), F32),
                        pltpu.VMEM((t, cols), BF16), pltpu.VMEM((t, cols), BF16)],
        compiler_params=_cparams(("arbitrary",), V7X_VMEM_LIMIT_BYTES),
        name=name,
    )(q, k, vt)


def _merge_kernel(*refs, n_x, nc, d, n_exp):
    x_refs, refs = refs[:n_x], refs[n_x:]
    (oa_ref, ol_ref, om_ref, gt_ref, m_ref, g_ref, wa_ref, wl_ref, wm_ref, wo_ref, rw_ref, rb_ref,
     xo_ref, h_ref, ids_ref, wts_ref) = refs
    tm = xo_ref.shape[1]
    parts = [slice(k * tm // MERGE_PARTS, (k + 1) * tm // MERGE_PARTS) for k in range(MERGE_PARTS)]
    x_in = _read_stream(x_refs, pl.program_id(1), nc)
    merged = []
    for r in parts:
        gates = gt_ref[0, r, :]
        merged.append(
            gates[:, 0:d].astype(F32) * jnp.dot(oa_ref[0, r, :], wa_ref[0], preferred_element_type=F32)
            + gates[:, d:2 * d].astype(F32) * jnp.dot(ol_ref[0, r, :], wl_ref[0], preferred_element_type=F32)
            + gates[:, 2 * d:3 * d].astype(F32) * jnp.dot(om_ref[0, r, :], wm_ref[0], preferred_element_type=F32))
    ys = [jnp.dot(mg.astype(BF16), wo_ref[0], preferred_element_type=F32) for mg in merged]
    his, los = [], []
    for r, y in zip(parts, ys):
        x = x_in[r, :] + m_ref[0, 0, 2:3, :] * y
        xo_ref[0, r, :] = x
        h = _rms(x) * g_ref[0]
        h = h * (1.0 + m_ref[0, 0, 4:5, :]) + m_ref[0, 0, 3:4, :]
        h_hi = h.astype(BF16)
        h_ref[0, r, :] = h_hi
        his.append(h_hi)
        los.append((h - h_hi.astype(F32)).astype(BF16))

    scores = jnp.concatenate(
        [_sigmoid(jnp.dot(jnp.concatenate([h_hi, h_lo, h_hi], axis=-1), rw_ref[...],
                          preferred_element_type=F32)).T[0:n_exp, :]
         for h_hi, h_lo in zip(his, los)], axis=-1)
    sel = scores + rb_ref[...]
    per = n_exp // N_GROUPS
    gs = []
    for g in range(N_GROUPS):
        r = [sel[g * per + j:g * per + j + 1, :] for j in range(per)]
        best = None
        for a in range(per):
            for bq in range(a + 1, per):
                pair = r[a] + r[bq]
                best = pair if best is None else jnp.maximum(best, pair)
        gs.append(best)
    gmax = functools.reduce(jnp.maximum, gs)
    gbest = jnp.full(gmax.shape, N_GROUPS - 1, jnp.int32)
    for g in range(N_GROUPS - 2, -1, -1):
        gbest = jnp.where(gs[g] == gmax, g, gbest)
    eid = lax.broadcasted_iota(jnp.int32, sel.shape, 0)
    gid = jnp.zeros(sel.shape, jnp.int32)
    for g in range(1, N_GROUPS):
        gid = gid + (eid >= g * per).astype(jnp.int32)
    masked = jnp.where(gid == gbest, sel, -jnp.inf)
    m1 = jnp.max(masked, axis=0, keepdims=True)
    i1 = jnp.min(jnp.where(masked == m1, eid, n_exp), axis=0, keepdims=True)
    masked2 = jnp.where(eid == i1, -jnp.inf, masked)
    m2 = jnp.max(masked2, axis=0, keepdims=True)
    i2 = jnp.min(jnp.where(masked2 == m2, eid, n_exp), axis=0, keepdims=True)
    s1 = jnp.sum(jnp.where(eid == i1, scores, 0.0), axis=0, keepdims=True)
    s2 = jnp.sum(jnp.where(eid == i2, scores, 0.0), axis=0, keepdims=True)
    tot = s1 + s2
    ids_ref[0] = jnp.concatenate([i1, i2], axis=0)
    wts_ref[0] = jnp.concatenate([ROUTED_SCALE * s1 / tot, ROUTED_SCALE * s2 / tot], axis=0)


def _merge(streams, o_attn, o_lru, o_mla, gates, layer, p, n_ctx, with_ctx):
    b, d = streams[0].shape[0], streams[0].shape[-1]
    t = o_lru.shape[1]
    n_exp = p["r_bias"].shape[0]
    tm = _tile(math.gcd(n_ctx, t), ROW_TILE)
    off = 0 if with_ctx else n_ctx // tm
    nc = n_ctx // tm
    tq = t if with_ctx else t - n_ctx
    nt = tq // tm
    full = lambda bb, i: (bb, i + off, 0)
    qrow = lambda bb, i: (bb, i, 0)
    const = lambda bb, i: (0, 0)
    lay3 = lambda bb, i: (layer, 0, 0)
    bw = o_attn.shape[-1]
    route_spec = pl.BlockSpec((1, 2, tm), lambda bb, i: (bb * nt + i, 0, 0))
    wspec = lambda a: pl.BlockSpec((1,) + a.shape[1:], lay3)
    in_specs = _stream_specs(streams, tm, nc, lambda bb, i: i + off, lambda bb, i: bb)
    in_specs += [pl.BlockSpec((1, tm, bw), qrow),
                 pl.BlockSpec((1, tm, o_lru.shape[-1]), full),
                 pl.BlockSpec((1, tm, bw), qrow),
                 pl.BlockSpec((1, tm, 3 * d), full),
                 pl.BlockSpec((1, 1, N_MOD, d), lambda bb, i: (layer, jnp.where(i + off < nc, 0, bb + 1), 0, 0)),
                 pl.BlockSpec((1, 1, d), lay3),
                 wspec(p["w_ba"]), wspec(p["w_bl"]), wspec(p["w_bm"]), wspec(p["w_out"]),
                 pl.BlockSpec(p["r_w"].shape, const),
                 pl.BlockSpec(p["r_bias"].shape, const)]
    return pl.pallas_call(
        functools.partial(_merge_kernel, n_x=len(streams), nc=nc - off, d=d, n_exp=n_exp),
        grid=(b, nt),
        in_specs=in_specs,
        out_specs=[pl.BlockSpec((1, tm, d), qrow), pl.BlockSpec((1, tm, d), qrow),
                   route_spec, route_spec],
        out_shape=[jax.ShapeDtypeStruct((b, tq, d), F32), jax.ShapeDtypeStruct((b, tq, d), BF16),
                   jax.ShapeDtypeStruct((b * nt, 2, tm), jnp.int32),
                   jax.ShapeDtypeStruct((b * nt, 2, tm), F32)],
        compiler_params=_cparams(("arbitrary", "arbitrary"), V7X_VMEM_LIMIT_BYTES),
        name="merge_router",
    )(*streams, o_attn, o_lru, o_mla, gates, p["mods"], p["norm_ffn"], p["w_ba"], p["w_bl"], p["w_bm"],
      p["w_out"], p["r_w"], p["r_bias"])


def _moe_kernel(h_ref, idr_ref, idc_ref, wtc_ref, low_ref, wg_ref, wu_ref, wd_ref, o_ref,
                xs, ys, cs, posc, meta, *, per, n_pairs):
    s = pl.program_id(1)
    tm = h_ref.shape[0]
    slots = xs.shape[0]
    pairs_per_group = per // 2

    @pl.when(s == 0)
    def _sort_tokens():
        e_row = idr_ref[0, 0:1, :]
        e_col = idc_ref[:, 0:1]
        g_row = jnp.zeros(e_row.shape, jnp.int32)
        g_col = jnp.zeros(e_col.shape, jnp.int32)
        for g in range(1, N_GROUPS):
            g_row = g_row + (e_row >= g * per).astype(jnp.int32)
            g_col = g_col + (e_col >= g * per).astype(jnp.int32)
        sub8 = lax.broadcasted_iota(jnp.int32, (V7X_SUBLANES, tm), 0)
        lane = lax.broadcasted_iota(jnp.int32, (tm, V7X_LANES), 1)
        onehot_row = jnp.where(sub8 == g_row, 1.0, 0.0)
        onehot_col = jnp.where(lane == g_col, 1.0, 0.0)
        rank_col = jnp.dot(low_ref[...], onehot_col.astype(BF16), preferred_element_type=F32)
        rank_row = lax.dot_general(onehot_row.astype(BF16), low_ref[...], (((1,), (1,)), ((), ())),
                                   preferred_element_type=F32)
        start = 0
        start_row = jnp.zeros((V7X_SUBLANES, 1), F32)
        start_col = jnp.zeros((1, V7X_LANES), F32)
        for g in range(N_GROUPS):
            cnt = jnp.sum((g_row == g).astype(jnp.int32))
            n_sub = lax.shift_right_logical(cnt + (V7X_LANES - 1), 7)
            meta[g] = start
            meta[N_GROUPS + g] = n_sub
            start_f = jnp.asarray(start, jnp.int32).astype(F32)
            start_row = jnp.where(lax.broadcasted_iota(jnp.int32, start_row.shape, 0) == g, start_f, start_row)
            start_col = jnp.where(lax.broadcasted_iota(jnp.int32, start_col.shape, 1) == g, start_f, start_col)
            start = start + n_sub * V7X_LANES
        pos_row = jnp.sum(onehot_row * (rank_row + start_row), axis=0, keepdims=True).astype(jnp.int32)
        pos_col = jnp.sum(onehot_col * (rank_col + start_col), axis=1, keepdims=True).astype(jnp.int32)
        posc[...] = pos_col
        slot = lax.broadcasted_iota(jnp.int32, (slots, tm), 0)
        perm = jnp.where(slot == pos_row, 1.0, 0.0).astype(BF16)
        xs[...] = jnp.dot(perm, h_ref[...], preferred_element_type=F32).astype(BF16)
        ids_c, wts_c = idc_ref[...], wtc_ref[...]
        comb = jnp.zeros((tm, V7X_LANES), F32)
        for k in range(2):
            local = jnp.bitwise_and(ids_c[:, k:k + 1], per - 1)
            comb = comb + jnp.where(lane == local, wts_c[:, k:k + 1], 0.0)
        c_hi = comb.astype(BF16)
        c_lo = (comb - c_hi.astype(F32)).astype(BF16)
        both = jnp.dot(perm, jnp.concatenate([c_hi, c_lo], axis=-1), preferred_element_type=F32)
        cs[...] = both[:, :V7X_LANES] + both[:, V7X_LANES:]
        ys[...] = jnp.zeros_like(ys)

    grp = s // pairs_per_group
    first_local = (s % pairs_per_group) * 2
    seg_start = meta[grp]
    lane_sub = lax.broadcasted_iota(jnp.int32, (V7X_LANES, V7X_LANES), 1)

    def sub_tile(k, carry):
        r0 = pl.multiple_of(seg_start + k * V7X_LANES, V7X_LANES)
        x = xs[pl.ds(r0, V7X_LANES), :]
        c_all = cs[pl.ds(r0, V7X_LANES), :]
        acts = []
        for jj in range(2):
            c = jnp.sum(jnp.where(lane_sub == first_local + jj, c_all, 0.0), axis=-1, keepdims=True)
            g = jnp.dot(x, wg_ref[0, jj], preferred_element_type=F32)
            u = jnp.dot(x, wu_ref[0, jj], preferred_element_type=F32)
            acts.append(((g * _sigmoid(g)) * u * c).astype(BF16))
        w_down = wd_ref[0].reshape(2 * wd_ref.shape[2], wd_ref.shape[3])
        acc = jnp.dot(jnp.concatenate(acts, axis=-1), w_down, preferred_element_type=F32)
        ys[pl.ds(r0, V7X_LANES), :] = ys[pl.ds(r0, V7X_LANES), :] + acc
        return carry

    lax.fori_loop(0, meta[N_GROUPS + grp], sub_tile, 0)

    @pl.when(s == n_pairs - 1)
    def _unsort():
        slot_l = lax.broadcasted_iota(jnp.int32, (tm, slots), 1)
        perm_t = jnp.where(slot_l == posc[...], 1.0, 0.0).astype(BF16)
        o_ref[...] = jnp.dot(perm_t, ys[...].astype(BF16), preferred_element_type=F32)


def _moe(h, ids, wts, layer, p):
    n, d = h.shape
    _, n_exp, _, ff = p["moe_wg"].shape
    per = n_exp // N_GROUPS
    assert per % 2 == 0 and per & (per - 1) == 0
    rt = ids.shape[-1]
    tm = _tile(n, MOE_ROW_TILE)
    assert tm % rt == 0 and tm % V7X_LANES == 0
    slots = tm + N_GROUPS * V7X_LANES
    ids_row = ids.reshape(n // tm, tm // rt, 2, rt).transpose(0, 2, 1, 3).reshape(n // tm, 2, tm)
    ids_col = jnp.transpose(ids, (0, 2, 1)).reshape(n, 2)
    wts_col = jnp.transpose(wts, (0, 2, 1)).reshape(n, 2)
    low = jnp.tril(jnp.ones((tm, tm), BF16), -1)
    n_pairs = n_exp // 2
    wspec = lambda a, b: pl.BlockSpec((1, 2, a, b), lambda i, s: (layer, s, 0, 0))
    return pl.pallas_call(
        functools.partial(_moe_kernel, per=per, n_pairs=n_pairs),
        grid=(n // tm, n_pairs),
        in_specs=[pl.BlockSpec((tm, d), lambda i, s: (i, 0)),
                  pl.BlockSpec((1, 2, tm), lambda i, s: (i, 0, 0)),
                  pl.BlockSpec((tm, 2), lambda i, s: (i, 0)),
                  pl.BlockSpec((tm, 2), lambda i, s: (i, 0)),
                  pl.BlockSpec((tm, tm), lambda i, s: (0, 0), pipeline_mode=pl.Buffered(1)),
                  wspec(d, ff), wspec(d, ff), wspec(ff, d)],
        out_specs=pl.BlockSpec((tm, d), lambda i, s: (i, 0)),
        out_shape=jax.ShapeDtypeStruct((n, d), F32),
        scratch_shapes=[pltpu.VMEM((slots, d), BF16), pltpu.VMEM((slots, d), F32),
                        pltpu.VMEM((slots, V7X_LANES), F32), pltpu.VMEM((tm, 1), jnp.int32),
                        pltpu.SMEM((2 * N_GROUPS,), jnp.int32)],
        compiler_params=_cparams(("arbitrary", "arbitrary"), V7X_VMEM_LIMIT_BYTES),
        name="moe_ffn",
    )(h, ids_row, ids_col, wts_col, low, p["moe_wg"], p["moe_wu"], p["moe_wd"])


def _final_kernel(x_ref, f_ref, m_ref, g_ref, o_ref):
    x = x_ref[0] + m_ref[0, 0, 5:6, :] * f_ref[0]
    o_ref[0] = _rms(x) * g_ref[...]


def _final(x, f, mods, layer, g):
    b, s, d = x.shape
    tm = _tile(s, 2 * ROW_TILE)
    row = lambda bb, i: (bb, i, 0)
    return pl.pallas_call(
        _final_kernel,
        grid=(b, s // tm),
        in_specs=[pl.BlockSpec((1, tm, d), row), pl.BlockSpec((1, tm, d), row),
                  pl.BlockSpec((1, 1, N_MOD, d), lambda bb, i: (layer, bb + 1, 0, 0)),
                  pl.BlockSpec((1, d), lambda bb, i: (0, 0))],
        out_specs=pl.BlockSpec((1, tm, d), row),
        out_shape=jax.ShapeDtypeStruct((b, s, d), F32),
        compiler_params=_cparams(("arbitrary", "arbitrary")),
        name="final_norm",
    )(x, f, mods, g.reshape(1, d))


def _rope_tables(n_ctx, n_lat, dim, period, lane_off, width):
    quarter = dim // 4
    pos = jnp.arange(n_lat, dtype=F32)
    r, col = jnp.floor(pos / GRID_W), pos - GRID_W * jnp.floor(pos / GRID_W)
    inv_freq = ROPE_THETA ** (-jnp.arange(quarter, dtype=F32) / quarter)
    lane = np.arange(width)
    j = (lane % period) - lane_off
    active = (j >= 0) & (j < dim)
    jj = np.where(active, j, 0)
    use_col = jj >= dim // 2
    upper = (jj % (dim // 2)) >= quarter
    f = jj % quarter
    ang = jnp.where(use_col[None, :], col[:, None], r[:, None]) * inv_freq[f][None, :]
    act = jnp.asarray(active)[None, :]
    cos = jnp.where(act, jnp.cos(ang), 1.0)
    sin = jnp.where(act, jnp.sin(ang), 0.0)
    sin_up = jnp.where(jnp.asarray(~upper)[None, :], -sin, 0.0)
    sin_dn = jnp.where(jnp.asarray(upper)[None, :], sin, 0.0)
    ident = lambda v, fill: jnp.concatenate([jnp.full((n_ctx, width), fill, F32), v], axis=0)
    return ident(cos, 1.0), ident(sin_up, 0.0), ident(sin_dn, 0.0)


def _lru_gate_weights(w_a, w_i):
    depth, _, nb, k, j = w_a.shape
    w = jnp.stack([w_a, w_i], axis=2)
    eye = jnp.eye(nb, dtype=w.dtype)
    dense = jnp.einsum('dsgnkj,nm->dnksgmj', w, eye)
    return (0.5 * dense).reshape(depth, nb * k, 4 * nb * j).astype(BF16)


def kernel(x, c, ctx, c_ctx, w_mod, b_mod, norm_mix, norm_ffn, w_in, gqa_q_norm, gqa_k_norm, conv_w, conv_b,
           lru_w_a, lru_b_a, lru_w_i, lru_b_i, lru_lam, mla_q_a_norm, mla_w_qb, mla_kv_a_norm, mla_w_kvb,
           w_branch_attn, w_branch_lru, w_branch_mla, w_out, router_w, router_bias,
           moe_w_gate, moe_w_up, moe_w_down, final_norm):
    bsz, n_lat, d = x.shape
    n_ctx = ctx.shape[1]
    depth = w_mod.shape[0]
    hd = gqa_q_norm.shape[-1]
    lru_w = conv_w.shape[-1]
    q_lora, kv_lora = mla_q_a_norm.shape[-1], mla_kv_a_norm.shape[-1]
    n_exp = router_w.shape[-1]
    assert bsz + 1 <= MOD_ROWS and d % V7X_LANES == 0

    cc = jnp.zeros((MOD_ROWS, d), F32).at[0].set(c_ctx).at[1:1 + bsz].set(c)
    mods = _modulation(cc, w_mod, b_mod).reshape(depth, MOD_ROWS, N_MOD, d)

    mq_period = MLA_QK * V7X_LANES // math.gcd(MLA_QK, V7X_LANES)
    kr_end = GQA_HEADS * hd + 2 * GQA_KV_HEADS * hd + 2 * lru_w + q_lora + kv_lora + MLA_ROPE
    w_in_b = _pack_w_in(w_in, kr_end)
    r_pad = jnp.pad(router_w, ((0, 0), (0, V7X_LANES - n_exp)))
    r_hi = r_pad.astype(BF16)
    r_w = jnp.concatenate([r_hi, r_hi, (r_pad - r_hi.astype(F32)).astype(BF16)], axis=0)
    p = dict(
        hd=hd, lru_w=lru_w, q_lora=q_lora, kv_lora=kv_lora, mods=mods,
        norm_mix=norm_mix.reshape(depth, 1, d), norm_ffn=norm_ffn.reshape(depth, 1, d), w_in=w_in_b,
        gq=jnp.tile(gqa_q_norm, (1, GQA_HEADS)).reshape(depth, 1, -1),
        gk=jnp.tile(gqa_k_norm, (1, GQA_KV_HEADS)).reshape(depth, 1, -1),
        gqa_tabs=_rope_tables(n_ctx, n_lat, hd, hd, 0, V7X_LANES),
        mla_q_tabs=_rope_tables(n_ctx, n_lat, MLA_ROPE, MLA_QK, MLA_NOPE, mq_period),
        mla_k_tabs=_rope_tables(n_ctx, n_lat, MLA_ROPE, V7X_LANES, 0, V7X_LANES),
        mla_q_norm=mla_q_a_norm.reshape(depth, 1, q_lora), mla_kv_norm=mla_kv_a_norm.reshape(depth, 1, kv_lora),
        w_qb=mla_w_qb.astype(BF16), w_kvb=mla_w_kvb.astype(BF16),
        conv_w=conv_w, conv_b=conv_b.reshape(depth, 1, lru_w),
        lru_wg=_lru_gate_weights(lru_w_a, lru_w_i),
        lru_bg=0.5 * jnp.stack([lru_b_a, lru_b_i], axis=2).reshape(depth, 1, 4 * lru_w),
        lru_lam=lru_lam,
        w_ba=w_branch_attn.astype(BF16), w_bl=w_branch_lru.astype(BF16), w_bm=w_branch_mla.astype(BF16),
        w_out=w_out.astype(BF16), r_w=r_w,
        r_bias=router_bias.reshape(n_exp, 1).astype(F32),
        moe_wg=moe_w_gate.astype(BF16), moe_wu=moe_w_up.astype(BF16), moe_wd=moe_w_down.astype(BF16),
    )

    streams = (ctx, x)
    prev = None
    for layer in range(depth):
        last = layer == depth - 1
        streams, (q_g, k_g, v_g, zlru, q_m, k_m, v_m, gates) = _in_proj(streams, prev, layer, p, n_ctx)
        o_attn = _attention(q_g, k_g, v_g, gqa=True, n_ctx=n_ctx, with_ctx=not last, name="gqa_attention")
        o_lru = _lru(zlru, layer, p, n_ctx)
        o_mla = _attention(q_m, k_m, v_m, gqa=False, n_ctx=n_ctx, with_ctx=not last, name="mla_attention")
        x_mid, h2, ids, wts = _merge(streams, o_attn, o_lru, o_mla, gates, layer, p, n_ctx, with_ctx=not last)

        n_tok = x_mid.shape[0] * x_mid.shape[1]
        f = _moe(h2.reshape(n_tok, d), ids, wts, layer, p).reshape(x_mid.shape)
        streams, prev = (x_mid,), (f, layer)

    return _final(streams[0], prev[0], mods, prev[1], final_norm)
```

```python
import functools
import math

import numpy as np
import jax
import jax.numpy as jnp
from jax import lax
from jax.experimental import pallas as pl
from jax.experimental.pallas import tpu as pltpu

F32 = jnp.float32
BF16 = jnp.bfloat16

GRID_W = 64
ROPE_THETA = 10000.0
NORM_EPS = 1e-6
N_MOD = 6
GQA_HEADS = 8
GQA_KV_HEADS = 2
GQA_GROUP = GQA_HEADS // GQA_KV_HEADS
MLA_HEADS = 8
MLA_NOPE = 64
MLA_ROPE = 32
MLA_V = 64
MLA_QK = MLA_NOPE + MLA_ROPE
CONV_WIDTH = 4
LRU_C = 8.0
N_GROUPS = 4
ROUTED_SCALE = 1.0
LOG2E = math.log2(math.e)

V7X_LANES = 128
V7X_SUBLANES = 8
V7X_VMEM_LIMIT_BYTES = 56 * 1024 * 1024

ROW_TILE = 256
ATTN_Q_TILE = 512
ATTN_SUB = 128
ATTN_DEPTH = 2
ATTN_KEY_BLOCK = 256
V_ROWS = 80
MERGE_PARTS = 2
MOE_ROW_TILE = 1024
MOD_ROWS = 16


def _cparams(sem, vmem=None):
    return pltpu.CompilerParams(dimension_semantics=sem, vmem_limit_bytes=vmem)


def _tile(n, pref):
    t = min(n, pref)
    while n % t or t % V7X_SUBLANES:
        t -= 1
    return t


def _sigmoid(x):
    return 0.5 * (1.0 + jnp.tanh(0.5 * x))


def _rms(x):
    return x * lax.rsqrt(jnp.mean(x * x, axis=-1, keepdims=True) + NORM_EPS)


def _tiled(tbl, width):
    reps = width // tbl.shape[-1]
    return tbl if reps == 1 else jnp.concatenate([tbl] * reps, axis=-1)


def _rope_lanes(x, cos, sin_up, sin_dn, half):
    outs = []
    for c in range(x.shape[-1] // V7X_LANES):
        sl = slice(c * V7X_LANES, (c + 1) * V7X_LANES)
        xc = x[:, sl]
        up = pltpu.roll(xc, V7X_LANES - half, 1)
        dn = pltpu.roll(xc, half, 1)
        outs.append(xc * cos[:, sl] + up * sin_up[:, sl] + dn * sin_dn[:, sl])
    return outs[0] if len(outs) == 1 else jnp.concatenate(outs, axis=-1)


def _ones_row_tail(rows, cols):
    r = lax.broadcasted_iota(jnp.int32, (rows, cols), 0)
    return jnp.where(r == 0, 1.0, 0.0).astype(BF16)


def _stream_specs(streams, tm, nc, tile_of, batch_of):
    d = streams[0].shape[-1]
    if len(streams) == 1:
        return [pl.BlockSpec((1, tm, d), lambda *g: (batch_of(*g), tile_of(*g), 0))]
    return [pl.BlockSpec((1, tm, d), lambda *g: (batch_of(*g), jnp.minimum(tile_of(*g), nc - 1), 0)),
            pl.BlockSpec((1, tm, d), lambda *g: (batch_of(*g), jnp.maximum(tile_of(*g) - nc, 0), 0))]


def _read_stream(refs, tile, nc):
    if len(refs) == 1:
        return refs[0][0]
    return jnp.where(tile < nc, refs[0][0], refs[1][0])


def _mod_kernel(c_ref, w_ref, b_ref, o_ref):
    c = c_ref[...]
    a = (c * _sigmoid(c)).astype(BF16)
    o_ref[0] = jnp.dot(a, w_ref[0].astype(BF16), preferred_element_type=F32) + b_ref[0]


def _modulation(cc, w_mod, b_mod):
    depth, d, n = w_mod.shape
    tn = _tile(n, 1536) if n % V7X_LANES == 0 else n
    return pl.pallas_call(
        _mod_kernel,
        grid=(depth, n // tn),
        in_specs=[pl.BlockSpec((MOD_ROWS, d), lambda l, j: (0, 0)),
                  pl.BlockSpec((1, d, tn), lambda l, j: (l, 0, j)),
                  pl.BlockSpec((1, 1, tn), lambda l, j: (l, 0, j))],
        out_specs=pl.BlockSpec((1, MOD_ROWS, tn), lambda l, j: (l, 0, j)),
        out_shape=jax.ShapeDtypeStruct((depth, MOD_ROWS, n), F32),
        compiler_params=_cparams(("arbitrary", "arbitrary"), V7X_VMEM_LIMIT_BYTES),
        name="modulation",
    )(cc, w_mod, b_mod.reshape(depth, 1, n))


def _pack_w_in_kernel(w_ref, o_ref, *, kr_end):
    w = w_ref[0].astype(BF16)
    pad = jnp.zeros((w.shape[0], o_ref.shape[-1] - w.shape[-1]), BF16)
    o_ref[0] = jnp.concatenate([w[:, :kr_end], pad, w[:, kr_end:]], axis=-1)


def _pack_w_in(w_in, kr_end):
    depth, d, n = w_in.shape
    n_out = n + V7X_LANES - MLA_ROPE
    rt = _tile(d, V7X_LANES)
    return pl.pallas_call(
        functools.partial(_pack_w_in_kernel, kr_end=kr_end),
        grid=(depth, d // rt),
        in_specs=[pl.BlockSpec((1, rt, n), lambda l, i: (l, i, 0))],
        out_specs=pl.BlockSpec((1, rt, n_out), lambda l, i: (l, i, 0)),
        out_shape=jax.ShapeDtypeStruct((depth, d, n_out), BF16),
        compiler_params=_cparams(("arbitrary", "arbitrary")),
        name="pack_w_in",
    )(w_in)


def _in_proj_kernel(*refs, n_x, has_prev, nc, hd, lru_w, q_lora, kv_lora, d):
    tile = pl.program_id(0)
    x_refs, refs = refs[:n_x], refs[n_x:]
    if has_prev:
        f_ref, pm_ref, refs = refs[0], refs[1], refs[2:]
    (g_ref, m_ref, w_ref, gq_ref, gk_ref, gc_ref, gu_ref, gd_ref, nqa_ref, nkv_ref, wq_ref, wkv_ref,
     qc_ref, qu_ref, qd_ref, kc_ref, ku_ref, kd_ref) = refs[:18]
    outs = refs[18:]
    x = _read_stream(x_refs, tile, nc)
    if has_prev:
        x = x + pm_ref[0, 0, 5:6, :] * f_ref[0]
        outs[0][0] = x
        outs = outs[1:]
    qg_ref, kg_ref, vg_ref, lru_ref, qm_ref, km_ref, vm_ref, gate_ref = outs
    tm = x.shape[0]
    h = _rms(x) * g_ref[0]
    h = h * (1.0 + m_ref[0, 0, 1:2, :]) + m_ref[0, 0, 0:1, :]
    hb = h.astype(BF16)

    nq, nk = GQA_HEADS * hd, GQA_KV_HEADS * hd
    w_gqa = nq + 2 * nk
    w_mla = q_lora + kv_lora + V7X_LANES
    o_lru, o_mla, o_gate = w_gqa, w_gqa + 2 * lru_w, w_gqa + 2 * lru_w + w_mla

    z = jnp.dot(hb, w_ref[0, :, 0:w_gqa], preferred_element_type=F32)

    def head_norm(a, n_heads):
        return jnp.concatenate([_rms(a[:, i * hd:(i + 1) * hd]) for i in range(n_heads)], axis=-1)

    cos, su, sd = gc_ref[...], gu_ref[...], gd_ref[...]
    q = head_norm(z[:, :nq], GQA_HEADS) * gq_ref[0]
    q = _rope_lanes(q, _tiled(cos, nq), _tiled(su, nq), _tiled(sd, nq), hd // 4)
    qg_ref[0] = (q * (LOG2E * hd ** -0.5)).astype(BF16)
    k = head_norm(z[:, nq:nq + nk], GQA_KV_HEADS) * gk_ref[0]
    k = _rope_lanes(k, _tiled(cos, nk), _tiled(su, nk), _tiled(sd, nk), hd // 4)
    v_t = z[:, nq + nk:nq + 2 * nk].T
    tail = _ones_row_tail(V_ROWS - hd, tm)
    for i in range(GQA_KV_HEADS):
        kg_ref[0, i] = k[:, i * hd:(i + 1) * hd].astype(BF16)
        vg_ref[0, i, 0:hd, :] = v_t[i * hd:(i + 1) * hd, :].astype(BF16)
        vg_ref[0, i, hd:V_ROWS, :] = tail

    lru_ref[0] = jnp.dot(hb, w_ref[0, :, o_lru:o_lru + 2 * lru_w], preferred_element_type=F32)

    z = jnp.dot(hb, w_ref[0, :, o_mla:o_mla + w_mla], preferred_element_type=F32)
    cq = (_rms(z[:, :q_lora]) * nqa_ref[0]).astype(BF16)
    qm = jnp.dot(cq, wq_ref[0], preferred_element_type=F32)
    wq = qm.shape[-1]
    qm = _rope_lanes(qm, _tiled(qc_ref[...], wq), _tiled(qu_ref[...], wq), _tiled(qd_ref[...], wq),
                     MLA_ROPE // 4)
    qm_ref[0] = (qm * (LOG2E * MLA_QK ** -0.5)).astype(BF16)
    ckv = (_rms(z[:, q_lora:q_lora + kv_lora]) * nkv_ref[0]).astype(BF16)
    kv = jnp.dot(ckv, wkv_ref[0], preferred_element_type=F32)
    kr = _rope_lanes(z[:, q_lora + kv_lora:], kc_ref[...], ku_ref[...], kd_ref[...], MLA_ROPE // 4)
    kr = kr[:, :MLA_ROPE]
    per = MLA_NOPE + MLA_V
    tail = _ones_row_tail(V_ROWS - MLA_V, tm)
    for i in range(MLA_HEADS):
        km_ref[0, i] = jnp.concatenate([kv[:, i * per:i * per + MLA_NOPE], kr], axis=-1).astype(BF16)
        head_t = kv[:, i * per:(i + 1) * per].T
        vm_ref[0, i, 0:MLA_V, :] = head_t[MLA_NOPE:per, :].astype(BF16)
        vm_ref[0, i, MLA_V:V_ROWS, :] = tail

    gate_ref[0] = _sigmoid(jnp.dot(hb, w_ref[0, :, o_gate:o_gate + 3 * d],
                                   preferred_element_type=F32)).astype(BF16)


def _in_proj(streams, prev, layer, p, n_ctx):
    b, d = streams[0].shape[0], streams[0].shape[-1]
    t = sum(s.shape[1] for s in streams) if len(streams) == 2 else streams[0].shape[1]
    tm = _tile(math.gcd(n_ctx, t), ROW_TILE)
    nc = n_ctx // tm
    hd, lru_w, q_lora, kv_lora = p["hd"], p["lru_w"], p["q_lora"], p["kv_lora"]
    nq, nk = GQA_HEADS * hd, GQA_KV_HEADS * hd
    qw = MLA_HEADS * MLA_QK
    tile_of, batch_of = (lambda i, bb: i), (lambda i, bb: bb)
    row = lambda i, bb: (bb, i, 0)
    mod_row = lambda i, bb: jnp.where(i < nc, 0, bb + 1)
    lay3 = lambda i, bb: (layer, 0, 0)
    tbl = lambda a: pl.BlockSpec((tm, a.shape[-1]), lambda i, bb: (i, 0))
    x_spec = pl.BlockSpec((1, tm, d), row)

    in_specs = _stream_specs(streams, tm, nc, tile_of, batch_of)
    args = list(streams)
    out_specs, out_shape = [], []
    if prev is not None:
        in_specs += [x_spec, pl.BlockSpec((1, 1, N_MOD, d), lambda i, bb: (prev[1], mod_row(i, bb), 0, 0))]
        args += [prev[0], p["mods"]]
        out_specs.append(x_spec)
        out_shape.append(jax.ShapeDtypeStruct((b, t, d), F32))
    in_specs += [pl.BlockSpec((1, 1, d), lay3),
                 pl.BlockSpec((1, 1, N_MOD, d), lambda i, bb: (layer, mod_row(i, bb), 0, 0)),
                 pl.BlockSpec((1,) + p["w_in"].shape[1:], lay3, pipeline_mode=pl.Buffered(1)),
                 pl.BlockSpec((1, 1, nq), lay3), pl.BlockSpec((1, 1, nk), lay3)]
    args += [p["norm_mix"], p["mods"], p["w_in"], p["gq"], p["gk"]]
    in_specs += [tbl(a) for a in p["gqa_tabs"]]
    args += list(p["gqa_tabs"])
    in_specs += [pl.BlockSpec((1, 1, q_lora), lay3), pl.BlockSpec((1, 1, kv_lora), lay3),
                 pl.BlockSpec((1,) + p["w_qb"].shape[1:], lay3),
                 pl.BlockSpec((1,) + p["w_kvb"].shape[1:], lay3)]
    args += [p["mla_q_norm"], p["mla_kv_norm"], p["w_qb"], p["w_kvb"]]
    in_specs += [tbl(a) for a in p["mla_q_tabs"]] + [tbl(a) for a in p["mla_k_tabs"]]
    args += list(p["mla_q_tabs"]) + list(p["mla_k_tabs"])

    def kv_specs(heads, dk):
        return [pl.BlockSpec((1, heads, tm, dk), lambda i, bb: (bb, 0, i, 0)),
                pl.BlockSpec((1, heads, V_ROWS, tm), lambda i, bb: (bb, 0, 0, i))]

    out_specs += ([pl.BlockSpec((1, tm, nq), row)] + kv_specs(GQA_KV_HEADS, hd)
                  + [pl.BlockSpec((1, tm, 2 * lru_w), row), pl.BlockSpec((1, tm, qw), row)]
                  + kv_specs(MLA_HEADS, MLA_QK) + [pl.BlockSpec((1, tm, 3 * d), row)])
    out_shape += [jax.ShapeDtypeStruct((b, t, nq), BF16),
                  jax.ShapeDtypeStruct((b, GQA_KV_HEADS, t, hd), BF16),
                  jax.ShapeDtypeStruct((b, GQA_KV_HEADS, V_ROWS, t), BF16),
                  jax.ShapeDtypeStruct((b, t, 2 * lru_w), F32),
                  jax.ShapeDtypeStruct((b, t, qw), BF16),
                  jax.ShapeDtypeStruct((b, MLA_HEADS, t, MLA_QK), BF16),
                  jax.ShapeDtypeStruct((b, MLA_HEADS, V_ROWS, t), BF16),
                  jax.ShapeDtypeStruct((b, t, 3 * d), BF16)]
    outs = pl.pallas_call(
        functools.partial(_in_proj_kernel, n_x=len(streams), has_prev=prev is not None, nc=nc, hd=hd,
                          lru_w=lru_w, q_lora=q_lora, kv_lora=kv_lora, d=d),
        grid=(t // tm, b),
        in_specs=in_specs, out_specs=out_specs, out_shape=out_shape,
        compiler_params=_cparams(("arbitrary", "arbitrary"), V7X_VMEM_LIMIT_BYTES),
        name="in_proj",
    )(*args)
    if prev is not None:
        return (outs[0],), outs[1:]
    return streams, outs


def _lru_kernel(z_ref, cw_ref, cb_ref, wg_ref, bg_ref, lam_ref, o_ref, a_f, b_f, a_b, b_b, *, n_ctx, chunk):
    t, width = a_f.shape
    n_chunks = t // chunk
    row = lax.broadcasted_iota(jnp.int32, (chunk, 1), 0)

    def conv_chunk(c):
        r0 = c * chunk
        seg_lo, seg_hi = (0, n_ctx) if r0 < n_ctx else (n_ctx, t)
        u = jnp.zeros((chunk, width), F32) + cb_ref[0]
        for j in range(CONV_WIDTH):
            off = j - 1
            lo = min(max(r0 + off, 0), t - chunk)
            tap = z_ref[0, lo:lo + chunk, 0:width]
            shift = (lo - (r0 + off)) % chunk
            if shift:
                tap = pltpu.roll(tap, shift, 0)
            if r0 + off < seg_lo or r0 + off + chunk > seg_hi:
                pos = row + (r0 + off)
                tap = jnp.where((pos >= seg_lo) & (pos < seg_hi), tap, 0.0)
            u = u + tap * cw_ref[0, j:j + 1, :]
        return u

    scr = ((a_f, b_f), (a_b, b_b))
    half_c_sp = []
    for dd in range(2):
        lam = lam_ref[0, dd:dd + 1, :]
        sp = jnp.maximum(-lam, 0.0) + jnp.log1p(jnp.exp(-jnp.abs(lam)))
        half_c_sp.append((-0.5 * LRU_C) * sp)
    for c in range(n_chunks):
        u = conv_chunk(c)
        g = jnp.dot(u.astype(BF16), wg_ref[0], preferred_element_type=F32) + bg_ref[0]
        half_u = 0.5 * u
        for dd in range(2):
            base = 2 * dd * width
            t_r = jnp.tanh(g[:, base:base + width])
            t_i = jnp.tanh(g[:, base + width:base + 2 * width])
            log_a = (1.0 + t_r) * half_c_sp[dd]
            th = jnp.tanh(log_a)
            v = (-2.0 * th) / (1.0 - th)
            root = jnp.where(v > 0.0, v * lax.rsqrt(v), 0.0)
            scr[dd][0][c * chunk:(c + 1) * chunk, :] = jnp.exp(log_a)
            scr[dd][1][c * chunk:(c + 1) * chunk, :] = root * ((1.0 + t_i) * half_u)

    ctx_blk = n_ctx // V7X_SUBLANES
    all_blk = t // V7X_SUBLANES
    nsub = V7X_SUBLANES

    def body(i, carry):
        hf, hb = carry
        rf = pl.multiple_of(i * nsub, nsub)
        jb = jnp.where(i < ctx_blk, ctx_blk - 1 - i, all_blk - 1 - (i - ctx_blk))
        rb = pl.multiple_of(jb * nsub, nsub)
        af, bf = a_f[pl.ds(rf, nsub), :], b_f[pl.ds(rf, nsub), :]
        ab, bb = a_b[pl.ds(rb, nsub), :], b_b[pl.ds(rb, nsub), :]
        rows_f, rows_b = [None] * nsub, [None] * nsub
        for s in range(nsub):
            hf = af[s:s + 1, :] * hf + bf[s:s + 1, :]
            rows_f[s] = hf
            sb = nsub - 1 - s
            hb = ab[sb:sb + 1, :] * hb + bb[sb:sb + 1, :]
            rows_b[sb] = hb
        a_f[pl.ds(rf, nsub), :] = jnp.concatenate(rows_f, axis=0)
        a_b[pl.ds(rb, nsub), :] = jnp.concatenate(rows_b, axis=0)
        return hf, hb

    h0 = jnp.zeros((1, width), F32)
    lax.fori_loop(0, all_blk, body, (h0, h0), unroll=2 if all_blk % 2 == 0 else 1)

    k0 = math.sqrt(2.0 / math.pi)
    for c in range(n_chunks):
        sl = slice(c * chunk, (c + 1) * chunk)
        y = z_ref[0, sl, width:2 * width]
        gelu = (0.5 * y) * (1.0 + jnp.tanh(y * (k0 + (k0 * 0.044715) * (y * y))))
        o_ref[0, sl, :] = ((a_f[sl, :] + a_b[sl, :]) * gelu).astype(BF16)


def _lru(zlru, layer, p, n_ctx):
    b, t, w2 = zlru.shape
    width = w2 // 2
    chunk = _tile(math.gcd(n_ctx, t), ROW_TILE)
    lay3 = lambda bb: (layer, 0, 0)
    return pl.pallas_call(
        functools.partial(_lru_kernel, n_ctx=n_ctx, chunk=chunk),
        grid=(b,),
        in_specs=[pl.BlockSpec((1, t, w2), lambda bb: (bb, 0, 0)),
                  pl.BlockSpec((1, CONV_WIDTH, width), lay3),
                  pl.BlockSpec((1, 1, width), lay3),
                  pl.BlockSpec((1,) + p["lru_wg"].shape[1:], lay3),
                  pl.BlockSpec((1, 1, 4 * width), lay3),
                  pl.BlockSpec((1, 2, width), lay3)],
        out_specs=pl.BlockSpec((1, t, width), lambda bb: (bb, 0, 0)),
        out_shape=jax.ShapeDtypeStruct((b, t, width), BF16),
        scratch_shapes=[pltpu.VMEM((t, width), F32)] * 4,
        compiler_params=_cparams(("arbitrary",), V7X_VMEM_LIMIT_BYTES),
        name="rg_lru",
    )(zlru, p["conv_w"], p["conv_b"], p["lru_wg"], p["lru_bg"], p["lru_lam"])


def _attn_kernel(q_ref, k_ref, vt_ref, o_ref, sbuf, p0, p1, *, gqa, dk, dv, n_ctx, n_lat, with_ctx, tq):
    t = n_ctx + n_lat
    pbuf = (p0, p1)
    out_off = 0 if with_ctx else n_ctx
    nt_dims = (((1,), (1,)), ((), ()))
    key_blk = math.gcd(n_ctx, ATTN_KEY_BLOCK)

    def n_sub(rows):
        return GQA_KV_HEADS * (rows // ATTN_SUB) if gqa else MLA_HEADS

    def load_q(r0, j, rows):
        if gqa:
            kvh, piece = j % GQA_KV_HEADS, j // GQA_KV_HEADS
            blk = q_ref[0, pl.ds(r0 + piece * ATTN_SUB, ATTN_SUB), kvh * GQA_GROUP * dk:(kvh + 1) * GQA_GROUP * dk]
            return jnp.concatenate([blk[:, h * dk:(h + 1) * dk] for h in range(GQA_GROUP)], axis=0), kvh
        return q_ref[0, pl.ds(r0, rows), j * dk:(j + 1) * dk], j

    def qk(r0, j, rows, n_keys, slot):
        q, kh = load_q(r0, j, rows)
        sbuf[slot, 0:n_keys, 0:q.shape[0]] = lax.dot_general(
            k_ref[0, kh, 0:n_keys, :], q, nt_dims, preferred_element_type=F32)

    def finish(r0, j, rows, n_keys, slot, par):
        cols = GQA_GROUP * ATTN_SUB if gqa else rows
        kh = j % GQA_KV_HEADS if gqa else j
        m = None
        for b0 in range(0, n_keys, key_blk):
            mb = jnp.max(sbuf[slot, b0:b0 + key_blk, 0:cols], axis=0, keepdims=True)
            m = mb if m is None else jnp.maximum(m, mb)
        for b0 in range(0, n_keys, key_blk):
            pbuf[par][b0:b0 + key_blk, 0:cols] = jnp.exp2(sbuf[slot, b0:b0 + key_blk, 0:cols] - m).astype(BF16)
        ot = jnp.dot(vt_ref[0, kh, :, 0:n_keys], pbuf[par][0:n_keys, 0:cols], preferred_element_type=F32)
        o = (ot[0:dv, :] / ot[dv:dv + 1, :]).T.astype(o_ref.dtype)
        if gqa:
            piece = j // GQA_KV_HEADS
            out = jnp.concatenate([o[h * ATTN_SUB:(h + 1) * ATTN_SUB, :] for h in range(GQA_GROUP)], axis=-1)
            o_ref[0, pl.ds(r0 - out_off + piece * ATTN_SUB, ATTN_SUB),
                  kh * GQA_GROUP * dv:(kh + 1) * GQA_GROUP * dv] = out
        else:
            o_ref[0, pl.ds(r0 - out_off, rows), j * dv:(j + 1) * dv] = o

    if with_ctx:
        rows_c = min(tq, n_ctx)
        for r in range(0, n_ctx, rows_c):
            for j in range(n_sub(rows_c)):
                qk(r, j, rows_c, n_ctx, 0)
                finish(r, j, rows_c, n_ctx, 0, 0)

    n_tiles = n_lat // tq
    ns = n_sub(tq)
    assert ns % ATTN_DEPTH == 0
    qk(n_ctx, 0, tq, t, 0)

    def body(i, carry):
        r0 = pl.multiple_of(n_ctx + i * tq, ATTN_SUB)
        r_next = pl.multiple_of(n_ctx + jnp.minimum(i + 1, n_tiles - 1) * tq, ATTN_SUB)
        for j in range(ns):
            if j + 1 < ns:
                qk(r0, j + 1, tq, t, (j + 1) % ATTN_DEPTH)
            else:
                qk(r_next, 0, tq, t, 0)
            finish(r0, j, tq, t, j % ATTN_DEPTH, j % 2)
        return carry

    lax.fori_loop(0, n_tiles, body, 0)


def _attention(q, k, vt, *, gqa, n_ctx, with_ctx, name):
    b, t, qw = q.shape
    hk, dk = k.shape[1], k.shape[-1]
    dv = MLA_V if not gqa else dk
    n_lat = t - n_ctx
    tq = _tile(n_lat, ATTN_Q_TILE)
    cols = GQA_GROUP * ATTN_SUB if gqa else tq
    t_out = t if with_ctx else n_lat
    w_out = (qw // dk) * dv
    kern = functools.partial(_attn_kernel, gqa=gqa, dk=dk, dv=dv, n_ctx=n_ctx, n_lat=n_lat,
                             with_ctx=with_ctx, tq=tq)
    return pl.pallas_call(
        kern,
        grid=(b,),
        in_specs=[pl.BlockSpec((1, t, qw), lambda bb: (bb, 0, 0)),
                  pl.BlockSpec((1, hk, t, dk), lambda bb: (bb, 0, 0, 0)),
                  pl.BlockSpec((1, hk, V_ROWS, t), lambda bb: (bb, 0, 0, 0))],
        out_specs=pl.BlockSpec((1, t_out, w_out), lambda bb: (bb, 0, 0)),
        out_shape=jax.ShapeDtypeStruct((b, t_out, w_out), BF16),
        scratch_shapes=[pltpu.VMEM((ATTN_DEPTH, t, cols), F32),
                        pltpu.VMEM((t, cols), BF16), pltpu.VMEM((t, cols), BF16)],
        compiler_params=_cparams(("arbitrary",), V7X_VMEM_LIMIT_BYTES),
        name=name,
    )(q, k, vt)


def _merge_kernel(*refs, n_x, nc, d, n_exp):
    x_refs, refs = refs[:n_x], refs[n_x:]
    (oa_ref, ol_ref, om_ref, gt_ref, m_ref, g_ref, wa_ref, wl_ref, wm_ref, wo_ref, rw_ref, rb_ref,
     xo_ref, h_ref, ids_ref, wts_ref) = refs
    tm = xo_ref.shape[1]
    parts = [slice(k * tm // MERGE_PARTS, (k + 1) * tm // MERGE_PARTS) for k in range(MERGE_PARTS)]
    x_in = _read_stream(x_refs, pl.program_id(1), nc)
    merged = []
    for r in parts:
        gates = gt_ref[0, r, :]
        merged.append(
            gates[:, 0:d].astype(F32) * jnp.dot(oa_ref[0, r, :], wa_ref[0], preferred_element_type=F32)
            + gates[:, d:2 * d].astype(F32) * jnp.dot(ol_ref[0, r, :], wl_ref[0], preferred_element_type=F32)
            + gates[:, 2 * d:3 * d].astype(F32) * jnp.dot(om_ref[0, r, :], wm_ref[0], preferred_element_type=F32))
    ys = [jnp.dot(mg.astype(BF16), wo_ref[0], preferred_element_type=F32) for mg in merged]
    his, los = [], []
    for r, y in zip(parts, ys):
        x = x_in[r, :] + m_ref[0, 0, 2:3, :] * y
        xo_ref[0, r, :] = x
        h = _rms(x) * g_ref[0]
        h = h * (1.0 + m_ref[0, 0, 4:5, :]) + m_ref[0, 0, 3:4, :]
        h_hi = h.astype(BF16)
        h_ref[0, r, :] = h_hi
        his.append(h_hi)
        los.append((h - h_hi.astype(F32)).astype(BF16))

    scores = jnp.concatenate(
        [_sigmoid(jnp.dot(jnp.concatenate([h_hi, h_lo, h_hi], axis=-1), rw_ref[...],
                          preferred_element_type=F32)).T[0:n_exp, :]
         for h_hi, h_lo in zip(his, los)], axis=-1)
    sel = scores + rb_ref[...]
    per = n_exp // N_GROUPS
    gs = []
    for g in range(N_GROUPS):
        r = [sel[g * per + j:g * per + j + 1, :] for j in range(per)]
        best = None
        for a in range(per):
            for bq in range(a + 1, per):
                pair = r[a] + r[bq]
                best = pair if best is None else jnp.maximum(best, pair)
        gs.append(best)
    gmax = functools.reduce(jnp.maximum, gs)
    gbest = jnp.full(gmax.shape, N_GROUPS - 1, jnp.int32)
    for g in range(N_GROUPS - 2, -1, -1):
        gbest = jnp.where(gs[g] == gmax, g, gbest)
    eid = lax.broadcasted_iota(jnp.int32, sel.shape, 0)
    gid = jnp.zeros(sel.shape, jnp.int32)
    for g in range(1, N_GROUPS):
        gid = gid + (eid >= g * per).astype(jnp.int32)
    masked = jnp.where(gid == gbest, sel, -jnp.inf)
    m1 = jnp.max(masked, axis=0, keepdims=True)
    i1 = jnp.min(jnp.where(masked == m1, eid, n_exp), axis=0, keepdims=True)
    masked2 = jnp.where(eid == i1, -jnp.inf, masked)
    m2 = jnp.max(masked2, axis=0, keepdims=True)
    i2 = jnp.min(jnp.where(masked2 == m2, eid, n_exp), axis=0, keepdims=True)
    s1 = jnp.sum(jnp.where(eid == i1, scores, 0.0), axis=0, keepdims=True)
    s2 = jnp.sum(jnp.where(eid == i2, scores, 0.0), axis=0, keepdims=True)
    tot = s1 + s2
    ids_ref[0] = jnp.concatenate([i1, i2], axis=0)
    wts_ref[0] = jnp.concatenate([ROUTED_SCALE * s1 / tot, ROUTED_SCALE * s2 / tot], axis=0)


def _merge(streams, o_attn, o_lru, o_mla, gates, layer, p, n_ctx, with_ctx):
    b, d = streams[0].shape[0], streams[0].shape[-1]
    t = o_lru.shape[1]
    n_exp = p["r_bias"].shape[0]
    tm = _tile(math.gcd(n_ctx, t), ROW_TILE)
    off = 0 if with_ctx else n_ctx // tm
    nc = n_ctx // tm
    tq = t if with_ctx else t - n_ctx
    nt = tq // tm
    full = lambda bb, i: (bb, i + off, 0)
    qrow = lambda bb, i: (bb, i, 0)
    const = lambda bb, i: (0, 0)
    lay3 = lambda bb, i: (layer, 0, 0)
    bw = o_attn.shape[-1]
    route_spec = pl.BlockSpec((1, 2, tm), lambda bb, i: (bb * nt + i, 0, 0))
    wspec = lambda a: pl.BlockSpec((1,) + a.shape[1:], lay3)
    in_specs = _stream_specs(streams, tm, nc, lambda bb, i: i + off, lambda bb, i: bb)
    in_specs += [pl.BlockSpec((1, tm, bw), qrow),
                 pl.BlockSpec((1, tm, o_lru.shape[-1]), full),
                 pl.BlockSpec((1, tm, bw), qrow),
                 pl.BlockSpec((1, tm, 3 * d), full),
                 pl.BlockSpec((1, 1, N_MOD, d), lambda bb, i: (layer, jnp.where(i + off < nc, 0, bb + 1), 0, 0)),
                 pl.BlockSpec((1, 1, d), lay3),
                 wspec(p["w_ba"]), wspec(p["w_bl"]), wspec(p["w_bm"]), wspec(p["w_out"]),
                 pl.BlockSpec(p["r_w"].shape, const),
                 pl.BlockSpec(p["r_bias"].shape, const)]
    return pl.pallas_call(
        functools.partial(_merge_kernel, n_x=len(streams), nc=nc - off, d=d, n_exp=n_exp),
        grid=(b, nt),
        in_specs=in_specs,
        out_specs=[pl.BlockSpec((1, tm, d), qrow), pl.BlockSpec((1, tm, d), qrow),
                   route_spec, route_spec],
        out_shape=[jax.ShapeDtypeStruct((b, tq, d), F32), jax.ShapeDtypeStruct((b, tq, d), BF16),
                   jax.ShapeDtypeStruct((b * nt, 2, tm), jnp.int32),
                   jax.ShapeDtypeStruct((b * nt, 2, tm), F32)],
        compiler_params=_cparams(("arbitrary", "arbitrary"), V7X_VMEM_LIMIT_BYTES),
        name="merge_router",
    )(*streams, o_attn, o_lru, o_mla, gates, p["mods"], p["norm_ffn"], p["w_ba"], p["w_bl"], p["w_bm"],
      p["w_out"], p["r_w"], p["r_bias"])


def _moe_kernel(h_ref, idr_ref, idc_ref, wtc_ref, low_ref, wg_ref, wu_ref, wd_ref, *rest, per, n_pairs, fuse_final):
    if fuse_final:
        x_ref, m_ref, g_ref, o_ref, xs, ys, cs, posc, meta = rest
    else:
        o_ref, xs, ys, cs, posc, meta = rest
    s = pl.program_id(1)
    tm = h_ref.shape[0]
    slots = xs.shape[0]
    pairs_per_group = per // 2

    @pl.when(s == 0)
    def _sort_tokens():
        e_row = idr_ref[0, 0:1, :]
        e_col = idc_ref[:, 0:1]
        g_row = jnp.zeros(e_row.shape, jnp.int32)
        g_col = jnp.zeros(e_col.shape, jnp.int32)
        for g in range(1, N_GROUPS):
            g_row = g_row + (e_row >= g * per).astype(jnp.int32)
            g_col = g_col + (e_col >= g * per).astype(jnp.int32)
        sub8 = lax.broadcasted_iota(jnp.int32, (V7X_SUBLANES, tm), 0)
        lane = lax.broadcasted_iota(jnp.int32, (tm, V7X_LANES), 1)
        onehot_row = jnp.where(sub8 == g_row, 1.0, 0.0)
        onehot_col = jnp.where(lane == g_col, 1.0, 0.0)
        rank_col = jnp.dot(low_ref[...], onehot_col.astype(BF16), preferred_element_type=F32)
        rank_row = lax.dot_general(onehot_row.astype(BF16), low_ref[...], (((1,), (1,)), ((), ())),
                                   preferred_element_type=F32)
        start = 0
        start_row = jnp.zeros((V7X_SUBLANES, 1), F32)
        start_col = jnp.zeros((1, V7X_LANES), F32)
        for g in range(N_GROUPS):
            cnt = jnp.sum((g_row == g).astype(jnp.int32))
            n_sub = lax.shift_right_logical(cnt + (V7X_LANES - 1), 7)
            meta[g] = start
            meta[N_GROUPS + g] = n_sub
            start_f = jnp.asarray(start, jnp.int32).astype(F32)
            start_row = jnp.where(lax.broadcasted_iota(jnp.int32, start_row.shape, 0) == g, start_f, start_row)
            start_col = jnp.where(lax.broadcasted_iota(jnp.int32, start_col.shape, 1) == g, start_f, start_col)
            start = start + n_sub * V7X_LANES
        pos_row = jnp.sum(onehot_row * (rank_row + start_row), axis=0, keepdims=True).astype(jnp.int32)
        pos_col = jnp.sum(onehot_col * (rank_col + start_col), axis=1, keepdims=True).astype(jnp.int32)
        posc[...] = pos_col
        slot = lax.broadcasted_iota(jnp.int32, (slots, tm), 0)
        perm = jnp.where(slot == pos_row, 1.0, 0.0).astype(BF16)
        xs[...] = jnp.dot(perm, h_ref[...], preferred_element_type=F32).astype(BF16)
        ids_c, wts_c = idc_ref[...], wtc_ref[...]
        comb = jnp.zeros((tm, V7X_LANES), F32)
        for k in range(2):
            local = jnp.bitwise_and(ids_c[:, k:k + 1], per - 1)
            comb = comb + jnp.where(lane == local, wts_c[:, k:k + 1], 0.0)
        c_hi = comb.astype(BF16)
        c_lo = (comb - c_hi.astype(F32)).astype(BF16)
        both = jnp.dot(perm, jnp.concatenate([c_hi, c_lo], axis=-1), preferred_element_type=F32)
        cs[...] = both[:, :V7X_LANES] + both[:, V7X_LANES:]
        ys[...] = jnp.zeros_like(ys)

    grp = s // pairs_per_group
    first_local = (s % pairs_per_group) * 2
    seg_start = meta[grp]
    lane_sub = lax.broadcasted_iota(jnp.int32, (V7X_LANES, V7X_LANES), 1)

    def sub_tile(k, carry):
        r0 = pl.multiple_of(seg_start + k * V7X_LANES, V7X_LANES)
        x = xs[pl.ds(r0, V7X_LANES), :]
        c_all = cs[pl.ds(r0, V7X_LANES), :]
        acts = []
        for jj in range(2):
            c = jnp.sum(jnp.where(lane_sub == first_local + jj, c_all, 0.0), axis=-1, keepdims=True)
            g = jnp.dot(x, wg_ref[0, jj], preferred_element_type=F32)
            u = jnp.dot(x, wu_ref[0, jj], preferred_element_type=F32)
            acts.append(((g * _sigmoid(g)) * u * c).astype(BF16))
        w_down = wd_ref[0].reshape(2 * wd_ref.shape[2], wd_ref.shape[3])
        acc = jnp.dot(jnp.concatenate(acts, axis=-1), w_down, preferred_element_type=F32)
        ys[pl.ds(r0, V7X_LANES), :] = ys[pl.ds(r0, V7X_LANES), :] + acc
        return carry

    lax.fori_loop(0, meta[N_GROUPS + grp], sub_tile, 0)

    @pl.when(s == n_pairs - 1)
    def _unsort():
        slot_l = lax.broadcasted_iota(jnp.int32, (tm, slots), 1)
        perm_t = jnp.where(slot_l == posc[...], 1.0, 0.0).astype(BF16)
        f = jnp.dot(perm_t, ys[...].astype(BF16), preferred_element_type=F32)
        if fuse_final:
            f = _rms(x_ref[...] + m_ref[0, 0, 5:6, :] * f) * g_ref[...]
        o_ref[...] = f


def _moe(h, ids, wts, layer, p, final=None):
    n, d = h.shape
    _, n_exp, _, ff = p["moe_wg"].shape
    per = n_exp // N_GROUPS
    assert per % 2 == 0 and per & (per - 1) == 0
    rt = ids.shape[-1]
    tm = _tile(n, MOE_ROW_TILE)
    assert tm % rt == 0 and tm % V7X_LANES == 0
    slots = tm + N_GROUPS * V7X_LANES
    ids_row = ids.reshape(n // tm, tm // rt, 2, rt).transpose(0, 2, 1, 3).reshape(n // tm, 2, tm)
    ids_col = jnp.transpose(ids, (0, 2, 1)).reshape(n, 2)
    wts_col = jnp.transpose(wts, (0, 2, 1)).reshape(n, 2)
    low = jnp.tril(jnp.ones((tm, tm), BF16), -1)
    n_pairs = n_exp // 2
    wspec = lambda a, b: pl.BlockSpec((1, 2, a, b), lambda i, s: (layer, s, 0, 0))
    in_specs = [pl.BlockSpec((tm, d), lambda i, s: (i, 0)),
                pl.BlockSpec((1, 2, tm), lambda i, s: (i, 0, 0)),
                pl.BlockSpec((tm, 2), lambda i, s: (i, 0)),
                pl.BlockSpec((tm, 2), lambda i, s: (i, 0)),
                pl.BlockSpec((tm, tm), lambda i, s: (0, 0), pipeline_mode=pl.Buffered(1)),
                wspec(d, ff), wspec(d, ff), wspec(ff, d)]
    args = [h, ids_row, ids_col, wts_col, low, p["moe_wg"], p["moe_wu"], p["moe_wd"]]
    if final is not None:
        x, rows_per_batch, gain = final
        assert rows_per_batch % tm == 0
        tiles_per_batch = rows_per_batch // tm
        in_specs += [pl.BlockSpec((tm, d), lambda i, s: (i, 0)),
                     pl.BlockSpec((1, 1, N_MOD, d), lambda i, s: (layer, i // tiles_per_batch + 1, 0, 0)),
                     pl.BlockSpec((1, d), lambda i, s: (0, 0))]
        args += [x, p["mods"], gain.reshape(1, d)]
    return pl.pallas_call(
        functools.partial(_moe_kernel, per=per, n_pairs=n_pairs, fuse_final=final is not None),
        grid=(n // tm, n_pairs),
        in_specs=in_specs,
        out_specs=pl.BlockSpec((tm, d), lambda i, s: (i, 0)),
        out_shape=jax.ShapeDtypeStruct((n, d), F32),
        scratch_shapes=[pltpu.VMEM((slots, d), BF16), pltpu.VMEM((slots, d), F32),
                        pltpu.VMEM((slots, V7X_LANES), F32), pltpu.VMEM((tm, 1), jnp.int32),
                        pltpu.SMEM((2 * N_GROUPS,), jnp.int32)],
        compiler_params=_cparams(("arbitrary", "arbitrary"), V7X_VMEM_LIMIT_BYTES),
        name="moe_ffn",
    )(*args)


def _final_kernel(x_ref, f_ref, m_ref, g_ref, o_ref):
    x = x_ref[0] + m_ref[0, 0, 5:6, :] * f_ref[0]
    o_ref[0] = _rms(x) * g_ref[...]


def _final(x, f, mods, layer, g):
    b, s, d = x.shape
    tm = _tile(s, 2 * ROW_TILE)
    row = lambda bb, i: (bb, i, 0)
    return pl.pallas_call(
        _final_kernel,
        grid=(b, s // tm),
        in_specs=[pl.BlockSpec((1, tm, d), row), pl.BlockSpec((1, tm, d), row),
                  pl.BlockSpec((1, 1, N_MOD, d), lambda bb, i: (layer, bb + 1, 0, 0)),
                  pl.BlockSpec((1, d), lambda bb, i: (0, 0))],
        out_specs=pl.BlockSpec((1, tm, d), row),
        out_shape=jax.ShapeDtypeStruct((b, s, d), F32),
        compiler_params=_cparams(("arbitrary", "arbitrary")),
        name="final_norm",
    )(x, f, mods, g.reshape(1, d))


def _rope_tables(n_ctx, n_lat, dim, period, lane_off, width):
    quarter = dim // 4
    pos = jnp.arange(n_lat, dtype=F32)
    r, col = jnp.floor(pos / GRID_W), pos - GRID_W * jnp.floor(pos / GRID_W)
    inv_freq = ROPE_THETA ** (-jnp.arange(quarter, dtype=F32) / quarter)
    lane = np.arange(width)
    j = (lane % period) - lane_off
    active = (j >= 0) & (j < dim)
    jj = np.where(active, j, 0)
    use_col = jj >= dim // 2
    upper = (jj % (dim // 2)) >= quarter
    f = jj % quarter
    ang = jnp.where(use_col[None, :], col[:, None], r[:, None]) * inv_freq[f][None, :]
    act = jnp.asarray(active)[None, :]
    cos = jnp.where(act, jnp.cos(ang), 1.0)
    sin = jnp.where(act, jnp.sin(ang), 0.0)
    sin_up = jnp.where(jnp.asarray(~upper)[None, :], -sin, 0.0)
    sin_dn = jnp.where(jnp.asarray(upper)[None, :], sin, 0.0)
    ident = lambda v, fill: jnp.concatenate([jnp.full((n_ctx, width), fill, F32), v], axis=0)
    return ident(cos, 1.0), ident(sin_up, 0.0), ident(sin_dn, 0.0)


def _lru_gate_weights(w_a, w_i):
    depth, _, nb, k, j = w_a.shape
    w = jnp.stack([w_a, w_i], axis=2)
    eye = jnp.eye(nb, dtype=w.dtype)
    dense = jnp.einsum('dsgnkj,nm->dnksgmj', w, eye)
    return (0.5 * dense).reshape(depth, nb * k, 4 * nb * j).astype(BF16)


def kernel(x, c, ctx, c_ctx, w_mod, b_mod, norm_mix, norm_ffn, w_in, gqa_q_norm, gqa_k_norm, conv_w, conv_b,
           lru_w_a, lru_b_a, lru_w_i, lru_b_i, lru_lam, mla_q_a_norm, mla_w_qb, mla_kv_a_norm, mla_w_kvb,
           w_branch_attn, w_branch_lru, w_branch_mla, w_out, router_w, router_bias,
           moe_w_gate, moe_w_up, moe_w_down, final_norm):
    bsz, n_lat, d = x.shape
    n_ctx = ctx.shape[1]
    depth = w_mod.shape[0]
    hd = gqa_q_norm.shape[-1]
    lru_w = conv_w.shape[-1]
    q_lora, kv_lora = mla_q_a_norm.shape[-1], mla_kv_a_norm.shape[-1]
    n_exp = router_w.shape[-1]
    assert bsz + 1 <= MOD_ROWS and d % V7X_LANES == 0

    cc = jnp.zeros((MOD_ROWS, d), F32).at[0].set(c_ctx).at[1:1 + bsz].set(c)
    mods = _modulation(cc, w_mod, b_mod).reshape(depth, MOD_ROWS, N_MOD, d)

    mq_period = MLA_QK * V7X_LANES // math.gcd(MLA_QK, V7X_LANES)
    kr_end = GQA_HEADS * hd + 2 * GQA_KV_HEADS * hd + 2 * lru_w + q_lora + kv_lora + MLA_ROPE
    w_in_b = _pack_w_in(w_in, kr_end)
    r_pad = jnp.pad(router_w, ((0, 0), (0, V7X_LANES - n_exp)))
    r_hi = r_pad.astype(BF16)
    r_w = jnp.concatenate([r_hi, r_hi, (r_pad - r_hi.astype(F32)).astype(BF16)], axis=0)
    p = dict(
        hd=hd, lru_w=lru_w, q_lora=q_lora, kv_lora=kv_lora, mods=mods,
        norm_mix=norm_mix.reshape(depth, 1, d), norm_ffn=norm_ffn.reshape(depth, 1, d), w_in=w_in_b,
        gq=jnp.tile(gqa_q_norm, (1, GQA_HEADS)).reshape(depth, 1, -1),
        gk=jnp.tile(gqa_k_norm, (1, GQA_KV_HEADS)).reshape(depth, 1, -1),
        gqa_tabs=_rope_tables(n_ctx, n_lat, hd, hd, 0, V7X_LANES),
        mla_q_tabs=_rope_tables(n_ctx, n_lat, MLA_ROPE, MLA_QK, MLA_NOPE, mq_period),
        mla_k_tabs=_rope_tables(n_ctx, n_lat, MLA_ROPE, V7X_LANES, 0, V7X_LANES),
        mla_q_norm=mla_q_a_norm.reshape(depth, 1, q_lora), mla_kv_norm=mla_kv_a_norm.reshape(depth, 1, kv_lora),
        w_qb=mla_w_qb.astype(BF16), w_kvb=mla_w_kvb.astype(BF16),
        conv_w=conv_w, conv_b=conv_b.reshape(depth, 1, lru_w),
        lru_wg=_lru_gate_weights(lru_w_a, lru_w_i),
        lru_bg=0.5 * jnp.stack([lru_b_a, lru_b_i], axis=2).reshape(depth, 1, 4 * lru_w),
        lru_lam=lru_lam,
        w_ba=w_branch_attn.astype(BF16), w_bl=w_branch_lru.astype(BF16), w_bm=w_branch_mla.astype(BF16),
        w_out=w_out.astype(BF16), r_w=r_w,
        r_bias=router_bias.reshape(n_exp, 1).astype(F32),
        moe_wg=moe_w_gate.astype(BF16), moe_wu=moe_w_up.astype(BF16), moe_wd=moe_w_down.astype(BF16),
    )

    streams = (ctx, x)
    prev = None
    for layer in range(depth):
        last = layer == depth - 1
        streams, (q_g, k_g, v_g, zlru, q_m, k_m, v_m, gates) = _in_proj(streams, prev, layer, p, n_ctx)
        o_attn = _attention(q_g, k_g, v_g, gqa=True, n_ctx=n_ctx, with_ctx=not last, name="gqa_attention")
        o_lru = _lru(zlru, layer, p, n_ctx)
        o_mla = _attention(q_m, k_m, v_m, gqa=False, n_ctx=n_ctx, with_ctx=not last, name="mla_attention")
        x_mid, h2, ids, wts = _merge(streams, o_attn, o_lru, o_mla, gates, layer, p, n_ctx, with_ctx=not last)

        n_tok = x_mid.shape[0] * x_mid.shape[1]
        rows = x_mid.shape[1]
        if last and rows % _tile(n_tok, MOE_ROW_TILE) == 0:
            return _moe(h2.reshape(n_tok, d), ids, wts, layer, p,
                        final=(x_mid.reshape(n_tok, d), rows, final_norm)).reshape(x_mid.shape)
        f = _moe(h2.reshape(n_tok, d), ids, wts, layer, p).reshape(x_mid.shape)
        streams, prev = (x_mid,), (f, layer)

    return _final(streams[0], prev[0], mods, prev[1], final_norm)
```
